```python
import numpy as np
import jax, jax.numpy as jnp
from jax import lax

D_MODEL = 1024
BATCH = 8
SEQ = 8192
DEPTH = 4
DEC_BATCH = 2
DEC_SEQ = 8192
PAST_LEN = 128

GRID_W = 64
PLE_DIM = 256
HEAD_DIM = 64
ROPE_THETA = 10000.0
Q_BLOCK = 128
EPS = 1e-6
A_HEADS = 6
A_KV_HEADS = 2
A_GROUP = A_HEADS // A_KV_HEADS
A_WIDTH = A_HEADS * HEAD_DIM
B_GROUPS = 4
B_GROUP_DIM = 64
B_WINDOWS = (2, 4, 8, 16)
B_WIDTH = B_GROUPS * B_GROUP_DIM
C_HEADS = 6
C_NOPE = 64
C_ROPE = 32
C_V = 64
C_Q_RANK = 256
C_KV_RANK = 128
C_WIDTH = C_HEADS * C_V
MIX_WIDTH = A_WIDTH + B_WIDTH + C_WIDTH
IN_SPLITS = (A_WIDTH, A_KV_HEADS * HEAD_DIM, A_KV_HEADS * HEAD_DIM, B_WIDTH, C_Q_RANK, C_KV_RANK, C_ROPE)
IN_COLS = sum(IN_SPLITS)
MOE_GROUPS = 4
EXPERTS_PER_GROUP = 8
N_EXPERTS = MOE_GROUPS * EXPERTS_PER_GROUP
TOP_K = 2
D_EXPERT = 256
MOE_CHUNK = 128

kernel_name = 'hybrid_gqa_pool_mla_hmoe_encoder'


def rmsnorm(x, g):
    xf = x.astype(jnp.float32)
    y = xf * lax.rsqrt(jnp.mean(xf * xf, axis=-1, keepdims=True) + EPS) * g.astype(jnp.float32)
    return y.astype(x.dtype)


def grid_angles(seq_len, rot_dim):
    rows_n = seq_len // GRID_W
    row = jnp.repeat(jnp.arange(rows_n), GRID_W).astype(jnp.float32)
    col = jnp.tile(jnp.arange(GRID_W), rows_n).astype(jnp.float32)
    n_freq = rot_dim // 4
    inv = ROPE_THETA ** (-jnp.arange(n_freq, dtype=jnp.float32) / n_freq)
    ang = jnp.stack([row[:, None] * inv, col[:, None] * inv], axis=1)
    return jnp.cos(ang), jnp.sin(ang)


def apply_rope_2d(x, cos, sin):
    B, S, H, d = x.shape
    xr = x.astype(jnp.float32).reshape(B, S, H, 2, 2, d // 4)
    x1, x2 = xr[..., 0, :], xr[..., 1, :]
    c = cos[None, :, None]
    s = sin[None, :, None]
    out = jnp.stack([x1 * c - x2 * s, x2 * c + x1 * s], axis=-2)
    return out.reshape(B, S, H, d).astype(x.dtype)


def block_attention(q, k, v, scale):
    B, S, KV, G, dk = q.shape
    nb = S // Q_BLOCK
    qb = jnp.moveaxis(q.reshape(B, nb, Q_BLOCK, KV, G, dk), 1, 0)

    def one_block(qblk):
        s = jnp.einsum('bqkgd,bskd->bkgqs', qblk, k).astype(jnp.float32) * scale
        p = jax.nn.softmax(s, axis=-1).astype(v.dtype)
        return jnp.einsum('bkgqs,bskd->bqkgd', p, v)

    o = lax.map(one_block, qb)
    return jnp.moveaxis(o, 0, 1).reshape(B, S, KV * G * v.shape[-1])


def pool_mixer(u, w_pool, s_pool):
    B, S, _ = u.shape
    uf = u.astype(jnp.float32).reshape(B, S, B_GROUPS, B_GROUP_DIM)
    cs = jnp.concatenate([jnp.zeros((B, 1, B_GROUPS, B_GROUP_DIM), jnp.float32), jnp.cumsum(uf, axis=1)], axis=1)
    t = np.arange(S)[:, None]
    half = np.array(B_WINDOWS)[None, :] // 2
    lo = np.clip(t - half, 0, S)
    hi = np.clip(t + half, 0, S)
    cnt = jnp.asarray((hi - lo).astype(np.float32))
    gidx = np.arange(B_GROUPS)[None, :]
    mean = (cs[:, hi, gidx] - cs[:, lo, gidx]) / cnt[None, :, :, None]
    d = mean - uf
    y = jnp.einsum('bsgc,gce->bsge', d, w_pool.astype(jnp.float32)) * s_pool.astype(jnp.float32).reshape(B_GROUPS, B_GROUP_DIM)
    return y.reshape(B, S, B_WIDTH).astype(u.dtype)


def mixing(a, lw, cos_a, sin_a, cos_c, sin_c):
    B, S, _ = a.shape
    z = a @ lw['w_in']
    cuts = [int(c) for c in np.cumsum(IN_SPLITS)[:-1]]
    qa, ka, va, ub, cq, ckv, kr = jnp.split(z, cuts, axis=-1)
    qa = apply_rope_2d(rmsnorm(qa.reshape(B, S, A_HEADS, HEAD_DIM), lw['g_qa']), cos_a, sin_a)
    ka = apply_rope_2d(rmsnorm(ka.reshape(B, S, A_KV_HEADS, HEAD_DIM), lw['g_ka']), cos_a, sin_a)
    va = va.reshape(B, S, A_KV_HEADS, HEAD_DIM)
    out_a = block_attention(qa.reshape(B, S, A_KV_HEADS, A_GROUP, HEAD_DIM), ka, va, HEAD_DIM ** -0.5)
    out_b = pool_mixer(ub, lw['w_pool'], lw['s_pool'])
    qc = (rmsnorm(cq, lw['g_cq']) @ lw['w_q_up']).reshape(B, S, C_HEADS, C_NOPE + C_ROPE)
    q_nope = qc[..., :C_NOPE]
    q_rope = apply_rope_2d(qc[..., C_NOPE:], cos_c, sin_c)
    kv = (rmsnorm(ckv, lw['g_ckv']) @ lw['w_kv_up']).reshape(B, S, C_HEADS, C_NOPE + C_V)
    k_nope = kv[..., :C_NOPE]
    vc = kv[..., C_NOPE:]
    k_rope = apply_rope_2d(kr.reshape(B, S, 1, C_ROPE), cos_c, sin_c)
    qc = jnp.concatenate([q_nope, q_rope], axis=-1)
    kc = jnp.concatenate([k_nope, jnp.broadcast_to(k_rope, (B, S, C_HEADS, C_ROPE))], axis=-1)
    out_c = block_attention(qc[:, :, :, None, :], kc, vc, (C_NOPE + C_ROPE) ** -0.5)
    merged = jnp.concatenate([rmsnorm(out_a, lw['g_out_a']), rmsnorm(out_b, lw['g_out_b']), rmsnorm(out_c, lw['g_out_c'])], axis=-1)
    return merged @ lw['w_out']


def hier_moe(m, w_router_group, w_router_expert, w_gate, w_up, w_down):
    B, S, D = m.shape
    T = B * S
    mt = m.reshape(T, D)
    mf = mt.astype(jnp.float32)
    pg = jax.nn.softmax(mf @ w_router_group.astype(jnp.float32), axis=-1)
    gsel = jnp.argmax(pg, axis=-1)
    gprob = jnp.take_along_axis(pg, gsel[:, None], axis=-1)
    le = (mf @ w_router_expert.astype(jnp.float32)).reshape(T, MOE_GROUPS, EXPERTS_PER_GROUP)
    le = jnp.take_along_axis(le, gsel[:, None, None], axis=1)[:, 0]
    pe = jax.nn.softmax(le, axis=-1)
    topv, topi = lax.top_k(pe, TOP_K)
    topv = topv / jnp.sum(topv, axis=-1, keepdims=True)
    w_grp = jnp.sum(topv[..., None] * jax.nn.one_hot(topi, EXPERTS_PER_GROUP, dtype=jnp.float32), axis=1) * gprob
    gates = (jax.nn.one_hot(gsel, MOE_GROUPS, dtype=jnp.float32)[:, :, None] * w_grp[:, None, :]).reshape(T, N_EXPERTS)

    def chunk(args):
        xc, gc = args
        g = jnp.einsum('cd,edf->cef', xc, w_gate)
        u = jnp.einsum('cd,edf->cef', xc, w_up)
        hid = jax.nn.silu(g) * u * gc[:, :, None].astype(xc.dtype)
        return jnp.einsum('cef,efd->cd', hid, w_down)

    y = lax.map(chunk, (mt.reshape(T // MOE_CHUNK, MOE_CHUNK, D), gates.reshape(T // MOE_CHUNK, MOE_CHUNK, N_EXPERTS)))
    return y.reshape(B, S, D)


def trunk(x, p, params, g_final):
    S = x.shape[1]
    cos_a, sin_a = grid_angles(S, HEAD_DIM)
    cos_c, sin_c = grid_angles(S, C_ROPE)
    h = x
    for i in range(DEPTH):
        lw = {name: w[i] for name, w in params.items()}
        h = h + mixing(rmsnorm(h, lw['g_mix']), lw, cos_a, sin_a, cos_c, sin_c)
        h = h + hier_moe(rmsnorm(h, lw['g_ffn']), lw['w_router_group'], lw['w_router_expert'], lw['w_gate'], lw['w_up'], lw['w_down'])
        gate = jax.nn.sigmoid((rmsnorm(h, lw['g_ple']) @ lw['w_ple_gate']).astype(jnp.float32))
        e = (p[i] @ lw['w_ple_proj']).astype(jnp.float32)
        h = h + (gate * e).astype(h.dtype)
    return rmsnorm(h, g_final)


def setup_inputs(seed: int = 0) -> dict:
    key = jax.random.key(seed)
    ks = jax.random.split(key, 28)
    f32 = jnp.float32

    def nrm(k, shape, scale):
        return jax.random.normal(k, shape, f32) * scale

    def gain(k, shape):
        return 1.0 + 0.02 * jax.random.normal(k, shape, f32)

    L = DEPTH
    return {
        'x_prompt': nrm(ks[0], (BATCH, SEQ, D_MODEL), 1.0),
        'x_sample': nrm(ks[1], (DEC_BATCH, DEC_SEQ, D_MODEL), 1.0),
        'p_prompt': nrm(ks[2], (DEPTH, BATCH, SEQ, PLE_DIM), 1.0),
        'p_sample': nrm(ks[3], (DEPTH, DEC_BATCH, DEC_SEQ, PLE_DIM), 1.0),
        'g_mix': gain(ks[4], (L, D_MODEL)),
        'w_in': nrm(ks[5], (L, D_MODEL, IN_COLS), D_MODEL ** -0.5),
        'g_qa': gain(ks[6], (L, HEAD_DIM)),
        'g_ka': gain(ks[7], (L, HEAD_DIM)),
        'w_pool': nrm(ks[8], (L, B_GROUPS, B_GROUP_DIM, B_GROUP_DIM), B_GROUP_DIM ** -0.5),
        's_pool': gain(ks[9], (L, B_WIDTH)),
        'g_cq': gain(ks[10], (L, C_Q_RANK)),
        'w_q_up': nrm(ks[11], (L, C_Q_RANK, C_HEADS * (C_NOPE + C_ROPE)), C_Q_RANK ** -0.5),
        'g_ckv': gain(ks[12], (L, C_KV_RANK)),
        'w_kv_up': nrm(ks[13], (L, C_KV_RANK, C_HEADS * (C_NOPE + C_V)), C_KV_RANK ** -0.5),
        'g_out_a': gain(ks[14], (L, A_WIDTH)),
        'g_out_b': gain(ks[15], (L, B_WIDTH)),
        'g_out_c': gain(ks[16], (L, C_WIDTH)),
        'w_out': nrm(ks[17], (L, MIX_WIDTH, D_MODEL), MIX_WIDTH ** -0.5),
        'g_ffn': gain(ks[18], (L, D_MODEL)),
        'w_router_group': nrm(ks[19], (L, D_MODEL, MOE_GROUPS), D_MODEL ** -0.5),
        'w_router_expert': nrm(ks[20], (L, D_MODEL, N_EXPERTS), D_MODEL ** -0.5),
        'w_gate': nrm(ks[21], (L, N_EXPERTS, D_MODEL, D_EXPERT), D_MODEL ** -0.5),
        'w_up': nrm(ks[22], (L, N_EXPERTS, D_MODEL, D_EXPERT), D_MODEL ** -0.5),
        'w_down': nrm(ks[23], (L, N_EXPERTS, D_EXPERT, D_MODEL), D_EXPERT ** -0.5),
        'g_ple': gain(ks[24], (L, D_MODEL)),
        'w_ple_gate': nrm(ks[25], (L, D_MODEL, D_MODEL), D_MODEL ** -0.5),
        'w_ple_proj': nrm(ks[26], (L, PLE_DIM, D_MODEL), PLE_DIM ** -0.5),
        'g_final': gain(ks[27], (D_MODEL,)),
    }


def reference(x_prompt, x_sample, p_prompt, p_sample, g_mix, w_in, g_qa, g_ka, w_pool, s_pool, g_cq, w_q_up, g_ckv, w_kv_up, g_out_a, g_out_b, g_out_c, w_out, g_ffn, w_router_group, w_router_expert, w_gate, w_up, w_down, g_ple, w_ple_gate, w_ple_proj, g_final):
    params = dict(g_mix=g_mix, w_in=w_in, g_qa=g_qa, g_ka=g_ka, w_pool=w_pool, s_pool=s_pool, g_cq=g_cq, w_q_up=w_q_up, g_ckv=g_ckv, w_kv_up=w_kv_up, g_out_a=g_out_a, g_out_b=g_out_b, g_out_c=g_out_c, w_out=w_out, g_ffn=g_ffn, w_router_group=w_router_group, w_router_expert=w_router_expert, w_gate=w_gate, w_up=w_up, w_down=w_down, g_ple=g_ple, w_ple_gate=w_ple_gate, w_ple_proj=w_ple_proj)
    y_prompt = trunk(x_prompt, p_prompt, params, g_final)
    y_sample = trunk(x_sample, p_sample, params, g_final)
    return (y_prompt, y_sample)
```

```python
import functools
import math

import numpy as np
import jax
import jax.numpy as jnp
from jax import lax
from jax.experimental import pallas as pl
from jax.experimental.pallas import tpu as pltpu

F32 = jnp.float32
BF16 = jnp.bfloat16

D_MODEL = 1024
DEPTH = 4
GRID_W = 64
PLE_DIM = 256
HEAD_DIM = 64
ROPE_THETA = 10000.0
EPS = 1e-6
A_HEADS = 6
A_KV_HEADS = 2
A_WIDTH = A_HEADS * HEAD_DIM
B_GROUPS = 4
B_GROUP_DIM = 64
B_WINDOWS = (2, 4, 8, 16)
B_WIDTH = B_GROUPS * B_GROUP_DIM
C_HEADS = 6
C_NOPE = 64
C_ROPE = 32
C_V = 64
C_Q_RANK = 256
C_KV_RANK = 128
C_WIDTH = C_HEADS * C_V
C_QK_PAD = 128
MOE_GROUPS = 4
EXPERTS_PER_GROUP = 8
N_EXPERTS = MOE_GROUPS * EXPERTS_PER_GROUP
D_EXPERT = 256
ROUTER_ROWS = 40
LOG2E = math.log2(math.e)

LANES = 128
VMEM_LIMIT_BYTES = 56 * 1024 * 1024

TOKEN_TILE = 512
Q_TILE = 256
MOE_TILE = 1024
POOL_HALO = 8

_QK_A = A_WIDTH + A_KV_HEADS * HEAD_DIM
_V_A0 = _QK_A
_U_B0 = _V_A0 + A_KV_HEADS * HEAD_DIM
_CQ0 = _U_B0 + B_WIDTH
_CKV0 = _CQ0 + C_Q_RANK
_KR0 = _CKV0 + C_KV_RANK
IN_COLS_PACKED = _KR0 + C_QK_PAD


def _params(sem):
    return pltpu.CompilerParams(dimension_semantics=sem, vmem_limit_bytes=VMEM_LIMIT_BYTES)


def _rms(x):
    return x * lax.rsqrt(jnp.mean(x * x, axis=-1, keepdims=True) + EPS)


def _swap_halves(x, half):
    n = x.shape[-1]
    lane = lax.broadcasted_iota(jnp.int32, x.shape, x.ndim - 1)
    return jnp.where((lane & half) == 0, pltpu.roll(x, n - half, x.ndim - 1), pltpu.roll(x, half, x.ndim - 1))


def _premix_kernel(h_ref, g_mix_ref, w_in_ref, hsum_ref, g_qk_ref, cosa_ref, sina_ref,
                   g_cq_ref, w_qup_ref, g_ckv_ref, w_kvup_ref, cosc_ref, sinc_ref,
                   qa_ref, ka_ref, vat_ref, ub_ref, qc_ref, kc_ref, vct_ref, *, c_scale):
    a = _rms(h_ref[0]) * g_mix_ref[...]
    z = jnp.dot(a.astype(BF16), w_in_ref[...], preferred_element_type=F32)

    qk = z[:, :_QK_A]
    ss = jnp.dot((qk * qk).astype(BF16), hsum_ref[...], preferred_element_type=F32)
    qk = qk * lax.rsqrt(ss * (1.0 / HEAD_DIM) + EPS) * g_qk_ref[...]
    reps = _QK_A // LANES
    cos_a = jnp.concatenate([cosa_ref[...]] * reps, axis=1)
    sin_a = jnp.concatenate([sina_ref[...]] * reps, axis=1)
    qk = (qk * cos_a + _swap_halves(qk, HEAD_DIM // 4) * sin_a).astype(BF16)
    for hh in range(A_HEADS):
        qa_ref[0, hh] = qk[:, hh * HEAD_DIM:(hh + 1) * HEAD_DIM]
    for hh in range(A_KV_HEADS):
        ka_ref[0, hh] = qk[:, A_WIDTH + hh * HEAD_DIM:A_WIDTH + (hh + 1) * HEAD_DIM]
    vt = z[:, _V_A0:_U_B0].T.astype(BF16)
    for hh in range(A_KV_HEADS):
        vat_ref[0, hh, 0] = vt[hh * HEAD_DIM:(hh + 1) * HEAD_DIM]

    ub_ref[0] = z[:, _U_B0:_CQ0]

    cqn = _rms(z[:, _CQ0:_CKV0]) * g_cq_ref[...]
    qc = jnp.dot(cqn.astype(BF16), w_qup_ref[...], preferred_element_type=F32)
    cos_c = cosc_ref[...]
    sin_c = sinc_ref[...]
    cos_q = jnp.concatenate([cos_c * c_scale] * C_HEADS, axis=1)
    sin_q = jnp.concatenate([sin_c * c_scale] * C_HEADS, axis=1)
    qc = (qc * cos_q + _swap_halves(qc, C_ROPE // 4) * sin_q).astype(BF16)
    ckvn = _rms(z[:, _CKV0:_KR0]) * g_ckv_ref[...]
    kv = jnp.dot(ckvn.astype(BF16), w_kvup_ref[...], preferred_element_type=F32)
    kr = z[:, _KR0:IN_COLS_PACKED]
    kr = kr * cos_c + _swap_halves(kr, C_ROPE // 4) * sin_c
    for hh in range(C_HEADS):
        qc_ref[0, hh] = qc[:, hh * C_QK_PAD:(hh + 1) * C_QK_PAD]
        kc_ref[0, hh] = (kv[:, hh * C_QK_PAD:(hh + 1) * C_QK_PAD] + kr).astype(BF16)
    vct = kv[:, C_HEADS * C_QK_PAD:].T.astype(BF16)
    for hh in range(C_HEADS):
        vct_ref[0, hh, 0] = vct[hh * C_V:(hh + 1) * C_V]


def _premix(h, lw, tabs):
    B, S, D = h.shape
    Tt = TOKEN_TILE
    nt = S // Tt
    tile = lambda b, i: (b, i, 0)
    const2 = lambda b, i: (0, 0)
    headmajor = lambda b, i: (b, 0, i, 0)
    vt_map = lambda b, i: (b, 0, i, 0, 0)
    tab_map = lambda b, i: (i, 0)
    in_specs = [
        pl.BlockSpec((1, Tt, D), tile),
        pl.BlockSpec((1, D), const2),
        pl.BlockSpec((D, IN_COLS_PACKED), const2),
        pl.BlockSpec((_QK_A, _QK_A), const2),
        pl.BlockSpec((1, _QK_A), const2),
        pl.BlockSpec((Tt, LANES), tab_map),
        pl.BlockSpec((Tt, LANES), tab_map),
        pl.BlockSpec((1, C_Q_RANK), const2),
        pl.BlockSpec((C_Q_RANK, C_HEADS * C_QK_PAD), const2),
        pl.BlockSpec((1, C_KV_RANK), const2),
        pl.BlockSpec((C_KV_RANK, C_HEADS * C_QK_PAD + C_WIDTH), const2),
        pl.BlockSpec((Tt, LANES), tab_map),
        pl.BlockSpec((Tt, LANES), tab_map),
    ]
    out_shape = [
        jax.ShapeDtypeStruct((B, A_HEADS, S, HEAD_DIM), BF16),
        jax.ShapeDtypeStruct((B, A_KV_HEADS, S, HEAD_DIM), BF16),
        jax.ShapeDtypeStruct((B, A_KV_HEADS, nt, HEAD_DIM, Tt), BF16),
        jax.ShapeDtypeStruct((B, S, B_WIDTH), F32),
        jax.ShapeDtypeStruct((B, C_HEADS, S, C_QK_PAD), BF16),
        jax.ShapeDtypeStruct((B, C_HEADS, S, C_QK_PAD), BF16),
        jax.ShapeDtypeStruct((B, C_HEADS, nt, C_V, Tt), BF16),
    ]
    out_specs = [
        pl.BlockSpec((1, A_HEADS, Tt, HEAD_DIM), headmajor),
        pl.BlockSpec((1, A_KV_HEADS, Tt, HEAD_DIM), headmajor),
        pl.BlockSpec((1, A_KV_HEADS, 1, HEAD_DIM, Tt), vt_map),
        pl.BlockSpec((1, Tt, B_WIDTH), tile),
        pl.BlockSpec((1, C_HEADS, Tt, C_QK_PAD), headmajor),
        pl.BlockSpec((1, C_HEADS, Tt, C_QK_PAD), headmajor),
        pl.BlockSpec((1, C_HEADS, 1, C_V, Tt), vt_map),
    ]
    kern = functools.partial(_premix_kernel, c_scale=(C_NOPE + C_ROPE) ** -0.5 * LOG2E)
    return pl.pallas_call(
        kern, grid=(B, nt), in_specs=in_specs, out_specs=out_specs, out_shape=out_shape,
        compiler_params=_params(("parallel", "parallel")), name="premix",
    )(h, lw["g_mix"], lw["w_in"], tabs["hsum"], lw["g_qk"], tabs["cos_a"], tabs["sin_a"],
      lw["g_cq"], lw["w_q_up"], lw["g_ckv"], lw["w_kv_up"], tabs["cos_c"], tabs["sin_c"])


def _attn_kernel(q_ref, k_ref, vt_ref, o_ref, *, heads, group, n_chunks, chunk):
    tq = q_ref.shape[2]
    dv = vt_ref.shape[3]

    def one_head(hh):
        kvh = hh // group
        q = q_ref[0, hh]

        def body(j, carry):
            m, l, acc = carry
            start = pl.multiple_of(j * chunk, chunk)
            k = k_ref[0, kvh, pl.ds(start, chunk), :]
            s = lax.dot_general(k, q, (((1,), (1,)), ((), ())), preferred_element_type=F32)
            m_new = jnp.maximum(m, jnp.max(s, axis=0, keepdims=True))
            alpha = jnp.exp2(m - m_new)
            p = jnp.exp2(s - m_new)
            l = alpha * l + jnp.sum(p, axis=0, keepdims=True)
            pv = jnp.dot(vt_ref[0, kvh, j], p.astype(BF16), preferred_element_type=F32)
            return m_new, l, alpha * acc + pv

        init = (jnp.full((1, tq), -jnp.inf, F32), jnp.zeros((1, tq), F32), jnp.zeros((dv, tq), F32))
        _, l, acc = lax.fori_loop(0, n_chunks, body, init)
        return acc * (1.0 / l)

    per_store = LANES // dv
    for h0 in range(0, heads, per_store):
        outs = [one_head(h0 + d) for d in range(per_store)]
        o_ref[0, :, h0 * dv:(h0 + per_store) * dv] = jnp.concatenate(outs, axis=0).T


def _attention(q, k, vt, *, heads_per_step, group):
    B, H, S, dk = q.shape
    Hkv = k.shape[1]
    nk, dv, Tk = vt.shape[2:]
    Tq = Q_TILE
    nsteps = H // heads_per_step
    kv_per_step = heads_per_step // group
    kern = functools.partial(_attn_kernel, heads=heads_per_step, group=group, n_chunks=nk, chunk=Tk)
    return pl.pallas_call(
        kern, grid=(B, nsteps, S // Tq),
        in_specs=[
            pl.BlockSpec((1, heads_per_step, Tq, dk), lambda b, g, i: (b, g, i, 0)),
            pl.BlockSpec((1, kv_per_step, S, dk), lambda b, g, i: (b, g, 0, 0)),
            pl.BlockSpec((1, kv_per_step, nk, dv, Tk), lambda b, g, i: (b, g, 0, 0, 0)),
        ],
        out_specs=pl.BlockSpec((1, Tq, heads_per_step * dv), lambda b, g, i: (b, i, g)),
        out_shape=jax.ShapeDtypeStruct((B, S, H * dv), F32),
        compiler_params=_params(("parallel", "parallel", "arbitrary")), name="attention",
    )(q, k, vt)


def _route(logits):
    tt = logits.shape[1]
    le = logits[:N_EXPERTS]
    lg = logits[N_EXPERTS:]
    row8 = lax.broadcasted_iota(jnp.int32, (ROUTER_ROWS - N_EXPERTS, tt), 0)
    lg = jnp.where(row8 < MOE_GROUPS, lg, -jnp.inf)
    gmax = jnp.max(lg, axis=0, keepdims=True)
    gsel = jnp.min(jnp.where(lg == gmax, row8, MOE_GROUPS), axis=0, keepdims=True)
    gprob = 1.0 / jnp.sum(jnp.exp(lg - gmax), axis=0, keepdims=True)
    row = lax.broadcasted_iota(jnp.int32, (N_EXPERTS, tt), 0)
    lm = jnp.where((row // EXPERTS_PER_GROUP) == gsel, le, -jnp.inf)
    m1 = jnp.max(lm, axis=0, keepdims=True)
    i1 = jnp.min(jnp.where(lm == m1, row, N_EXPERTS), axis=0, keepdims=True)
    lm2 = jnp.where(row == i1, -jnp.inf, lm)
    m2 = jnp.max(lm2, axis=0, keepdims=True)
    i2 = jnp.min(jnp.where(lm2 == m2, row, N_EXPERTS), axis=0, keepdims=True)
    r = jnp.exp(m2 - m1)
    w1 = gprob / (1.0 + r)
    w2 = w1 * r
    return jnp.where(row == i1, w1, 0.0) + jnp.where(row == i2, w2, 0.0)


def _postmix_kernel(h_ref, oa_ref, oc_ref, ub_ref, prev_ref, next_ref, wpool_ref, spool_ref,
                    ga_ref, gb_ref, gc_ref, wout_ref, gffn_ref, wr_hi_ref, wr_lo_ref,
                    h1_ref, m_ref, gates_ref, ext_ref, *, seq_len):
    i = pl.program_id(1)
    nt = pl.num_programs(1)
    tt = h_ref.shape[1]
    half = B_WIDTH // 2

    u = ub_ref[0]
    ext_ref[POOL_HALO:POOL_HALO + tt, :] = u
    ext_ref[:POOL_HALO, :] = jnp.where(i > 0, prev_ref[0, 0], 0.0)
    ext_ref[POOL_HALO + tt:, :] = jnp.where(i < nt - 1, next_ref[0, 0], 0.0)

    def window(lo, hi, lanes):
        acc = None
        for d in range(lo, hi):
            piece = ext_ref[POOL_HALO + d:POOL_HALO + d + tt, lanes]
            acc = piece if acc is None else acc + piece
        return acc

    lo_lanes = slice(0, half)
    hi_lanes = slice(half, B_WIDTH)
    w2 = window(-1, 1, lo_lanes)
    w4 = w2 + window(-2, -1, lo_lanes) + window(1, 2, lo_lanes)
    w8 = window(-4, 4, hi_lanes)
    w16 = w8 + window(-8, -4, hi_lanes) + window(4, 8, hi_lanes)
    lane = lax.broadcasted_iota(jnp.int32, (tt, half), 1)
    first = lane < B_GROUP_DIM
    sums = jnp.concatenate([jnp.where(first, w2, w4), jnp.where(first, w8, w16)], axis=1)
    pos = i * tt + lax.broadcasted_iota(jnp.int32, (tt, B_WIDTH), 0)
    lane_b = lax.broadcasted_iota(jnp.int32, (tt, B_WIDTH), 1)
    hw = jnp.left_shift(1, lane_b // B_GROUP_DIM)
    cnt = jnp.minimum(pos + hw, seq_len) - jnp.maximum(pos - hw, 0)
    dlt = sums / cnt.astype(F32) - u
    ob = jnp.dot(dlt.astype(BF16), wpool_ref[...], preferred_element_type=F32) * spool_ref[...]

    merged = jnp.concatenate([
        (_rms(oa_ref[0]) * ga_ref[...]).astype(BF16),
        (_rms(ob) * gb_ref[...]).astype(BF16),
        (_rms(oc_ref[0]) * gc_ref[...]).astype(BF16)], axis=1)
    h1 = h_ref[0] + jnp.dot(merged, wout_ref[...], preferred_element_type=F32)
    h1_ref[0] = h1

    m = _rms(h1) * gffn_ref[...]
    m_hi = m.astype(BF16)
    m_ref[0] = m_hi
    m_lo = (m - m_hi.astype(F32)).astype(BF16)
    nt_dims = (((1,), (1,)), ((), ()))
    logits = (lax.dot_general(wr_hi_ref[...], m_hi, nt_dims, preferred_element_type=F32)
              + lax.dot_general(wr_lo_ref[...], m_hi, nt_dims, preferred_element_type=F32)
              + lax.dot_general(wr_hi_ref[...], m_lo, nt_dims, preferred_element_type=F32))
    gates = _route(logits)
    gates = jnp.concatenate([gates, jnp.zeros((LANES - N_EXPERTS, tt), F32)], axis=0)
    gates_ref[0] = gates.T


def _postmix(h, oa, oc, ub, lw):
    B, S, D = h.shape
    Tt = TOKEN_TILE
    nt = S // Tt
    rows = Tt // POOL_HALO
    ub_rows = ub.reshape(B, S // POOL_HALO, POOL_HALO, B_WIDTH)
    tile = lambda b, i: (b, i, 0)
    const2 = lambda b, i: (0, 0)
    in_specs = [
        pl.BlockSpec((1, Tt, D), tile),
        pl.BlockSpec((1, Tt, A_WIDTH), tile),
        pl.BlockSpec((1, Tt, C_WIDTH), tile),
        pl.BlockSpec((1, Tt, B_WIDTH), tile),
        pl.BlockSpec((1, 1, POOL_HALO, B_WIDTH), lambda b, i: (b, jnp.maximum(i * rows - 1, 0), 0, 0)),
        pl.BlockSpec((1, 1, POOL_HALO, B_WIDTH),
                     lambda b, i: (b, jnp.minimum((i + 1) * rows, S // POOL_HALO - 1), 0, 0)),
        pl.BlockSpec((B_WIDTH, B_WIDTH), const2),
        pl.BlockSpec((1, B_WIDTH), const2),
        pl.BlockSpec((1, A_WIDTH), const2),
        pl.BlockSpec((1, B_WIDTH), const2),
        pl.BlockSpec((1, C_WIDTH), const2),
        pl.BlockSpec((D, D), const2),
        pl.BlockSpec((1, D), const2),
        pl.BlockSpec((ROUTER_ROWS, D), const2),
        pl.BlockSpec((ROUTER_ROWS, D), const2),
    ]
    out_shape = [
        jax.ShapeDtypeStruct((B, S, D), F32),
        jax.ShapeDtypeStruct((B, S, D), BF16),
        jax.ShapeDtypeStruct((B, S, LANES), F32),
    ]
    out_specs = [
        pl.BlockSpec((1, Tt, D), tile),
        pl.BlockSpec((1, Tt, D), tile),
        pl.BlockSpec((1, Tt, LANES), tile),
    ]
    return pl.pallas_call(
        functools.partial(_postmix_kernel, seq_len=S), grid=(B, nt),
        in_specs=in_specs, out_specs=out_specs, out_shape=out_shape,
        scratch_shapes=[pltpu.VMEM((Tt + 2 * POOL_HALO, B_WIDTH), F32)],
        compiler_params=_params(("parallel", "parallel")), name="postmix",
    )(h, oa, oc, ub, ub_rows, ub_rows, lw["w_pool"], lw["s_pool"], lw["g_out_a"], lw["g_out_b"],
      lw["g_out_c"], lw["w_out"], lw["g_ffn"], lw["w_router_hi"], lw["w_router_lo"])


def _moe_kernel(m_ref, gates_ref, h1_ref, wgu_ref, wd_ref, o_ref, acc_ref):
    e = pl.program_id(1)

    @pl.when(e == 0)
    def _():
        acc_ref[...] = jnp.zeros_like(acc_ref)

    gu = jnp.dot(m_ref[...], wgu_ref[0], preferred_element_type=F32)
    g = gu[:, :D_EXPERT]
    up = gu[:, D_EXPERT:]
    gates = gates_ref[...]
    lane = lax.broadcasted_iota(jnp.int32, gates.shape, 1)
    gate = jnp.sum(jnp.where(lane == e, gates, 0.0), axis=1, keepdims=True)
    hid = g * jax.nn.sigmoid(g) * up * gate
    acc_ref[...] += jnp.dot(hid.astype(BF16), wd_ref[0], preferred_element_type=F32)

    @pl.when(e == pl.num_programs(1) - 1)
    def _():
        o_ref[...] = h1_ref[...] + acc_ref[...]


def _moe(m, gates, h1, lw):
    T, D = m.shape
    Tm = min(MOE_TILE, T)
    tile = lambda t, e: (t, 0)
    return pl.pallas_call(
        _moe_kernel, grid=(T // Tm, N_EXPERTS),
        in_specs=[
            pl.BlockSpec((Tm, D), tile),
            pl.BlockSpec((Tm, LANES), tile),
            pl.BlockSpec((Tm, D), tile),
            pl.BlockSpec((1, D, 2 * D_EXPERT), lambda t, e: (e, 0, 0)),
            pl.BlockSpec((1, D_EXPERT, D), lambda t, e: (e, 0, 0)),
        ],
        out_specs=pl.BlockSpec((Tm, D), tile),
        out_shape=jax.ShapeDtypeStruct((T, D), F32),
        scratch_shapes=[pltpu.VMEM((Tm, D), F32)],
        compiler_params=_params(("parallel", "arbitrary")), name="moe",
    )(m, gates, h1, lw["w_gate_up"], lw["w_down"])


def _ple_kernel(h_ref, p_ref, gple_ref, wgate_ref, wproj_ref, gfinal_ref, o_ref, *, final):
    h = h_ref[0]
    r = (_rms(h) * gple_ref[...]).astype(BF16)
    gate = jax.nn.sigmoid(jnp.dot(r, wgate_ref[...], preferred_element_type=F32))
    emb = jnp.dot(p_ref[0].astype(BF16), wproj_ref[...], preferred_element_type=F32)
    out = h + gate * emb
    if final:
        out = _rms(out) * gfinal_ref[...]
    o_ref[0] = out


def _ple(h, p, lw, g_final, final):
    B, S, D = h.shape
    Tt = TOKEN_TILE
    tile = lambda b, i: (b, i, 0)
    const2 = lambda b, i: (0, 0)
    return pl.pallas_call(
        functools.partial(_ple_kernel, final=final), grid=(B, S // Tt),
        in_specs=[
            pl.BlockSpec((1, Tt, D), tile),
            pl.BlockSpec((1, Tt, PLE_DIM), tile),
            pl.BlockSpec((1, D), const2),
            pl.BlockSpec((D, D), const2),
            pl.BlockSpec((PLE_DIM, D), const2),
            pl.BlockSpec((1, D), const2),
        ],
        out_specs=pl.BlockSpec((1, Tt, D), tile),
        out_shape=jax.ShapeDtypeStruct((B, S, D), F32),
        compiler_params=_params(("parallel", "parallel")), name="ple",
    )(h, p, lw["g_ple"], lw["w_ple_gate"], lw["w_ple_proj"], g_final)


def _rope_tables(seq_len):
    rows_n = seq_len // GRID_W
    row = jnp.repeat(jnp.arange(rows_n), GRID_W).astype(F32)[:, None]
    col = jnp.tile(jnp.arange(GRID_W), rows_n).astype(F32)[:, None]

    def axial(rot_dim):
        n_freq = rot_dim // 4
        inv = ROPE_THETA ** (-jnp.arange(n_freq, dtype=F32) / n_freq)
        ar, ac = row * inv, col * inv
        cos = jnp.concatenate([jnp.cos(ar), jnp.cos(ar), jnp.cos(ac), jnp.cos(ac)], axis=1)
        sin = jnp.concatenate([-jnp.sin(ar), jnp.sin(ar), -jnp.sin(ac), jnp.sin(ac)], axis=1)
        return cos, sin

    cos_a, sin_a = axial(HEAD_DIM)
    cos_r, sin_r = axial(C_ROPE)
    ones = jnp.ones((seq_len, C_NOPE), F32)
    zeros = jnp.zeros((seq_len, C_NOPE), F32)
    pad = jnp.zeros((seq_len, C_QK_PAD - C_NOPE - C_ROPE), F32)
    return dict(
        cos_a=jnp.concatenate([cos_a, cos_a], axis=1), sin_a=jnp.concatenate([sin_a, sin_a], axis=1),
        cos_c=jnp.concatenate([ones, cos_r, pad], axis=1), sin_c=jnp.concatenate([zeros, sin_r, pad], axis=1),
    )


def _pack_weights(w):
    L = DEPTH
    w_in = w["w_in"]
    kr_pad = jnp.pad(w_in[:, :, _KR0:], ((0, 0), (0, 0), (C_NOPE, C_QK_PAD - C_NOPE - C_ROPE)))
    q_scale = jnp.full((A_WIDTH,), HEAD_DIM ** -0.5 * LOG2E, F32)
    k_scale = jnp.ones((A_KV_HEADS * HEAD_DIM,), F32)
    g_qk = jnp.concatenate([jnp.tile(w["g_qa"], (1, A_HEADS)), jnp.tile(w["g_ka"], (1, A_KV_HEADS))], axis=1)
    g_qk = g_qk * jnp.concatenate([q_scale, k_scale])[None, :]
    w_q_up = jnp.pad(w["w_q_up"].reshape(L, C_Q_RANK, C_HEADS, C_NOPE + C_ROPE),
                     ((0, 0), (0, 0), (0, 0), (0, C_QK_PAD - C_NOPE - C_ROPE)))
    kv = w["w_kv_up"].reshape(L, C_KV_RANK, C_HEADS, C_NOPE + C_V)
    k_nope = jnp.pad(kv[..., :C_NOPE], ((0, 0), (0, 0), (0, 0), (0, C_QK_PAD - C_NOPE)))
    w_kv_up = jnp.concatenate([k_nope.reshape(L, C_KV_RANK, C_HEADS * C_QK_PAD),
                               kv[..., C_NOPE:].reshape(L, C_KV_RANK, C_WIDTH)], axis=2)
    eye = jnp.eye(B_GROUPS, dtype=F32)
    w_pool = (w["w_pool"][:, :, :, None, :] * eye[None, :, None, :, None]).reshape(L, B_WIDTH, B_WIDTH)
    w_router = jnp.concatenate([w["w_router_expert"], w["w_router_group"]], axis=2)
    w_router = jnp.pad(jnp.swapaxes(w_router, 1, 2), ((0, 0), (0, ROUTER_ROWS - N_EXPERTS - MOE_GROUPS), (0, 0)))
    w_router_hi = w_router.astype(BF16)
    w_router_lo = (w_router - w_router_hi.astype(F32)).astype(BF16)
    row = lambda a: a[:, None, :]
    return dict(
        g_mix=row(w["g_mix"]), w_in=jnp.concatenate([w_in[:, :, :_KR0], kr_pad], axis=2).astype(BF16),
        g_qk=row(g_qk), g_cq=row(w["g_cq"]), w_q_up=w_q_up.reshape(L, C_Q_RANK, C_HEADS * C_QK_PAD).astype(BF16),
        g_ckv=row(w["g_ckv"]), w_kv_up=w_kv_up.astype(BF16),
        w_pool=w_pool.astype(BF16), s_pool=row(w["s_pool"]),
        g_out_a=row(w["g_out_a"]), g_out_b=row(w["g_out_b"]), g_out_c=row(w["g_out_c"]),
        w_out=w["w_out"].astype(BF16), g_ffn=row(w["g_ffn"]),
        w_router_hi=w_router_hi, w_router_lo=w_router_lo,
        w_gate_up=jnp.concatenate([w["w_gate"], w["w_up"]], axis=3).astype(BF16),
        w_down=w["w_down"].astype(BF16),
        g_ple=row(w["g_ple"]), w_ple_gate=w["w_ple_gate"].astype(BF16), w_ple_proj=w["w_ple_proj"].astype(BF16),
    )


def _trunk(x, p, packed, g_final):
    B, S, D = x.shape
    tabs = _rope_tables(S)
    hs = np.kron(np.eye(_QK_A // HEAD_DIM, dtype=np.float32), np.ones((HEAD_DIM, HEAD_DIM), np.float32))
    tabs["hsum"] = jnp.asarray(hs, BF16)
    g_final = g_final[None, :]
    h = x
    for i in range(DEPTH):
        lw = {name: a[i] for name, a in packed.items()}
        qa, ka, vat, ub, qc, kc, vct = _premix(h, lw, tabs)
        oa = _attention(qa, ka, vat, heads_per_step=A_HEADS, group=A_HEADS // A_KV_HEADS)
        oc = _attention(qc, kc, vct, heads_per_step=LANES // C_V, group=1)
        h1, m, gates = _postmix(h, oa, oc, ub, lw)
        h2 = _moe(m.reshape(B * S, D), gates.reshape(B * S, LANES), h1.reshape(B * S, D), lw)
        h = _ple(h2.reshape(B, S, D), p[i], lw, g_final, final=(i == DEPTH - 1))
    return h


def kernel(x_prompt, x_sample, p_prompt, p_sample, g_mix, w_in, g_qa, g_ka, w_pool, s_pool, g_cq, w_q_up, g_ckv, w_kv_up, g_out_a, g_out_b, g_out_c, w_out, g_ffn, w_router_group, w_router_expert, w_gate, w_up, w_down, g_ple, w_ple_gate, w_ple_proj, g_final):
    weights = dict(g_mix=g_mix, w_in=w_in, g_qa=g_qa, g_ka=g_ka, w_pool=w_pool, s_pool=s_pool, g_cq=g_cq,
                   w_q_up=w_q_up, g_ckv=g_ckv, w_kv_up=w_kv_up, g_out_a=g_out_a, g_out_b=g_out_b,
                   g_out_c=g_out_c, w_out=w_out, g_ffn=g_ffn, w_router_group=w_router_group,
                   w_router_expert=w_router_expert, w_gate=w_gate, w_up=w_up, w_down=w_down, g_ple=g_ple,
                   w_ple_gate=w_ple_gate, w_ple_proj=w_ple_proj)
    packed = _pack_weights(weights)
    nb = x_prompt.shape[0]
    x = jnp.concatenate([x_prompt, x_sample], axis=0)
    p = jnp.concatenate([p_prompt, p_sample], axis=1)
    y = _trunk(x, p, packed, g_final)
    return (y[:nb], y[nb:])
```

```python
import functools
import math

import numpy as np
import jax
import jax.numpy as jnp
from jax import lax
from jax.experimental import pallas as pl
from jax.experimental.pallas import tpu as pltpu

F32 = jnp.float32
BF16 = jnp.bfloat16

D_MODEL = 1024
DEPTH = 4
GRID_W = 64
PLE_DIM = 256
HEAD_DIM = 64
ROPE_THETA = 10000.0
EPS = 1e-6
A_HEADS = 6
A_KV_HEADS = 2
A_WIDTH = A_HEADS * HEAD_DIM
B_GROUPS = 4
B_GROUP_DIM = 64
B_WINDOWS = (2, 4, 8, 16)
B_WIDTH = B_GROUPS * B_GROUP_DIM
C_HEADS = 6
C_NOPE = 64
C_ROPE = 32
C_V = 64
C_Q_RANK = 256
C_KV_RANK = 128
C_WIDTH = C_HEADS * C_V
C_QK_PAD = 128
MOE_GROUPS = 4
EXPERTS_PER_GROUP = 8
N_EXPERTS = MOE_GROUPS * EXPERTS_PER_GROUP
D_EXPERT = 256
ROUTER_ROWS = 40
LOG2E = math.log2(math.e)

LANES = 128
VMEM_LIMIT_BYTES = 56 * 1024 * 1024

TOKEN_TILE = 512
Q_TILE = 256
KEY_SUB = 128
MOE_TILE = 1024
V_PAD = 16
POOL_HALO = 8

_QK_A = A_WIDTH + A_KV_HEADS * HEAD_DIM
_V_A0 = _QK_A
_U_B0 = _V_A0 + A_KV_HEADS * HEAD_DIM
_CQ0 = _U_B0 + B_WIDTH
_CKV0 = _CQ0 + C_Q_RANK
_KR0 = _CKV0 + C_KV_RANK
IN_COLS_PACKED = _KR0 + C_QK_PAD


def _params(sem):
    return pltpu.CompilerParams(dimension_semantics=sem, vmem_limit_bytes=VMEM_LIMIT_BYTES)


def _rms(x):
    return x * lax.rsqrt(jnp.mean(x * x, axis=-1, keepdims=True) + EPS)


def _swap_halves(x, half):
    n = x.shape[-1]
    lane = lax.broadcasted_iota(jnp.int32, x.shape, x.ndim - 1)
    return jnp.where((lane & half) == 0, pltpu.roll(x, n - half, x.ndim - 1), pltpu.roll(x, half, x.ndim - 1))


def _premix_kernel(h_ref, g_mix_ref, w_in_ref, hsum_ref, g_qk_ref, cosa_ref, sina_ref,
                   g_cq_ref, w_qup_ref, g_ckv_ref, w_kvup_ref, cosc_ref, sinc_ref,
                   qat_ref, ka_ref, vat_ref, ub_ref, qct_ref, kc_ref, vct_ref, *, c_scale):
    a = _rms(h_ref[0]) * g_mix_ref[...]
    z = jnp.dot(a.astype(BF16), w_in_ref[...], preferred_element_type=F32)

    qk = z[:, :_QK_A]
    ss = jnp.dot((qk * qk).astype(BF16), hsum_ref[...], preferred_element_type=F32)
    qk = qk * lax.rsqrt(ss * (1.0 / HEAD_DIM) + EPS) * g_qk_ref[...]
    reps = _QK_A // LANES
    cos_a = jnp.concatenate([cosa_ref[...]] * reps, axis=1)
    sin_a = jnp.concatenate([sina_ref[...]] * reps, axis=1)
    qk = qk * cos_a + _swap_halves(qk, HEAD_DIM // 4) * sin_a
    qat = qk[:, :A_WIDTH].T.astype(BF16)
    for hh in range(A_HEADS):
        qat_ref[0, hh] = qat[hh * HEAD_DIM:(hh + 1) * HEAD_DIM]
    ka = qk[:, A_WIDTH:].astype(BF16)
    for hh in range(A_KV_HEADS):
        ka_ref[0, hh] = ka[:, hh * HEAD_DIM:(hh + 1) * HEAD_DIM]
    tt = z.shape[0]
    ones_rows = (lax.broadcasted_iota(jnp.int32, (V_PAD, tt), 0) == 0).astype(BF16)
    vt = z[:, _V_A0:_U_B0].T.astype(BF16)
    for hh in range(A_KV_HEADS):
        vat_ref[0, hh, 0] = jnp.concatenate([vt[hh * HEAD_DIM:(hh + 1) * HEAD_DIM], ones_rows], axis=0)

    ub_ref[0] = z[:, _U_B0:_CQ0]

    cqn = _rms(z[:, _CQ0:_CKV0]) * g_cq_ref[...]
    qc = jnp.dot(cqn.astype(BF16), w_qup_ref[...], preferred_element_type=F32)
    cos_c = cosc_ref[...]
    sin_c = sinc_ref[...]
    cos_q = jnp.concatenate([cos_c * c_scale] * C_HEADS, axis=1)
    sin_q = jnp.concatenate([sin_c * c_scale] * C_HEADS, axis=1)
    qct = (qc * cos_q + _swap_halves(qc, C_ROPE // 4) * sin_q).T.astype(BF16)
    ckvn = _rms(z[:, _CKV0:_KR0]) * g_ckv_ref[...]
    kv = jnp.dot(ckvn.astype(BF16), w_kvup_ref[...], preferred_element_type=F32)
    kr = z[:, _KR0:IN_COLS_PACKED]
    kr = kr * cos_c + _swap_halves(kr, C_ROPE // 4) * sin_c
    for hh in range(C_HEADS):
        qct_ref[0, hh] = qct[hh * C_QK_PAD:(hh + 1) * C_QK_PAD]
        kc_ref[0, hh] = (kv[:, hh * C_QK_PAD:(hh + 1) * C_QK_PAD] + kr).astype(BF16)
    vct = kv[:, C_HEADS * C_QK_PAD:].T.astype(BF16)
    for hh in range(C_HEADS):
        vct_ref[0, hh, 0] = jnp.concatenate([vct[hh * C_V:(hh + 1) * C_V], ones_rows], axis=0)


def _premix(h, lw, tabs):
    B, S, D = h.shape
    Tt = TOKEN_TILE
    nt = S // Tt
    tile = lambda b, i: (b, i, 0)
    const2 = lambda b, i: (0, 0)
    headmajor = lambda b, i: (b, 0, i, 0)
    headmajor_t = lambda b, i: (b, 0, 0, i)
    vt_map = lambda b, i: (b, 0, i, 0, 0)
    tab_map = lambda b, i: (i, 0)
    in_specs = [
        pl.BlockSpec((1, Tt, D), tile),
        pl.BlockSpec((1, D), const2),
        pl.BlockSpec((D, IN_COLS_PACKED), const2),
        pl.BlockSpec((_QK_A, _QK_A), const2),
        pl.BlockSpec((1, _QK_A), const2),
        pl.BlockSpec((Tt, LANES), tab_map),
        pl.BlockSpec((Tt, LANES), tab_map),
        pl.BlockSpec((1, C_Q_RANK), const2),
        pl.BlockSpec((C_Q_RANK, C_HEADS * C_QK_PAD), const2),
        pl.BlockSpec((1, C_KV_RANK), const2),
        pl.BlockSpec((C_KV_RANK, C_HEADS * C_QK_PAD + C_WIDTH), const2),
        pl.BlockSpec((Tt, LANES), tab_map),
        pl.BlockSpec((Tt, LANES), tab_map),
    ]
    out_shape = [
        jax.ShapeDtypeStruct((B, A_HEADS, HEAD_DIM, S), BF16),
        jax.ShapeDtypeStruct((B, A_KV_HEADS, S, HEAD_DIM), BF16),
        jax.ShapeDtypeStruct((B, A_KV_HEADS, nt, HEAD_DIM + V_PAD, Tt), BF16),
        jax.ShapeDtypeStruct((B, S, B_WIDTH), F32),
        jax.ShapeDtypeStruct((B, C_HEADS, C_QK_PAD, S), BF16),
        jax.ShapeDtypeStruct((B, C_HEADS, S, C_QK_PAD), BF16),
        jax.ShapeDtypeStruct((B, C_HEADS, nt, C_V + V_PAD, Tt), BF16),
    ]
    out_specs = [
        pl.BlockSpec((1, A_HEADS, HEAD_DIM, Tt), headmajor_t),
        pl.BlockSpec((1, A_KV_HEADS, Tt, HEAD_DIM), headmajor),
        pl.BlockSpec((1, A_KV_HEADS, 1, HEAD_DIM + V_PAD, Tt), vt_map),
        pl.BlockSpec((1, Tt, B_WIDTH), tile),
        pl.BlockSpec((1, C_HEADS, C_QK_PAD, Tt), headmajor_t),
        pl.BlockSpec((1, C_HEADS, Tt, C_QK_PAD), headmajor),
        pl.BlockSpec((1, C_HEADS, 1, C_V + V_PAD, Tt), vt_map),
    ]
    kern = functools.partial(_premix_kernel, c_scale=(C_NOPE + C_ROPE) ** -0.5 * LOG2E)
    return pl.pallas_call(
        kern, grid=(B, nt), in_specs=in_specs, out_specs=out_specs, out_shape=out_shape,
        compiler_params=_params(("parallel", "parallel")), name="premix",
    )(h, lw["g_mix"], lw["w_in"], tabs["hsum"], lw["g_qk"], tabs["cos_a"], tabs["sin_a"],
      lw["g_cq"], lw["w_q_up"], lw["g_ckv"], lw["w_kv_up"], tabs["cos_c"], tabs["sin_c"])


def _attn_kernel(qt_ref, k_ref, vt_ref, o_ref, s_buf, p_buf, *, passes, n_chunks, chunk, sub, dv):
    tq = qt_ref.shape[3]
    rows = vt_ref.shape[3]
    nsub = chunk // sub
    outs = []
    for streams in passes:
        n = len(streams)
        assert n >= 2, "the score buffer of an item is rewritten two items later"

        def scores(slot, j, r, streams=streams):
            hq, hkv = streams[slot]
            start = pl.multiple_of(j * chunk + r * sub, sub)
            s = jnp.dot(k_ref[0, hkv, pl.ds(start, sub), :], qt_ref[0, hq], preferred_element_type=F32)
            s_buf[slot, r * sub:(r + 1) * sub, :] = s
            return jnp.max(s, axis=0, keepdims=True)

        def probs(slot, r, m_b):
            s = s_buf[slot, r * sub:(r + 1) * sub, :]
            p_buf[slot, r * sub:(r + 1) * sub, :] = jnp.exp2(s - m_b).astype(BF16)

        def body(j, carry, streams=streams, n=n):
            cm, state = carry
            new_state = []
            for i, (_, hkv) in enumerate(streams):
                m, acc = state[i]
                m_new = jnp.maximum(m, cm)
                m_b = jnp.broadcast_to(m_new, (sub, tq))
                if i + 1 < n:
                    nxt, nj = i + 1, j
                else:
                    nxt, nj = 0, jnp.minimum(j + 1, n_chunks - 1)
                cm = None
                for r in range(nsub):
                    c = scores(nxt, nj, r)
                    cm = c if cm is None else jnp.maximum(cm, c)
                    probs(i, r, m_b)
                pv = jnp.dot(vt_ref[0, hkv, j], p_buf[i], preferred_element_type=F32)
                new_state.append((m_new, jnp.exp2(m - m_new) * acc + pv))
            return cm, tuple(new_state)

        cm0 = None
        for r in range(nsub):
            c = scores(0, 0, r)
            cm0 = c if cm0 is None else jnp.maximum(cm0, c)
        init = tuple((jnp.full((1, tq), -jnp.inf, F32), jnp.zeros((rows, tq), F32)) for _ in streams)
        _, final = lax.fori_loop(0, n_chunks, body, (cm0, init))
        for _, acc in final:
            outs.append(acc[:dv] * (1.0 / acc[dv:dv + 1]))
    o_ref[0] = jnp.concatenate(outs, axis=0).T


def _attention(qt, k, vt, *, heads_per_step, group, dv):
    B, H, dk, S = qt.shape
    nk, rows, Tk = vt.shape[2:]
    Tq = Q_TILE
    nsteps = H // heads_per_step
    kv_per_step = heads_per_step // group
    if group > 1:
        passes = tuple(tuple((kv * group + g, kv) for g in range(group)) for kv in range(kv_per_step))
    else:
        passes = (tuple((hh, hh) for hh in range(heads_per_step)),)
    n_streams = max(len(p) for p in passes)
    kern = functools.partial(_attn_kernel, passes=passes, n_chunks=nk, chunk=Tk, sub=KEY_SUB, dv=dv)
    return pl.pallas_call(
        kern, grid=(B, nsteps, S // Tq),
        in_specs=[
            pl.BlockSpec((1, heads_per_step, dk, Tq), lambda b, g, i: (b, g, 0, i)),
            pl.BlockSpec((1, kv_per_step, S, dk), lambda b, g, i: (b, g, 0, 0)),
            pl.BlockSpec((1, kv_per_step, nk, rows, Tk), lambda b, g, i: (b, g, 0, 0, 0)),
        ],
        out_specs=pl.BlockSpec((1, Tq, heads_per_step * dv), lambda b, g, i: (b, i, g)),
        out_shape=jax.ShapeDtypeStruct((B, S, H * dv), F32),
        scratch_shapes=[pltpu.VMEM((n_streams, Tk, Tq), F32), pltpu.VMEM((n_streams, Tk, Tq), BF16)],
        compiler_params=_params(("parallel", "parallel", "arbitrary")), name="attention",
    )(qt, k, vt)


def _route(logits):
    tt = logits.shape[1]
    le = logits[:N_EXPERTS]
    lg = logits[N_EXPERTS:]
    row8 = lax.broadcasted_iota(jnp.int32, (ROUTER_ROWS - N_EXPERTS, tt), 0)
    lg = jnp.where(row8 < MOE_GROUPS, lg, -jnp.inf)
    gmax = jnp.max(lg, axis=0, keepdims=True)
    gsel = jnp.min(jnp.where(lg == gmax, row8, MOE_GROUPS), axis=0, keepdims=True)
    gprob = 1.0 / jnp.sum(jnp.exp(lg - gmax), axis=0, keepdims=True)
    row = lax.broadcasted_iota(jnp.int32, (N_EXPERTS, tt), 0)
    lm = jnp.where((row // EXPERTS_PER_GROUP) == gsel, le, -jnp.inf)
    m1 = jnp.max(lm, axis=0, keepdims=True)
    i1 = jnp.min(jnp.where(lm == m1, row, N_EXPERTS), axis=0, keepdims=True)
    lm2 = jnp.where(row == i1, -jnp.inf, lm)
    m2 = jnp.max(lm2, axis=0, keepdims=True)
    i2 = jnp.min(jnp.where(lm2 == m2, row, N_EXPERTS), axis=0, keepdims=True)
    r = jnp.exp(m2 - m1)
    w1 = gprob / (1.0 + r)
    w2 = w1 * r
    return jnp.where(row == i1, w1, 0.0) + jnp.where(row == i2, w2, 0.0)


def _postmix_kernel(h_ref, oa_ref, oc_ref, ub_ref, prev_ref, next_ref, wpool_ref, spool_ref,
                    ga_ref, gb_ref, gc_ref, wout_ref, gffn_ref, wr_hi_ref, wr_lo_ref,
                    h1_ref, m_ref, gates_ref, ext_ref, *, seq_len):
    i = pl.program_id(1)
    nt = pl.num_programs(1)
    tt = h_ref.shape[1]
    half = B_WIDTH // 2

    u = ub_ref[0]
    ext_ref[POOL_HALO:POOL_HALO + tt, :] = u
    ext_ref[:POOL_HALO, :] = jnp.where(i > 0, prev_ref[0, 0], 0.0)
    ext_ref[POOL_HALO + tt:, :] = jnp.where(i < nt - 1, next_ref[0, 0], 0.0)

    def window(lo, hi, lanes):
        acc = None
        for d in range(lo, hi):
            piece = ext_ref[POOL_HALO + d:POOL_HALO + d + tt, lanes]
            acc = piece if acc is None else acc + piece
        return acc

    lo_lanes = slice(0, half)
    hi_lanes = slice(half, B_WIDTH)
    w2 = window(-1, 1, lo_lanes)
    w4 = w2 + window(-2, -1, lo_lanes) + window(1, 2, lo_lanes)
    w8 = window(-4, 4, hi_lanes)
    w16 = w8 + window(-8, -4, hi_lanes) + window(4, 8, hi_lanes)
    lane = lax.broadcasted_iota(jnp.int32, (tt, half), 1)
    first = lane < B_GROUP_DIM
    sums = jnp.concatenate([jnp.where(first, w2, w4), jnp.where(first, w8, w16)], axis=1)
    pos = i * tt + lax.broadcasted_iota(jnp.int32, (tt, B_WIDTH), 0)
    lane_b = lax.broadcasted_iota(jnp.int32, (tt, B_WIDTH), 1)
    hw = jnp.left_shift(1, lane_b // B_GROUP_DIM)
    cnt = jnp.minimum(pos + hw, seq_len) - jnp.maximum(pos - hw, 0)
    dlt = sums / cnt.astype(F32) - u
    ob = jnp.dot(dlt.astype(BF16), wpool_ref[...], preferred_element_type=F32) * spool_ref[...]

    merged = jnp.concatenate([
        (_rms(oa_ref[0]) * ga_ref[...]).astype(BF16),
        (_rms(ob) * gb_ref[...]).astype(BF16),
        (_rms(oc_ref[0]) * gc_ref[...]).astype(BF16)], axis=1)
    h1 = h_ref[0] + jnp.dot(merged, wout_ref[...], preferred_element_type=F32)
    h1_ref[0] = h1

    m = _rms(h1) * gffn_ref[...]
    m_hi = m.astype(BF16)
    m_ref[0] = m_hi
    m_lo = (m - m_hi.astype(F32)).astype(BF16)
    nt_dims = (((1,), (1,)), ((), ()))
    logits = (lax.dot_general(wr_hi_ref[...], m_hi, nt_dims, preferred_element_type=F32)
              + lax.dot_general(wr_lo_ref[...], m_hi, nt_dims, preferred_element_type=F32)
              + lax.dot_general(wr_hi_ref[...], m_lo, nt_dims, preferred_element_type=F32))
    gates = _route(logits)
    gates = jnp.concatenate([gates, jnp.zeros((LANES - N_EXPERTS, tt), F32)], axis=0)
    gates_ref[0] = gates.T


def _postmix(h, oa, oc, ub, lw):
    B, S, D = h.shape
    Tt = TOKEN_TILE
    nt = S // Tt
    rows = Tt // POOL_HALO
    ub_rows = ub.reshape(B, S // POOL_HALO, POOL_HALO, B_WIDTH)
    tile = lambda b, i: (b, i, 0)
    const2 = lambda b, i: (0, 0)
    in_specs = [
        pl.BlockSpec((1, Tt, D), tile),
        pl.BlockSpec((1, Tt, A_WIDTH), tile),
        pl.BlockSpec((1, Tt, C_WIDTH), tile),
        pl.BlockSpec((1, Tt, B_WIDTH), tile),
        pl.BlockSpec((1, 1, POOL_HALO, B_WIDTH), lambda b, i: (b, jnp.maximum(i * rows - 1, 0), 0, 0)),
        pl.BlockSpec((1, 1, POOL_HALO, B_WIDTH),
                     lambda b, i: (b, jnp.minimum((i + 1) * rows, S // POOL_HALO - 1), 0, 0)),
        pl.BlockSpec((B_WIDTH, B_WIDTH), const2),
        pl.BlockSpec((1, B_WIDTH), const2),
        pl.BlockSpec((1, A_WIDTH), const2),
        pl.BlockSpec((1, B_WIDTH), const2),
        pl.BlockSpec((1, C_WIDTH), const2),
        pl.BlockSpec((D, D), const2),
        pl.BlockSpec((1, D), const2),
        pl.BlockSpec((ROUTER_ROWS, D), const2),
        pl.BlockSpec((ROUTER_ROWS, D), const2),
    ]
    out_shape = [
        jax.ShapeDtypeStruct((B, S, D), F32),
        jax.ShapeDtypeStruct((B, S, D), BF16),
        jax.ShapeDtypeStruct((B, S, LANES), F32),
    ]
    out_specs = [
        pl.BlockSpec((1, Tt, D), tile),
        pl.BlockSpec((1, Tt, D), tile),
        pl.BlockSpec((1, Tt, LANES), tile),
    ]
    return pl.pallas_call(
        functools.partial(_postmix_kernel, seq_len=S), grid=(B, nt),
        in_specs=in_specs, out_specs=out_specs, out_shape=out_shape,
        scratch_shapes=[pltpu.VMEM((Tt + 2 * POOL_HALO, B_WIDTH), F32)],
        compiler_params=_params(("parallel", "parallel")), name="postmix",
    )(h, oa, oc, ub, ub_rows, ub_rows, lw["w_pool"], lw["s_pool"], lw["g_out_a"], lw["g_out_b"],
      lw["g_out_c"], lw["w_out"], lw["g_ffn"], lw["w_router_hi"], lw["w_router_lo"])


def _moe_kernel(m_ref, gates_ref, h1_ref, wgu_ref, wd_ref, o_ref, acc_ref):
    e = pl.program_id(1)

    @pl.when(e == 0)
    def _():
        acc_ref[...] = jnp.zeros_like(acc_ref)

    gu = jnp.dot(m_ref[...], wgu_ref[0], preferred_element_type=F32)
    g = gu[:, :D_EXPERT]
    up = gu[:, D_EXPERT:]
    gates = gates_ref[...]
    lane = lax.broadcasted_iota(jnp.int32, gates.shape, 1)
    gate = jnp.sum(jnp.where(lane == e, gates, 0.0), axis=1, keepdims=True)
    hid = g * jax.nn.sigmoid(g) * up * gate
    acc_ref[...] += jnp.dot(hid.astype(BF16), wd_ref[0], preferred_element_type=F32)

    @pl.when(e == pl.num_programs(1) - 1)
    def _():
        o_ref[...] = h1_ref[...] + acc_ref[...]


def _moe(m, gates, h1, lw):
    T, D = m.shape
    Tm = min(MOE_TILE, T)
    tile = lambda t, e: (t, 0)
    return pl.pallas_call(
        _moe_kernel, grid=(T // Tm, N_EXPERTS),
        in_specs=[
            pl.BlockSpec((Tm, D), tile),
            pl.BlockSpec((Tm, LANES), tile),
            pl.BlockSpec((Tm, D), tile),
            pl.BlockSpec((1, D, 2 * D_EXPERT), lambda t, e: (e, 0, 0)),
            pl.BlockSpec((1, D_EXPERT, D), lambda t, e: (e, 0, 0)),
        ],
        out_specs=pl.BlockSpec((Tm, D), tile),
        out_shape=jax.ShapeDtypeStruct((T, D), F32),
        scratch_shapes=[pltpu.VMEM((Tm, D), F32)],
        compiler_params=_params(("parallel", "arbitrary")), name="moe",
    )(m, gates, h1, lw["w_gate_up"], lw["w_down"])


def _ple_kernel(h_ref, p_ref, gple_ref, wgate_ref, wproj_ref, gfinal_ref, o_ref, *, final):
    h = h_ref[0]
    r = (_rms(h) * gple_ref[...]).astype(BF16)
    gate = jax.nn.sigmoid(jnp.dot(r, wgate_ref[...], preferred_element_type=F32))
    emb = jnp.dot(p_ref[0].astype(BF16), wproj_ref[...], preferred_element_type=F32)
    out = h + gate * emb
    if final:
        out = _rms(out) * gfinal_ref[...]
    o_ref[0] = out


def _ple(h, p, lw, g_final, final):
    B, S, D = h.shape
    Tt = TOKEN_TILE
    tile = lambda b, i: (b, i, 0)
    const2 = lambda b, i: (0, 0)
    return pl.pallas_call(
        functools.partial(_ple_kernel, final=final), grid=(B, S // Tt),
        in_specs=[
            pl.BlockSpec((1, Tt, D), tile),
            pl.BlockSpec((1, Tt, PLE_DIM), tile),
            pl.BlockSpec((1, D), const2),
            pl.BlockSpec((D, D), const2),
            pl.BlockSpec((PLE_DIM, D), const2),
            pl.BlockSpec((1, D), const2),
        ],
        out_specs=pl.BlockSpec((1, Tt, D), tile),
        out_shape=jax.ShapeDtypeStruct((B, S, D), F32),
        compiler_params=_params(("parallel", "parallel")), name="ple",
    )(h, p, lw["g_ple"], lw["w_ple_gate"], lw["w_ple_proj"], g_final)


def _rope_tables(seq_len):
    rows_n = seq_len // GRID_W
    row = jnp.repeat(jnp.arange(rows_n), GRID_W).astype(F32)[:, None]
    col = jnp.tile(jnp.arange(GRID_W), rows_n).astype(F32)[:, None]

    def axial(rot_dim):
        n_freq = rot_dim // 4
        inv = ROPE_THETA ** (-jnp.arange(n_freq, dtype=F32) / n_freq)
        ar, ac = row * inv, col * inv
        cos = jnp.concatenate([jnp.cos(ar), jnp.cos(ar), jnp.cos(ac), jnp.cos(ac)], axis=1)
        sin = jnp.concatenate([-jnp.sin(ar), jnp.sin(ar), -jnp.sin(ac), jnp.sin(ac)], axis=1)
        return cos, sin

    cos_a, sin_a = axial(HEAD_DIM)
    cos_r, sin_r = axial(C_ROPE)
    ones = jnp.ones((seq_len, C_NOPE), F32)
    zeros = jnp.zeros((seq_len, C_NOPE), F32)
    pad = jnp.zeros((seq_len, C_QK_PAD - C_NOPE - C_ROPE), F32)
    return dict(
        cos_a=jnp.concatenate([cos_a, cos_a], axis=1), sin_a=jnp.concatenate([sin_a, sin_a], axis=1),
        cos_c=jnp.concatenate([ones, cos_r, pad], axis=1), sin_c=jnp.concatenate([zeros, sin_r, pad], axis=1),
    )


def _pack_weights(w):
    L = DEPTH
    w_in = w["w_in"]
    kr_pad = jnp.pad(w_in[:, :, _KR0:], ((0, 0), (0, 0), (C_NOPE, C_QK_PAD - C_NOPE - C_ROPE)))
    q_scale = jnp.full((A_WIDTH,), HEAD_DIM ** -0.5 * LOG2E, F32)
    k_scale = jnp.ones((A_KV_HEADS * HEAD_DIM,), F32)
    g_qk = jnp.concatenate([jnp.tile(w["g_qa"], (1, A_HEADS)), jnp.tile(w["g_ka"], (1, A_KV_HEADS))], axis=1)
    g_qk = g_qk * jnp.concatenate([q_scale, k_scale])[None, :]
    w_q_up = jnp.pad(w["w_q_up"].reshape(L, C_Q_RANK, C_HEADS, C_NOPE + C_ROPE),
                     ((0, 0), (0, 0), (0, 0), (0, C_QK_PAD - C_NOPE - C_ROPE)))
    kv = w["w_kv_up"].reshape(L, C_KV_RANK, C_HEADS, C_NOPE + C_V)
    k_nope = jnp.pad(kv[..., :C_NOPE], ((0, 0), (0, 0), (0, 0), (0, C_QK_PAD - C_NOPE)))
    w_kv_up = jnp.concatenate([k_nope.reshape(L, C_KV_RANK, C_HEADS * C_QK_PAD),
                               kv[..., C_NOPE:].reshape(L, C_KV_RANK, C_WIDTH)], axis=2)
    eye = jnp.eye(B_GROUPS, dtype=F32)
    w_pool = (w["w_pool"][:, :, :, None, :] * eye[None, :, None, :, None]).reshape(L, B_WIDTH, B_WIDTH)
    w_router = jnp.concatenate([w["w_router_expert"], w["w_router_group"]], axis=2)
    w_router = jnp.pad(jnp.swapaxes(w_router, 1, 2), ((0, 0), (0, ROUTER_ROWS - N_EXPERTS - MOE_GROUPS), (0, 0)))
    w_router_hi = w_router.astype(BF16)
    w_router_lo = (w_router - w_router_hi.astype(F32)).astype(BF16)
    row = lambda a: a[:, None, :]
    return dict(
        g_mix=row(w["g_mix"]), w_in=jnp.concatenate([w_in[:, :, :_KR0], kr_pad], axis=2).astype(BF16),
        g_qk=row(g_qk), g_cq=row(w["g_cq"]), w_q_up=w_q_up.reshape(L, C_Q_RANK, C_HEADS * C_QK_PAD).astype(BF16),
        g_ckv=row(w["g_ckv"]), w_kv_up=w_kv_up.astype(BF16),
        w_pool=w_pool.astype(BF16), s_pool=row(w["s_pool"]),
        g_out_a=row(w["g_out_a"]), g_out_b=row(w["g_out_b"]), g_out_c=row(w["g_out_c"]),
        w_out=w["w_out"].astype(BF16), g_ffn=row(w["g_ffn"]),
        w_router_hi=w_router_hi, w_router_lo=w_router_lo,
        w_gate_up=jnp.concatenate([w["w_gate"], w["w_up"]], axis=3).astype(BF16),
        w_down=w["w_down"].astype(BF16),
        g_ple=row(w["g_ple"]), w_ple_gate=w["w_ple_gate"].astype(BF16), w_ple_proj=w["w_ple_proj"].astype(BF16),
    )


def _trunk(x, p, packed, g_final):
    B, S, D = x.shape
    tabs = _rope_tables(S)
    hs = np.kron(np.eye(_QK_A // HEAD_DIM, dtype=np.float32), np.ones((HEAD_DIM, HEAD_DIM), np.float32))
    tabs["hsum"] = jnp.asarray(hs, BF16)
    g_final = g_final[None, :]
    h = x
    for i in range(DEPTH):
        lw = {name: a[i] for name, a in packed.items()}
        qat, ka, vat, ub, qct, kc, vct = _premix(h, lw, tabs)
        oa = _attention(qat, ka, vat, heads_per_step=A_HEADS, group=A_HEADS // A_KV_HEADS, dv=HEAD_DIM)
        oc = _attention(qct, kc, vct, heads_per_step=LANES // C_V, group=1, dv=C_V)
        h1, m, gates = _postmix(h, oa, oc, ub, lw)
        h2 = _moe(m.reshape(B * S, D), gates.reshape(B * S, LANES), h1.reshape(B * S, D), lw)
        h = _ple(h2.reshape(B, S, D), p[i], lw, g_final, final=(i == DEPTH - 1))
    return h


def kernel(x_prompt, x_sample, p_prompt, p_sample, g_mix, w_in, g_qa, g_ka, w_pool, s_pool, g_cq, w_q_up, g_ckv, w_kv_up, g_out_a, g_out_b, g_out_c, w_out, g_ffn, w_router_group, w_router_expert, w_gate, w_up, w_down, g_ple, w_ple_gate, w_ple_proj, g_final):
    weights = dict(g_mix=g_mix, w_in=w_in, g_qa=g_qa, g_ka=g_ka, w_pool=w_pool, s_pool=s_pool, g_cq=g_cq,
                   w_q_up=w_q_up, g_ckv=g_ckv, w_kv_up=w_kv_up, g_out_a=g_out_a, g_out_b=g_out_b,
                   g_out_c=g_out_c, w_out=w_out, g_ffn=g_ffn, w_router_group=w_router_group,
                   w_router_expert=w_router_expert, w_gate=w_gate, w_up=w_up, w_down=w_down, g_ple=g_ple,
                   w_ple_gate=w_ple_gate, w_ple_proj=w_ple_proj)
    packed = _pack_weights(weights)
    nb = x_prompt.shape[0]
    x = jnp.concatenate([x_prompt, x_sample], axis=0)
    p = jnp.concatenate([p_prompt, p_sample], axis=1)
    y = _trunk(x, p, packed, g_final)
    return (y[:nb], y[nb:])
```

```python
import functools
import math

import numpy as np
import jax
import jax.numpy as jnp
from jax import lax
from jax.experimental import pallas as pl
from jax.experimental.pallas import tpu as pltpu

F32 = jnp.float32
BF16 = jnp.bfloat16

D_MODEL = 1024
DEPTH = 4
GRID_W = 64
PLE_DIM = 256
HEAD_DIM = 64
ROPE_THETA = 10000.0
EPS = 1e-6
A_HEADS = 6
A_KV_HEADS = 2
A_WIDTH = A_HEADS * HEAD_DIM
B_GROUPS = 4
B_GROUP_DIM = 64
B_WINDOWS = (2, 4, 8, 16)
B_WIDTH = B_GROUPS * B_GROUP_DIM
C_HEADS = 6
C_NOPE = 64
C_ROPE = 32
C_V = 64
C_Q_RANK = 256
C_KV_RANK = 128
C_WIDTH = C_HEADS * C_V
C_QK_PAD = 128
MOE_GROUPS = 4
EXPERTS_PER_GROUP = 8
N_EXPERTS = MOE_GROUPS * EXPERTS_PER_GROUP
D_EXPERT = 256
ROUTER_ROWS = 40
LOG2E = math.log2(math.e)

LANES = 128
VMEM_LIMIT_BYTES = 56 * 1024 * 1024

TOKEN_TILE = 512
Q_TILE = 256
KEY_SUB = 128
LOOKAHEAD = 2
MOE_TILE = 1024
SEG_ALIGN = 16
EXPERT_BLOCK = 128
DISPATCH_STRIP = 256
V_PAD = 16
POOL_HALO = 8

_QK_A = A_WIDTH + A_KV_HEADS * HEAD_DIM
_V_A0 = _QK_A
_U_B0 = _V_A0 + A_KV_HEADS * HEAD_DIM
_CQ0 = _U_B0 + B_WIDTH
_CKV0 = _CQ0 + C_Q_RANK
_KR0 = _CKV0 + C_KV_RANK
IN_COLS_PACKED = _KR0 + C_QK_PAD


def _params(sem):
    return pltpu.CompilerParams(dimension_semantics=sem, vmem_limit_bytes=VMEM_LIMIT_BYTES)


def _rms(x):
    return x * lax.rsqrt(jnp.mean(x * x, axis=-1, keepdims=True) + EPS)


def _swap_halves(x, half):
    n = x.shape[-1]
    lane = lax.broadcasted_iota(jnp.int32, x.shape, x.ndim - 1)
    return jnp.where((lane & half) == 0, pltpu.roll(x, n - half, x.ndim - 1), pltpu.roll(x, half, x.ndim - 1))


def _premix_kernel(h_ref, g_mix_ref, w_in_ref, hsum_ref, g_qk_ref, cosa_ref, sina_ref,
                   g_cq_ref, w_qup_ref, g_ckv_ref, w_kvup_ref, cosc_ref, sinc_ref,
                   qat_ref, ka_ref, vat_ref, ub_ref, qct_ref, kc_ref, vct_ref, *, c_scale):
    a = _rms(h_ref[0]) * g_mix_ref[...]
    z = jnp.dot(a.astype(BF16), w_in_ref[...], preferred_element_type=F32)

    qk = z[:, :_QK_A]
    ss = jnp.dot((qk * qk).astype(BF16), hsum_ref[...], preferred_element_type=F32)
    qk = qk * lax.rsqrt(ss * (1.0 / HEAD_DIM) + EPS) * g_qk_ref[...]
    reps = _QK_A // LANES
    cos_a = jnp.concatenate([cosa_ref[...]] * reps, axis=1)
    sin_a = jnp.concatenate([sina_ref[...]] * reps, axis=1)
    qk = qk * cos_a + _swap_halves(qk, HEAD_DIM // 4) * sin_a
    qat = qk[:, :A_WIDTH].T.astype(BF16)
    for hh in range(A_HEADS):
        qat_ref[0, hh] = qat[hh * HEAD_DIM:(hh + 1) * HEAD_DIM]
    ka = qk[:, A_WIDTH:].astype(BF16)
    for hh in range(A_KV_HEADS):
        ka_ref[0, hh] = ka[:, hh * HEAD_DIM:(hh + 1) * HEAD_DIM]
    tt = z.shape[0]
    ones_rows = (lax.broadcasted_iota(jnp.int32, (V_PAD, tt), 0) == 0).astype(BF16)
    vt = z[:, _V_A0:_U_B0].T.astype(BF16)
    for hh in range(A_KV_HEADS):
        vat_ref[0, hh, 0] = jnp.concatenate([vt[hh * HEAD_DIM:(hh + 1) * HEAD_DIM], ones_rows], axis=0)

    ub_ref[0] = z[:, _U_B0:_CQ0]

    cqn = _rms(z[:, _CQ0:_CKV0]) * g_cq_ref[...]
    qc = jnp.dot(cqn.astype(BF16), w_qup_ref[...], preferred_element_type=F32)
    cos_c = cosc_ref[...]
    sin_c = sinc_ref[...]
    cos_q = jnp.concatenate([cos_c * c_scale] * C_HEADS, axis=1)
    sin_q = jnp.concatenate([sin_c * c_scale] * C_HEADS, axis=1)
    qct = (qc * cos_q + _swap_halves(qc, C_ROPE // 4) * sin_q).T.astype(BF16)
    ckvn = _rms(z[:, _CKV0:_KR0]) * g_ckv_ref[...]
    kv = jnp.dot(ckvn.astype(BF16), w_kvup_ref[...], preferred_element_type=F32)
    kr = z[:, _KR0:IN_COLS_PACKED]
    kr = kr * cos_c + _swap_halves(kr, C_ROPE // 4) * sin_c
    for hh in range(C_HEADS):
        qct_ref[0, hh] = qct[hh * C_QK_PAD:(hh + 1) * C_QK_PAD]
        kc_ref[0, hh] = (kv[:, hh * C_QK_PAD:(hh + 1) * C_QK_PAD] + kr).astype(BF16)
    vct = kv[:, C_HEADS * C_QK_PAD:].T.astype(BF16)
    for hh in range(C_HEADS):
        vct_ref[0, hh, 0] = jnp.concatenate([vct[hh * C_V:(hh + 1) * C_V], ones_rows], axis=0)


def _premix(h, lw, tabs):
    B, S, D = h.shape
    Tt = TOKEN_TILE
    nt = S // Tt
    tile = lambda b, i: (b, i, 0)
    const2 = lambda b, i: (0, 0)
    headmajor = lambda b, i: (b, 0, i, 0)
    headmajor_t = lambda b, i: (b, 0, 0, i)
    vt_map = lambda b, i: (b, 0, i, 0, 0)
    tab_map = lambda b, i: (i, 0)
    in_specs = [
        pl.BlockSpec((1, Tt, D), tile),
        pl.BlockSpec((1, D), const2),
        pl.BlockSpec((D, IN_COLS_PACKED), const2),
        pl.BlockSpec((_QK_A, _QK_A), const2),
        pl.BlockSpec((1, _QK_A), const2),
        pl.BlockSpec((Tt, LANES), tab_map),
        pl.BlockSpec((Tt, LANES), tab_map),
        pl.BlockSpec((1, C_Q_RANK), const2),
        pl.BlockSpec((C_Q_RANK, C_HEADS * C_QK_PAD), const2),
        pl.BlockSpec((1, C_KV_RANK), const2),
        pl.BlockSpec((C_KV_RANK, C_HEADS * C_QK_PAD + C_WIDTH), const2),
        pl.BlockSpec((Tt, LANES), tab_map),
        pl.BlockSpec((Tt, LANES), tab_map),
    ]
    out_shape = [
        jax.ShapeDtypeStruct((B, A_HEADS, HEAD_DIM, S), BF16),
        jax.ShapeDtypeStruct((B, A_KV_HEADS, S, HEAD_DIM), BF16),
        jax.ShapeDtypeStruct((B, A_KV_HEADS, nt, HEAD_DIM + V_PAD, Tt), BF16),
        jax.ShapeDtypeStruct((B, S, B_WIDTH), F32),
        jax.ShapeDtypeStruct((B, C_HEADS, C_QK_PAD, S), BF16),
        jax.ShapeDtypeStruct((B, C_HEADS, S, C_QK_PAD), BF16),
        jax.ShapeDtypeStruct((B, C_HEADS, nt, C_V + V_PAD, Tt), BF16),
    ]
    out_specs = [
        pl.BlockSpec((1, A_HEADS, HEAD_DIM, Tt), headmajor_t),
        pl.BlockSpec((1, A_KV_HEADS, Tt, HEAD_DIM), headmajor),
        pl.BlockSpec((1, A_KV_HEADS, 1, HEAD_DIM + V_PAD, Tt), vt_map),
        pl.BlockSpec((1, Tt, B_WIDTH), tile),
        pl.BlockSpec((1, C_HEADS, C_QK_PAD, Tt), headmajor_t),
        pl.BlockSpec((1, C_HEADS, Tt, C_QK_PAD), headmajor),
        pl.BlockSpec((1, C_HEADS, 1, C_V + V_PAD, Tt), vt_map),
    ]
    kern = functools.partial(_premix_kernel, c_scale=(C_NOPE + C_ROPE) ** -0.5 * LOG2E)
    return pl.pallas_call(
        kern, grid=(B, nt), in_specs=in_specs, out_specs=out_specs, out_shape=out_shape,
        compiler_params=_params(("parallel", "parallel")), name="premix",
    )(h, lw["g_mix"], lw["w_in"], tabs["hsum"], lw["g_qk"], tabs["cos_a"], tabs["sin_a"],
      lw["g_cq"], lw["w_q_up"], lw["g_ckv"], lw["w_kv_up"], tabs["cos_c"], tabs["sin_c"])


def _attn_kernel(qt_ref, k_ref, vt_ref, o_ref, s_buf, p_buf, *, passes, order, n_chunks, chunk, sub, dv):
    tq = qt_ref.shape[3]
    rows = vt_ref.shape[3]
    nsub = chunk // sub
    n_pos = len(order)
    per_body = 1 + max(coff for _, coff in order)
    assert LOOKAHEAD < n_pos and n_chunks % per_body == 0
    outs = []
    for streams in passes:

        def score_piece(pos, j, r, streams=streams):
            hq, hkv = streams[order[pos][0]]
            start = pl.multiple_of(j * chunk + r * sub, sub)
            s = jnp.dot(k_ref[0, hkv, pl.ds(start, sub), :], qt_ref[0, hq], preferred_element_type=F32)
            s_buf[pos, r * sub:(r + 1) * sub, :] = s
            return jnp.max(s, axis=0, keepdims=True)

        def prob_piece(pos, r, m_b):
            s = s_buf[pos, r * sub:(r + 1) * sub, :]
            p_buf[pos, r * sub:(r + 1) * sub, :] = jnp.exp2(s - m_b).astype(BF16)

        def all_scores(pos, j):
            cm = None
            for r in range(nsub):
                c = score_piece(pos, j, r)
                cm = c if cm is None else jnp.maximum(cm, c)
            return cm

        def body(b, carry, streams=streams):
            pending, state = carry
            colmax = dict(enumerate(pending))
            state = list(state)
            for i, (si, coff) in enumerate(order):
                hkv = streams[si][1]
                m, acc = state[si]
                m_new = jnp.maximum(m, colmax[i])
                m_b = jnp.broadcast_to(m_new, (sub, tq))
                tpos = (i + LOOKAHEAD) % n_pos
                tj = jnp.minimum((b + (i + LOOKAHEAD) // n_pos) * per_body + order[tpos][1], n_chunks - 1)
                cm = None
                for r in range(nsub):
                    c = score_piece(tpos, tj, r)
                    cm = c if cm is None else jnp.maximum(cm, c)
                    prob_piece(i, r, m_b)
                colmax[i + LOOKAHEAD] = cm
                pv = jnp.dot(vt_ref[0, hkv, b * per_body + coff], p_buf[i], preferred_element_type=F32)
                state[si] = (m_new, jnp.exp2(m - m_new) * acc + pv)
            return tuple(colmax[n_pos + k] for k in range(LOOKAHEAD)), tuple(state)

        pending = tuple(all_scores(k, order[k][1]) for k in range(LOOKAHEAD))
        init = tuple((jnp.full((1, tq), -jnp.inf, F32), jnp.zeros((rows, tq), F32)) for _ in streams)
        _, final = lax.fori_loop(0, n_chunks // per_body, body, (pending, init))
        for _, acc in final:
            outs.append(acc[:dv] * (1.0 / acc[dv:dv + 1]))
    o_ref[0] = jnp.concatenate(outs, axis=0).T


def _attention(qt, k, vt, *, heads_per_step, group, dv):
    B, H, dk, S = qt.shape
    nk, rows, Tk = vt.shape[2:]
    Tq = Q_TILE
    nsteps = H // heads_per_step
    kv_per_step = heads_per_step // group
    if group > 1:
        passes = tuple(tuple((kv * group + g, kv) for g in range(group)) for kv in range(kv_per_step))
    else:
        passes = (tuple((hh, hh) for hh in range(heads_per_step)),)
    n_streams = len(passes[0])
    per_body = 2
    order = tuple((si, coff) for coff in range(per_body) for si in range(n_streams))
    kern = functools.partial(_attn_kernel, passes=passes, order=order, n_chunks=nk, chunk=Tk, sub=KEY_SUB, dv=dv)
    return pl.pallas_call(
        kern, grid=(B, nsteps, S // Tq),
        in_specs=[
            pl.BlockSpec((1, heads_per_step, dk, Tq), lambda b, g, i: (b, g, 0, i)),
            pl.BlockSpec((1, kv_per_step, S, dk), lambda b, g, i: (b, g, 0, 0)),
            pl.BlockSpec((1, kv_per_step, nk, rows, Tk), lambda b, g, i: (b, g, 0, 0, 0)),
        ],
        out_specs=pl.BlockSpec((1, Tq, heads_per_step * dv), lambda b, g, i: (b, i, g)),
        out_shape=jax.ShapeDtypeStruct((B, S, H * dv), F32),
        scratch_shapes=[pltpu.VMEM((len(order), Tk, Tq), F32), pltpu.VMEM((len(order), Tk, Tq), BF16)],
        compiler_params=_params(("parallel", "parallel", "arbitrary")), name="attention",
    )(qt, k, vt)


def _route(logits):
    tt = logits.shape[1]
    le = logits[:N_EXPERTS]
    lg = logits[N_EXPERTS:]
    row8 = lax.broadcasted_iota(jnp.int32, (ROUTER_ROWS - N_EXPERTS, tt), 0)
    lg = jnp.where(row8 < MOE_GROUPS, lg, -jnp.inf)
    gmax = jnp.max(lg, axis=0, keepdims=True)
    gsel = jnp.min(jnp.where(lg == gmax, row8, MOE_GROUPS), axis=0, keepdims=True)
    gprob = 1.0 / jnp.sum(jnp.exp(lg - gmax), axis=0, keepdims=True)
    row = lax.broadcasted_iota(jnp.int32, (N_EXPERTS, tt), 0)
    lm = jnp.where((row // EXPERTS_PER_GROUP) == gsel, le, -jnp.inf)
    m1 = jnp.max(lm, axis=0, keepdims=True)
    i1 = jnp.min(jnp.where(lm == m1, row, N_EXPERTS), axis=0, keepdims=True)
    lm2 = jnp.where(row == i1, -jnp.inf, lm)
    m2 = jnp.max(lm2, axis=0, keepdims=True)
    i2 = jnp.min(jnp.where(lm2 == m2, row, N_EXPERTS), axis=0, keepdims=True)
    r = jnp.exp(m2 - m1)
    w1 = gprob / (1.0 + r)
    w2 = w1 * r
    return i1, i2, w1, w2


def _dispatch_plan(i1, i2, w1, w2, before_ref, lower_ref):
    tt = i1.shape[1]
    row = lax.broadcasted_iota(jnp.int32, (N_EXPERTS, tt), 0)
    hit1 = row == i1
    hit2 = row == i2
    assign = jnp.where(hit1 | hit2, 1.0, 0.0)
    rank = jnp.dot(assign.astype(BF16), before_ref[...], preferred_element_type=F32)
    count = jnp.sum(assign, axis=1, keepdims=True)
    segs = jnp.floor((count + (SEG_ALIGN - 1.0)) * (1.0 / SEG_ALIGN))
    segs_b = jnp.broadcast_to(segs, (N_EXPERTS, LANES)).astype(BF16)
    offs = SEG_ALIGN * jnp.dot(lower_ref[...], segs_b, preferred_element_type=F32)
    pos = offs[:, :1] + rank
    pos1 = jnp.sum(jnp.where(hit1, pos, 0.0), axis=0, keepdims=True)
    pos2 = jnp.sum(jnp.where(hit2, pos, 0.0), axis=0, keepdims=True)
    route = jnp.concatenate([pos1, pos2, w1, w2, jnp.zeros((4, tt), F32)], axis=0)
    counts = jnp.broadcast_to(count, (N_EXPERTS, LANES))
    return route, offs.astype(jnp.int32), counts.astype(jnp.int32)


def _postmix_kernel(h_ref, oa_ref, oc_ref, ub_ref, prev_ref, next_ref, wpool_ref, spool_ref,
                    ga_ref, gb_ref, gc_ref, wout_ref, gffn_ref, wr_hi_ref, wr_lo_ref, before_ref, lower_ref,
                    h1_ref, m_ref, route_ref, offs_ref, counts_ref, ext_ref, *, seq_len):
    i = pl.program_id(1)
    nt = pl.num_programs(1)
    tt = h_ref.shape[1]
    half = B_WIDTH // 2

    u = ub_ref[0]
    ext_ref[POOL_HALO:POOL_HALO + tt, :] = u
    ext_ref[:POOL_HALO, :] = jnp.where(i > 0, prev_ref[0, 0], 0.0)
    ext_ref[POOL_HALO + tt:, :] = jnp.where(i < nt - 1, next_ref[0, 0], 0.0)

    def window(lo, hi, lanes):
        acc = None
        for d in range(lo, hi):
            piece = ext_ref[POOL_HALO + d:POOL_HALO + d + tt, lanes]
            acc = piece if acc is None else acc + piece
        return acc

    lo_lanes = slice(0, half)
    hi_lanes = slice(half, B_WIDTH)
    w2 = window(-1, 1, lo_lanes)
    w4 = w2 + window(-2, -1, lo_lanes) + window(1, 2, lo_lanes)
    w8 = window(-4, 4, hi_lanes)
    w16 = w8 + window(-8, -4, hi_lanes) + window(4, 8, hi_lanes)
    lane = lax.broadcasted_iota(jnp.int32, (tt, half), 1)
    first = lane < B_GROUP_DIM
    sums = jnp.concatenate([jnp.where(first, w2, w4), jnp.where(first, w8, w16)], axis=1)
    pos = i * tt + lax.broadcasted_iota(jnp.int32, (tt, B_WIDTH), 0)
    lane_b = lax.broadcasted_iota(jnp.int32, (tt, B_WIDTH), 1)
    hw = jnp.left_shift(1, lane_b // B_GROUP_DIM)
    cnt = jnp.minimum(pos + hw, seq_len) - jnp.maximum(pos - hw, 0)
    dlt = sums / cnt.astype(F32) - u
    ob = jnp.dot(dlt.astype(BF16), wpool_ref[...], preferred_element_type=F32) * spool_ref[...]

    merged = jnp.concatenate([
        (_rms(oa_ref[0]) * ga_ref[...]).astype(BF16),
        (_rms(ob) * gb_ref[...]).astype(BF16),
        (_rms(oc_ref[0]) * gc_ref[...]).astype(BF16)], axis=1)
    h1 = h_ref[0] + jnp.dot(merged, wout_ref[...], preferred_element_type=F32)
    h1_ref[0] = h1

    m = _rms(h1) * gffn_ref[...]
    m_hi = m.astype(BF16)
    m_ref[0] = m_hi
    m_lo = (m - m_hi.astype(F32)).astype(BF16)
    nt_dims = (((1,), (1,)), ((), ()))
    logits = (lax.dot_general(wr_hi_ref[...], m_hi, nt_dims, preferred_element_type=F32)
              + lax.dot_general(wr_lo_ref[...], m_hi, nt_dims, preferred_element_type=F32)
              + lax.dot_general(wr_hi_ref[...], m_lo, nt_dims, preferred_element_type=F32))
    route, offs, counts = _dispatch_plan(*_route(logits), before_ref, lower_ref)
    route_ref[0, 0] = route
    offs_ref[0, 0] = offs
    counts_ref[0, 0] = counts


def _postmix(h, oa, oc, ub, lw, tabs):
    B, S, D = h.shape
    Tt = min(MOE_TILE, S)
    nt = S // Tt
    rows = Tt // POOL_HALO
    ub_rows = ub.reshape(B, S // POOL_HALO, POOL_HALO, B_WIDTH)
    tile = lambda b, i: (b, i, 0)
    const2 = lambda b, i: (0, 0)
    in_specs = [
        pl.BlockSpec((1, Tt, D), tile),
        pl.BlockSpec((1, Tt, A_WIDTH), tile),
        pl.BlockSpec((1, Tt, C_WIDTH), tile),
        pl.BlockSpec((1, Tt, B_WIDTH), tile),
        pl.BlockSpec((1, 1, POOL_HALO, B_WIDTH), lambda b, i: (b, jnp.maximum(i * rows - 1, 0), 0, 0)),
        pl.BlockSpec((1, 1, POOL_HALO, B_WIDTH),
                     lambda b, i: (b, jnp.minimum((i + 1) * rows, S // POOL_HALO - 1), 0, 0)),
        pl.BlockSpec((B_WIDTH, B_WIDTH), const2),
        pl.BlockSpec((1, B_WIDTH), const2),
        pl.BlockSpec((1, A_WIDTH), const2),
        pl.BlockSpec((1, B_WIDTH), const2),
        pl.BlockSpec((1, C_WIDTH), const2),
        pl.BlockSpec((D, D), const2),
        pl.BlockSpec((1, D), const2),
        pl.BlockSpec((ROUTER_ROWS, D), const2),
        pl.BlockSpec((ROUTER_ROWS, D), const2),
        pl.BlockSpec((Tt, Tt), const2),
        pl.BlockSpec((N_EXPERTS, N_EXPERTS), const2),
    ]
    per_tile = lambda b, i: (b, i, 0, 0)
    out_shape = [
        jax.ShapeDtypeStruct((B, S, D), F32),
        jax.ShapeDtypeStruct((B, S, D), BF16),
        jax.ShapeDtypeStruct((B, nt, 8, Tt), F32),
        jax.ShapeDtypeStruct((B, nt, N_EXPERTS, LANES), jnp.int32),
        jax.ShapeDtypeStruct((B, nt, N_EXPERTS, LANES), jnp.int32),
    ]
    out_specs = [
        pl.BlockSpec((1, Tt, D), tile),
        pl.BlockSpec((1, Tt, D), tile),
        pl.BlockSpec((1, 1, 8, Tt), per_tile),
        pl.BlockSpec((1, 1, N_EXPERTS, LANES), per_tile),
        pl.BlockSpec((1, 1, N_EXPERTS, LANES), per_tile),
    ]
    return pl.pallas_call(
        functools.partial(_postmix_kernel, seq_len=S), grid=(B, nt),
        in_specs=in_specs, out_specs=out_specs, out_shape=out_shape,
        scratch_shapes=[pltpu.VMEM((Tt + 2 * POOL_HALO, B_WIDTH), F32)],
        compiler_params=_params(("parallel", "parallel")), name="postmix",
    )(h, oa, oc, ub, ub_rows, ub_rows, lw["w_pool"], lw["s_pool"], lw["g_out_a"], lw["g_out_b"],
      lw["g_out_c"], lw["w_out"], lw["g_ffn"], lw["w_router_hi"], lw["w_router_lo"],
      tabs["before"], tabs["lower"])


def _moe_kernel(plan_ref, m_ref, route_ref, h1_ref, wgu_ref, wd_ref, o_ref,
                perm_ref, xs_ref, ys_ref, ws_ref, *, n_rows):
    t = pl.program_id(0)
    e = pl.program_id(1)
    tm = m_ref.shape[0]

    @pl.when(e == 0)
    def _():
        route = route_ref[0]
        pos1 = route[0:1].astype(jnp.int32)
        pos2 = route[1:2].astype(jnp.int32)
        w1 = route[2:3]
        w2 = route[3:4]
        m = m_ref[...]

        def strip(k, _):
            r0 = pl.multiple_of(k * DISPATCH_STRIP, DISPATCH_STRIP)
            rows = r0 + lax.broadcasted_iota(jnp.int32, (DISPATCH_STRIP, tm), 0)
            hit1 = rows == pos1
            hit2 = rows == pos2
            perm = jnp.where(hit1 | hit2, 1.0, 0.0).astype(BF16)
            perm_ref[pl.ds(r0, DISPATCH_STRIP), :] = perm
            ws_ref[pl.ds(r0, DISPATCH_STRIP), :] = jnp.sum(
                jnp.where(hit1, w1, jnp.where(hit2, w2, 0.0)), axis=1, keepdims=True)
            xs_ref[pl.ds(r0, DISPATCH_STRIP), :] = jnp.dot(perm, m, preferred_element_type=F32).astype(BF16)
            return 0

        lax.fori_loop(0, n_rows // DISPATCH_STRIP, strip, 0)
        xs_ref[n_rows:, :] = jnp.zeros((EXPERT_BLOCK, xs_ref.shape[1]), BF16)
        ws_ref[n_rows:, :] = jnp.zeros((EXPERT_BLOCK, 1), F32)
        ys_ref[...] = jnp.zeros_like(ys_ref)

    off = plan_ref[t, e]
    cnt = plan_ref[t, N_EXPERTS + e]
    n_blocks = lax.shift_right_logical(cnt + (EXPERT_BLOCK - 1), EXPERT_BLOCK.bit_length() - 1)

    def block(bi, _):
        r0 = pl.multiple_of(off + bi * EXPERT_BLOCK, SEG_ALIGN)
        x = xs_ref[pl.ds(r0, EXPERT_BLOCK), :]
        gu = jnp.dot(x, wgu_ref[0], preferred_element_type=F32)
        g = gu[:, :D_EXPERT]
        hid = g * jax.nn.sigmoid(g) * gu[:, D_EXPERT:] * ws_ref[pl.ds(r0, EXPERT_BLOCK), :]
        ys_ref[pl.ds(r0, EXPERT_BLOCK), :] = jnp.dot(
            hid.astype(BF16), wd_ref[0], preferred_element_type=F32).astype(BF16)
        return 0

    lax.fori_loop(0, n_blocks, block, 0)

    @pl.when(e == pl.num_programs(1) - 1)
    def _():
        tn_dims = (((0,), (0,)), ((), ()))
        y = lax.dot_general(perm_ref[...], ys_ref[:n_rows, :], tn_dims, preferred_element_type=F32)
        o_ref[...] = h1_ref[...] + y


def _moe(m, route, plan, h1, lw):
    T, D = m.shape
    Tm = route.shape[-1]
    n_rows = 2 * Tm + N_EXPERTS * SEG_ALIGN
    assert n_rows % DISPATCH_STRIP == 0
    tile = lambda t, e, plan: (t, 0)
    grid_spec = pltpu.PrefetchScalarGridSpec(
        num_scalar_prefetch=1, grid=(T // Tm, N_EXPERTS),
        in_specs=[
            pl.BlockSpec((Tm, D), tile),
            pl.BlockSpec((1, 8, Tm), lambda t, e, plan: (t, 0, 0)),
            pl.BlockSpec((Tm, D), tile),
            pl.BlockSpec((1, D, 2 * D_EXPERT), lambda t, e, plan: (e, 0, 0)),
            pl.BlockSpec((1, D_EXPERT, D), lambda t, e, plan: (e, 0, 0)),
        ],
        out_specs=pl.BlockSpec((Tm, D), tile),
        scratch_shapes=[
            pltpu.VMEM((n_rows, Tm), BF16),
            pltpu.VMEM((n_rows + EXPERT_BLOCK, D), BF16),
            pltpu.VMEM((n_rows + EXPERT_BLOCK, D), BF16),
            pltpu.VMEM((n_rows + EXPERT_BLOCK, 1), F32),
        ],
    )
    return pl.pallas_call(
        functools.partial(_moe_kernel, n_rows=n_rows), grid_spec=grid_spec,
        out_shape=jax.ShapeDtypeStruct((T, D), F32),
        compiler_params=_params(("parallel", "arbitrary")), name="moe",
    )(plan, m, route, h1, lw["w_gate_up"], lw["w_down"])


def _ple_kernel(h_ref, p_ref, gple_ref, wgate_ref, wproj_ref, gfinal_ref, o_ref, *, final):
    h = h_ref[0]
    r = (_rms(h) * gple_ref[...]).astype(BF16)
    gate = jax.nn.sigmoid(jnp.dot(r, wgate_ref[...], preferred_element_type=F32))
    emb = jnp.dot(p_ref[0].astype(BF16), wproj_ref[...], preferred_element_type=F32)
    out = h + gate * emb
    if final:
        out = _rms(out) * gfinal_ref[...]
    o_ref[0] = out


def _ple(h, p, lw, g_final, final):
    B, S, D = h.shape
    Tt = TOKEN_TILE
    tile = lambda b, i: (b, i, 0)
    const2 = lambda b, i: (0, 0)
    return pl.pallas_call(
        functools.partial(_ple_kernel, final=final), grid=(B, S // Tt),
        in_specs=[
            pl.BlockSpec((1, Tt, D), tile),
            pl.BlockSpec((1, Tt, PLE_DIM), tile),
            pl.BlockSpec((1, D), const2),
            pl.BlockSpec((D, D), const2),
            pl.BlockSpec((PLE_DIM, D), const2),
            pl.BlockSpec((1, D), const2),
        ],
        out_specs=pl.BlockSpec((1, Tt, D), tile),
        out_shape=jax.ShapeDtypeStruct((B, S, D), F32),
        compiler_params=_params(("parallel", "parallel")), name="ple",
    )(h, p, lw["g_ple"], lw["w_ple_gate"], lw["w_ple_proj"], g_final)


def _rope_tables(seq_len):
    rows_n = seq_len // GRID_W
    row = jnp.repeat(jnp.arange(rows_n), GRID_W).astype(F32)[:, None]
    col = jnp.tile(jnp.arange(GRID_W), rows_n).astype(F32)[:, None]

    def axial(rot_dim):
        n_freq = rot_dim // 4
        inv = ROPE_THETA ** (-jnp.arange(n_freq, dtype=F32) / n_freq)
        ar, ac = row * inv, col * inv
        cos = jnp.concatenate([jnp.cos(ar), jnp.cos(ar), jnp.cos(ac), jnp.cos(ac)], axis=1)
        sin = jnp.concatenate([-jnp.sin(ar), jnp.sin(ar), -jnp.sin(ac), jnp.sin(ac)], axis=1)
        return cos, sin

    cos_a, sin_a = axial(HEAD_DIM)
    cos_r, sin_r = axial(C_ROPE)
    ones = jnp.ones((seq_len, C_NOPE), F32)
    zeros = jnp.zeros((seq_len, C_NOPE), F32)
    pad = jnp.zeros((seq_len, C_QK_PAD - C_NOPE - C_ROPE), F32)
    return dict(
        cos_a=jnp.concatenate([cos_a, cos_a], axis=1), sin_a=jnp.concatenate([sin_a, sin_a], axis=1),
        cos_c=jnp.concatenate([ones, cos_r, pad], axis=1), sin_c=jnp.concatenate([zeros, sin_r, pad], axis=1),
    )


def _pack_weights(w):
    L = DEPTH
    w_in = w["w_in"]
    kr_pad = jnp.pad(w_in[:, :, _KR0:], ((0, 0), (0, 0), (C_NOPE, C_QK_PAD - C_NOPE - C_ROPE)))
    q_scale = jnp.full((A_WIDTH,), HEAD_DIM ** -0.5 * LOG2E, F32)
    k_scale = jnp.ones((A_KV_HEADS * HEAD_DIM,), F32)
    g_qk = jnp.concatenate([jnp.tile(w["g_qa"], (1, A_HEADS)), jnp.tile(w["g_ka"], (1, A_KV_HEADS))], axis=1)
    g_qk = g_qk * jnp.concatenate([q_scale, k_scale])[None, :]
    w_q_up = jnp.pad(w["w_q_up"].reshape(L, C_Q_RANK, C_HEADS, C_NOPE + C_ROPE),
                     ((0, 0), (0, 0), (0, 0), (0, C_QK_PAD - C_NOPE - C_ROPE)))
    kv = w["w_kv_up"].reshape(L, C_KV_RANK, C_HEADS, C_NOPE + C_V)
    k_nope = jnp.pad(kv[..., :C_NOPE], ((0, 0), (0, 0), (0, 0), (0, C_QK_PAD - C_NOPE)))
    w_kv_up = jnp.concatenate([k_nope.reshape(L, C_KV_RANK, C_HEADS * C_QK_PAD),
                               kv[..., C_NOPE:].reshape(L, C_KV_RANK, C_WIDTH)], axis=2)
    eye = jnp.eye(B_GROUPS, dtype=F32)
    w_pool = (w["w_pool"][:, :, :, None, :] * eye[None, :, None, :, None]).reshape(L, B_WIDTH, B_WIDTH)
    w_router = jnp.concatenate([w["w_router_expert"], w["w_router_group"]], axis=2)
    w_router = jnp.pad(jnp.swapaxes(w_router, 1, 2), ((0, 0), (0, ROUTER_ROWS - N_EXPERTS - MOE_GROUPS), (0, 0)))
    w_router_hi = w_router.astype(BF16)
    w_router_lo = (w_router - w_router_hi.astype(F32)).astype(BF16)
    row = lambda a: a[:, None, :]
    return dict(
        g_mix=row(w["g_mix"]), w_in=jnp.concatenate([w_in[:, :, :_KR0], kr_pad], axis=2).astype(BF16),
        g_qk=row(g_qk), g_cq=row(w["g_cq"]), w_q_up=w_q_up.reshape(L, C_Q_RANK, C_HEADS * C_QK_PAD).astype(BF16),
        g_ckv=row(w["g_ckv"]), w_kv_up=w_kv_up.astype(BF16),
        w_pool=w_pool.astype(BF16), s_pool=row(w["s_pool"]),
        g_out_a=row(w["g_out_a"]), g_out_b=row(w["g_out_b"]), g_out_c=row(w["g_out_c"]),
        w_out=w["w_out"].astype(BF16), g_ffn=row(w["g_ffn"]),
        w_router_hi=w_router_hi, w_router_lo=w_router_lo,
        w_gate_up=jnp.concatenate([w["w_gate"], w["w_up"]], axis=3).astype(BF16),
        w_down=w["w_down"].astype(BF16),
        g_ple=row(w["g_ple"]), w_ple_gate=w["w_ple_gate"].astype(BF16), w_ple_proj=w["w_ple_proj"].astype(BF16),
    )


def _trunk(x, p, packed, g_final):
    B, S, D = x.shape
    tabs = _rope_tables(S)
    hs = np.kron(np.eye(_QK_A // HEAD_DIM, dtype=np.float32), np.ones((HEAD_DIM, HEAD_DIM), np.float32))
    tabs["hsum"] = jnp.asarray(hs, BF16)
    tm = min(MOE_TILE, S)
    tabs["before"] = jnp.asarray(np.triu(np.ones((tm, tm), np.float32), 1), BF16)
    tabs["lower"] = jnp.asarray(np.tril(np.ones((N_EXPERTS, N_EXPERTS), np.float32), -1), BF16)
    g_final = g_final[None, :]
    h = x
    for i in range(DEPTH):
        lw = {name: a[i] for name, a in packed.items()}
        qat, ka, vat, ub, qct, kc, vct = _premix(h, lw, tabs)
        oa = _attention(qat, ka, vat, heads_per_step=A_HEADS, group=A_HEADS // A_KV_HEADS, dv=HEAD_DIM)
        oc = _attention(qct, kc, vct, heads_per_step=LANES // C_V, group=1, dv=C_V)
        h1, m, route, offs, counts = _postmix(h, oa, oc, ub, lw, tabs)
        n_tiles = offs.shape[0] * offs.shape[1]
        plan = jnp.concatenate([offs[..., 0], counts[..., 0]], axis=-1).reshape(n_tiles, 2 * N_EXPERTS)
        h2 = _moe(m.reshape(B * S, D), route.reshape(n_tiles, 8, route.shape[-1]), plan, h1.reshape(B * S, D), lw)
        h = _ple(h2.reshape(B, S, D), p[i], lw, g_final, final=(i == DEPTH - 1))
    return h


def kernel(x_prompt, x_sample, p_prompt, p_sample, g_mix, w_in, g_qa, g_ka, w_pool, s_pool, g_cq, w_q_up, g_ckv, w_kv_up, g_out_a, g_out_b, g_out_c, w_out, g_ffn, w_router_group, w_router_expert, w_gate, w_up, w_down, g_ple, w_ple_gate, w_ple_proj, g_final):
    weights = dict(g_mix=g_mix, w_in=w_in, g_qa=g_qa, g_ka=g_ka, w_pool=w_pool, s_pool=s_pool, g_cq=g_cq,
                   w_q_up=w_q_up, g_ckv=g_ckv, w_kv_up=w_kv_up, g_out_a=g_out_a, g_out_b=g_out_b,
                   g_out_c=g_out_c, w_out=w_out, g_ffn=g_ffn, w_router_group=w_router_group,
                   w_router_expert=w_router_expert, w_gate=w_gate, w_up=w_up, w_down=w_down, g_ple=g_ple,
                   w_ple_gate=w_ple_gate, w_ple_proj=w_ple_proj)
    packed = _pack_weights(weights)
    nb = x_prompt.shape[0]
    x = jnp.concatenate([x_prompt, x_sample], axis=0)
    p = jnp.concatenate([p_prompt, p_sample], axis=1)
    y = _trunk(x, p, packed, g_final)
    return (y[:nb], y[nb:])
```

```python
import functools
import math

import numpy as np
import jax
import jax.numpy as jnp
from jax import lax
from jax.experimental import pallas as pl
from jax.experimental.pallas import tpu as pltpu

F32 = jnp.float32
BF16 = jnp.bfloat16

D_MODEL = 1024
DEPTH = 4
GRID_W = 64
PLE_DIM = 256
HEAD_DIM = 64
ROPE_THETA = 10000.0
EPS = 1e-6
A_HEADS = 6
A_KV_HEADS = 2
A_WIDTH = A_HEADS * HEAD_DIM
B_GROUPS = 4
B_GROUP_DIM = 64
B_WINDOWS = (2, 4, 8, 16)
B_WIDTH = B_GROUPS * B_GROUP_DIM
C_HEADS = 6
C_NOPE = 64
C_ROPE = 32
C_V = 64
C_Q_RANK = 256
C_KV_RANK = 128
C_WIDTH = C_HEADS * C_V
C_QK_PAD = 128
MOE_GROUPS = 4
EXPERTS_PER_GROUP = 8
N_EXPERTS = MOE_GROUPS * EXPERTS_PER_GROUP
D_EXPERT = 256
ROUTER_ROWS = 40
LOG2E = math.log2(math.e)

LANES = 128
VMEM_LIMIT_BYTES = 56 * 1024 * 1024

TOKEN_TILE = 512
Q_TILE_GQA = 256
Q_TILE_MLA = 512
ATTN_STREAMS = 2
KEY_SUB = 256
LOOKAHEAD = 2
MOE_TILE = 1024
SEG_ALIGN = 16
EXPERT_BLOCK = 128
MOE_TILE_GROUP = 2
DISPATCH_STRIP = 256
V_PAD = 16
POOL_HALO = 8

_QK_A = A_WIDTH + A_KV_HEADS * HEAD_DIM
_V_A0 = _QK_A
_U_B0 = _V_A0 + A_KV_HEADS * HEAD_DIM
_CQ0 = _U_B0 + B_WIDTH
_CKV0 = _CQ0 + C_Q_RANK
_KR0 = _CKV0 + C_KV_RANK
IN_COLS_PACKED = _KR0 + C_QK_PAD


def _params(sem):
    return pltpu.CompilerParams(dimension_semantics=sem, vmem_limit_bytes=VMEM_LIMIT_BYTES)


def _rms(x):
    return x * lax.rsqrt(jnp.mean(x * x, axis=-1, keepdims=True) + EPS)


def _swap_halves(x, half):
    n = x.shape[-1]
    lane = lax.broadcasted_iota(jnp.int32, x.shape, x.ndim - 1)
    return jnp.where((lane & half) == 0, pltpu.roll(x, n - half, x.ndim - 1), pltpu.roll(x, half, x.ndim - 1))


def _premix_kernel(h_ref, g_mix_ref, w_in_ref, hsum_ref, g_qk_ref, cosa_ref, sina_ref,
                   g_cq_ref, w_qup_ref, g_ckv_ref, w_kvup_ref, cosc_ref, sinc_ref,
                   qat_ref, ka_ref, vat_ref, ub_ref, qct_ref, kc_ref, vct_ref, *, c_scale):
    a = _rms(h_ref[0]) * g_mix_ref[...]
    z = jnp.dot(a.astype(BF16), w_in_ref[...], preferred_element_type=F32)

    qk = z[:, :_QK_A]
    ss = jnp.dot((qk * qk).astype(BF16), hsum_ref[...], preferred_element_type=F32)
    qk = qk * lax.rsqrt(ss * (1.0 / HEAD_DIM) + EPS) * g_qk_ref[...]
    reps = _QK_A // LANES
    cos_a = jnp.concatenate([cosa_ref[...]] * reps, axis=1)
    sin_a = jnp.concatenate([sina_ref[...]] * reps, axis=1)
    qk = qk * cos_a + _swap_halves(qk, HEAD_DIM // 4) * sin_a
    qat = qk[:, :A_WIDTH].T.astype(BF16)
    tt = qat.shape[1]
    a_group = A_HEADS // A_KV_HEADS
    for hh in range(A_HEADS):
        for qi in range(tt // Q_TILE_GQA):
            col = (qi * a_group + hh % a_group) * Q_TILE_GQA
            qat_ref[0, hh // a_group, :, col:col + Q_TILE_GQA] = qat[
                hh * HEAD_DIM:(hh + 1) * HEAD_DIM, qi * Q_TILE_GQA:(qi + 1) * Q_TILE_GQA]
    ka = qk[:, A_WIDTH:].astype(BF16)
    for hh in range(A_KV_HEADS):
        ka_ref[0, hh] = ka[:, hh * HEAD_DIM:(hh + 1) * HEAD_DIM]
    ones_rows = (lax.broadcasted_iota(jnp.int32, (V_PAD, tt), 0) == 0).astype(BF16)
    vt = z[:, _V_A0:_U_B0].T.astype(BF16)
    for hh in range(A_KV_HEADS):
        vat_ref[0, hh, 0] = jnp.concatenate([vt[hh * HEAD_DIM:(hh + 1) * HEAD_DIM], ones_rows], axis=0)

    ub_ref[0] = z[:, _U_B0:_CQ0]

    cqn = _rms(z[:, _CQ0:_CKV0]) * g_cq_ref[...]
    qc = jnp.dot(cqn.astype(BF16), w_qup_ref[...], preferred_element_type=F32)
    cos_c = cosc_ref[...]
    sin_c = sinc_ref[...]
    cos_q = jnp.concatenate([cos_c * c_scale] * C_HEADS, axis=1)
    sin_q = jnp.concatenate([sin_c * c_scale] * C_HEADS, axis=1)
    qct = (qc * cos_q + _swap_halves(qc, C_ROPE // 4) * sin_q).T.astype(BF16)
    ckvn = _rms(z[:, _CKV0:_KR0]) * g_ckv_ref[...]
    kv = jnp.dot(ckvn.astype(BF16), w_kvup_ref[...], preferred_element_type=F32)
    kr = z[:, _KR0:IN_COLS_PACKED]
    kr = kr * cos_c + _swap_halves(kr, C_ROPE // 4) * sin_c
    for hh in range(C_HEADS):
        qct_ref[0, hh] = qct[hh * C_QK_PAD:(hh + 1) * C_QK_PAD]
        kc_ref[0, hh] = (kv[:, hh * C_QK_PAD:(hh + 1) * C_QK_PAD] + kr).astype(BF16)
    vct = kv[:, C_HEADS * C_QK_PAD:].T.astype(BF16)
    for hh in range(C_HEADS):
        vct_ref[0, hh, 0] = jnp.concatenate([vct[hh * C_V:(hh + 1) * C_V], ones_rows], axis=0)


def _premix(h, lw, tabs):
    B, S, D = h.shape
    Tt = TOKEN_TILE
    nt = S // Tt
    tile = lambda b, i: (b, i, 0)
    const2 = lambda b, i: (0, 0)
    headmajor = lambda b, i: (b, 0, i, 0)
    headmajor_t = lambda b, i: (b, 0, 0, i)
    vt_map = lambda b, i: (b, 0, i, 0, 0)
    tab_map = lambda b, i: (i, 0)
    in_specs = [
        pl.BlockSpec((1, Tt, D), tile),
        pl.BlockSpec((1, D), const2),
        pl.BlockSpec((D, IN_COLS_PACKED), const2),
        pl.BlockSpec((_QK_A, _QK_A), const2),
        pl.BlockSpec((1, _QK_A), const2),
        pl.BlockSpec((Tt, LANES), tab_map),
        pl.BlockSpec((Tt, LANES), tab_map),
        pl.BlockSpec((1, C_Q_RANK), const2),
        pl.BlockSpec((C_Q_RANK, C_HEADS * C_QK_PAD), const2),
        pl.BlockSpec((1, C_KV_RANK), const2),
        pl.BlockSpec((C_KV_RANK, C_HEADS * C_QK_PAD + C_WIDTH), const2),
        pl.BlockSpec((Tt, LANES), tab_map),
        pl.BlockSpec((Tt, LANES), tab_map),
    ]
    out_shape = [
        jax.ShapeDtypeStruct((B, A_KV_HEADS, HEAD_DIM, S * (A_HEADS // A_KV_HEADS)), BF16),
        jax.ShapeDtypeStruct((B, A_KV_HEADS, S, HEAD_DIM), BF16),
        jax.ShapeDtypeStruct((B, A_KV_HEADS, nt, HEAD_DIM + V_PAD, Tt), BF16),
        jax.ShapeDtypeStruct((B, S, B_WIDTH), F32),
        jax.ShapeDtypeStruct((B, C_HEADS, C_QK_PAD, S), BF16),
        jax.ShapeDtypeStruct((B, C_HEADS, S, C_QK_PAD), BF16),
        jax.ShapeDtypeStruct((B, C_HEADS, nt, C_V + V_PAD, Tt), BF16),
    ]
    out_specs = [
        pl.BlockSpec((1, A_KV_HEADS, HEAD_DIM, Tt * (A_HEADS // A_KV_HEADS)), headmajor_t),
        pl.BlockSpec((1, A_KV_HEADS, Tt, HEAD_DIM), headmajor),
        pl.BlockSpec((1, A_KV_HEADS, 1, HEAD_DIM + V_PAD, Tt), vt_map),
        pl.BlockSpec((1, Tt, B_WIDTH), tile),
        pl.BlockSpec((1, C_HEADS, C_QK_PAD, Tt), headmajor_t),
        pl.BlockSpec((1, C_HEADS, Tt, C_QK_PAD), headmajor),
        pl.BlockSpec((1, C_HEADS, 1, C_V + V_PAD, Tt), vt_map),
    ]
    kern = functools.partial(_premix_kernel, c_scale=(C_NOPE + C_ROPE) ** -0.5 * LOG2E)
    return pl.pallas_call(
        kern, grid=(B, nt), in_specs=in_specs, out_specs=out_specs, out_shape=out_shape,
        compiler_params=_params(("parallel", "parallel")), name="premix",
    )(h, lw["g_mix"], lw["w_in"], tabs["hsum"], lw["g_qk"], tabs["cos_a"], tabs["sin_a"],
      lw["g_cq"], lw["w_q_up"], lw["g_ckv"], lw["w_kv_up"], tabs["cos_c"], tabs["sin_c"])


def _attn_kernel(qt_ref, k_ref, vt_ref, o_ref, s_buf, p_buf, *, order, n_chunks, chunk, sub, dv, group):
    n_streams, _, width = qt_ref.shape[1:]
    rows = vt_ref.shape[3]
    nsub = chunk // sub
    n_pos = len(order)
    per_body = 1 + max(coff for _, coff in order)
    assert LOOKAHEAD < n_pos and n_chunks % per_body == 0

    def score_piece(pos, j, r):
        si = order[pos][0]
        start = pl.multiple_of(j * chunk + r * sub, sub)
        s = jnp.dot(k_ref[0, si, pl.ds(start, sub), :], qt_ref[0, si], preferred_element_type=F32)
        s_buf[pos, r * sub:(r + 1) * sub, :] = s
        return jnp.max(s, axis=0, keepdims=True)

    def prob_piece(pos, r, m_b):
        s = s_buf[pos, r * sub:(r + 1) * sub, :]
        p_buf[pos, r * sub:(r + 1) * sub, :] = jnp.exp2(s - m_b).astype(BF16)

    def all_scores(pos, j):
        cm = None
        for r in range(nsub):
            c = score_piece(pos, j, r)
            cm = c if cm is None else jnp.maximum(cm, c)
        return cm

    def body(b, carry):
        pending, state = carry
        colmax = dict(enumerate(pending))
        state = list(state)
        for i, (si, coff) in enumerate(order):
            m, acc = state[si]
            m_new = jnp.maximum(m, colmax[i])
            m_b = jnp.broadcast_to(m_new, (sub, width))
            tpos = (i + LOOKAHEAD) % n_pos
            tj = jnp.minimum((b + (i + LOOKAHEAD) // n_pos) * per_body + order[tpos][1], n_chunks - 1)
            cm = None
            for r in range(nsub):
                c = score_piece(tpos, tj, r)
                cm = c if cm is None else jnp.maximum(cm, c)
                prob_piece(i, r, m_b)
            colmax[i + LOOKAHEAD] = cm
            pv = jnp.dot(vt_ref[0, si, b * per_body + coff], p_buf[i], preferred_element_type=F32)
            state[si] = (m_new, jnp.exp2(m - m_new) * acc + pv)
        return tuple(colmax[n_pos + k] for k in range(LOOKAHEAD)), tuple(state)

    pending = tuple(all_scores(k, order[k][1]) for k in range(LOOKAHEAD))
    init = tuple((jnp.full((1, width), -jnp.inf, F32), jnp.zeros((rows, width), F32)) for _ in range(n_streams))
    _, final = lax.fori_loop(0, n_chunks // per_body, body, (pending, init))
    tq = width // group
    outs = []
    for _, acc in final:
        out_t = acc[:dv] * (1.0 / acc[dv:dv + 1])
        outs += [out_t[:, g * tq:(g + 1) * tq] for g in range(group)]
    o_ref[0] = jnp.concatenate(outs, axis=0).T


def _attention(qt, k, vt, *, tq, group, dv):
    B, Hkv, dk, _ = qt.shape
    S = k.shape[2]
    nk, rows, Tk = vt.shape[2:]
    n_streams = ATTN_STREAMS
    width = group * tq
    per_body = 2
    order = tuple((si, coff) for coff in range(per_body) for si in range(n_streams))
    kern = functools.partial(_attn_kernel, order=order, n_chunks=nk, chunk=Tk, sub=KEY_SUB, dv=dv, group=group)
    return pl.pallas_call(
        kern, grid=(B, Hkv // n_streams, S // tq),
        in_specs=[
            pl.BlockSpec((1, n_streams, dk, width), lambda b, g, i: (b, g, 0, i)),
            pl.BlockSpec((1, n_streams, S, dk), lambda b, g, i: (b, g, 0, 0)),
            pl.BlockSpec((1, n_streams, nk, rows, Tk), lambda b, g, i: (b, g, 0, 0, 0)),
        ],
        out_specs=pl.BlockSpec((1, tq, n_streams * group * dv), lambda b, g, i: (b, i, g)),
        out_shape=jax.ShapeDtypeStruct((B, S, Hkv * group * dv), F32),
        scratch_shapes=[pltpu.VMEM((len(order), Tk, width), F32), pltpu.VMEM((len(order), Tk, width), BF16)],
        compiler_params=_params(("parallel", "parallel", "arbitrary")), name="attention",
    )(qt, k, vt)


def _route(logits):
    tt = logits.shape[1]
    le = logits[:N_EXPERTS]
    lg = logits[N_EXPERTS:]
    row8 = lax.broadcasted_iota(jnp.int32, (ROUTER_ROWS - N_EXPERTS, tt), 0)
    lg = jnp.where(row8 < MOE_GROUPS, lg, -jnp.inf)
    gmax = jnp.max(lg, axis=0, keepdims=True)
    gsel = jnp.min(jnp.where(lg == gmax, row8, MOE_GROUPS), axis=0, keepdims=True)
    gprob = 1.0 / jnp.sum(jnp.exp(lg - gmax), axis=0, keepdims=True)
    row = lax.broadcasted_iota(jnp.int32, (N_EXPERTS, tt), 0)
    lm = jnp.where((row // EXPERTS_PER_GROUP) == gsel, le, -jnp.inf)
    m1 = jnp.max(lm, axis=0, keepdims=True)
    i1 = jnp.min(jnp.where(lm == m1, row, N_EXPERTS), axis=0, keepdims=True)
    lm2 = jnp.where(row == i1, -jnp.inf, lm)
    m2 = jnp.max(lm2, axis=0, keepdims=True)
    i2 = jnp.min(jnp.where(lm2 == m2, row, N_EXPERTS), axis=0, keepdims=True)
    r = jnp.exp(m2 - m1)
    w1 = gprob / (1.0 + r)
    w2 = w1 * r
    return i1, i2, w1, w2


def _dispatch_plan(i1, i2, w1, w2, before_ref, lower_ref):
    tt = i1.shape[1]
    row = lax.broadcasted_iota(jnp.int32, (N_EXPERTS, tt), 0)
    hit1 = row == i1
    hit2 = row == i2
    assign = jnp.where(hit1 | hit2, 1.0, 0.0)
    rank = jnp.dot(assign.astype(BF16), before_ref[...], preferred_element_type=F32)
    count = jnp.sum(assign, axis=1, keepdims=True)
    segs = jnp.floor((count + (SEG_ALIGN - 1.0)) * (1.0 / SEG_ALIGN))
    segs_b = jnp.broadcast_to(segs, (N_EXPERTS, LANES)).astype(BF16)
    offs = SEG_ALIGN * jnp.dot(lower_ref[...], segs_b, preferred_element_type=F32)
    pos = offs[:, :1] + rank
    pos1 = jnp.sum(jnp.where(hit1, pos, 0.0), axis=0, keepdims=True)
    pos2 = jnp.sum(jnp.where(hit2, pos, 0.0), axis=0, keepdims=True)
    route = jnp.concatenate([pos1, pos2, w1, w2, jnp.zeros((4, tt), F32)], axis=0)
    counts = jnp.broadcast_to(count, (N_EXPERTS, LANES))
    return route, offs.astype(jnp.int32), counts.astype(jnp.int32)


def _postmix_kernel(h_ref, oa_ref, oc_ref, ub_ref, prev_ref, next_ref, wpool_ref, spool_ref,
                    ga_ref, gb_ref, gc_ref, wout_ref, gffn_ref, wr_hi_ref, wr_lo_ref, before_ref, lower_ref,
                    h1_ref, m_ref, route_ref, offs_ref, counts_ref, ext_ref, *, seq_len):
    i = pl.program_id(1)
    nt = pl.num_programs(1)
    tt = h_ref.shape[1]
    half = B_WIDTH // 2

    u = ub_ref[0]
    ext_ref[POOL_HALO:POOL_HALO + tt, :] = u
    ext_ref[:POOL_HALO, :] = jnp.where(i > 0, prev_ref[0, 0], 0.0)
    ext_ref[POOL_HALO + tt:, :] = jnp.where(i < nt - 1, next_ref[0, 0], 0.0)

    def window(lo, hi, lanes):
        acc = None
        for d in range(lo, hi):
            piece = ext_ref[POOL_HALO + d:POOL_HALO + d + tt, lanes]
            acc = piece if acc is None else acc + piece
        return acc

    lo_lanes = slice(0, half)
    hi_lanes = slice(half, B_WIDTH)
    w2 = window(-1, 1, lo_lanes)
    w4 = w2 + window(-2, -1, lo_lanes) + window(1, 2, lo_lanes)
    w8 = window(-4, 4, hi_lanes)
    w16 = w8 + window(-8, -4, hi_lanes) + window(4, 8, hi_lanes)
    lane = lax.broadcasted_iota(jnp.int32, (tt, half), 1)
    first = lane < B_GROUP_DIM
    sums = jnp.concatenate([jnp.where(first, w2, w4), jnp.where(first, w8, w16)], axis=1)
    pos = i * tt + lax.broadcasted_iota(jnp.int32, (tt, B_WIDTH), 0)
    lane_b = lax.broadcasted_iota(jnp.int32, (tt, B_WIDTH), 1)
    hw = jnp.left_shift(1, lane_b // B_GROUP_DIM)
    cnt = jnp.minimum(pos + hw, seq_len) - jnp.maximum(pos - hw, 0)
    dlt = sums / cnt.astype(F32) - u
    ob = jnp.dot(dlt.astype(BF16), wpool_ref[...], preferred_element_type=F32) * spool_ref[...]

    merged = jnp.concatenate([
        (_rms(oa_ref[0]) * ga_ref[...]).astype(BF16),
        (_rms(ob) * gb_ref[...]).astype(BF16),
        (_rms(oc_ref[0]) * gc_ref[...]).astype(BF16)], axis=1)
    h1 = h_ref[0] + jnp.dot(merged, wout_ref[...], preferred_element_type=F32)
    h1_ref[0] = h1

    m = _rms(h1) * gffn_ref[...]
    m_hi = m.astype(BF16)
    m_ref[0] = m_hi
    m_lo = (m - m_hi.astype(F32)).astype(BF16)
    nt_dims = (((1,), (1,)), ((), ()))
    logits = (lax.dot_general(wr_hi_ref[...], m_hi, nt_dims, preferred_element_type=F32)
              + lax.dot_general(wr_lo_ref[...], m_hi, nt_dims, preferred_element_type=F32)
              + lax.dot_general(wr_hi_ref[...], m_lo, nt_dims, preferred_element_type=F32))
    route, offs, counts = _dispatch_plan(*_route(logits), before_ref, lower_ref)
    route_ref[0, 0] = route
    offs_ref[0, 0] = offs
    counts_ref[0, 0] = counts


def _postmix(h, oa, oc, ub, lw, tabs):
    B, S, D = h.shape
    Tt = min(MOE_TILE, S)
    nt = S // Tt
    rows = Tt // POOL_HALO
    ub_rows = ub.reshape(B, S // POOL_HALO, POOL_HALO, B_WIDTH)
    tile = lambda b, i: (b, i, 0)
    const2 = lambda b, i: (0, 0)
    in_specs = [
        pl.BlockSpec((1, Tt, D), tile),
        pl.BlockSpec((1, Tt, A_WIDTH), tile),
        pl.BlockSpec((1, Tt, C_WIDTH), tile),
        pl.BlockSpec((1, Tt, B_WIDTH), tile),
        pl.BlockSpec((1, 1, POOL_HALO, B_WIDTH), lambda b, i: (b, jnp.maximum(i * rows - 1, 0), 0, 0)),
        pl.BlockSpec((1, 1, POOL_HALO, B_WIDTH),
                     lambda b, i: (b, jnp.minimum((i + 1) * rows, S // POOL_HALO - 1), 0, 0)),
        pl.BlockSpec((B_WIDTH, B_WIDTH), const2),
        pl.BlockSpec((1, B_WIDTH), const2),
        pl.BlockSpec((1, A_WIDTH), const2),
        pl.BlockSpec((1, B_WIDTH), const2),
        pl.BlockSpec((1, C_WIDTH), const2),
        pl.BlockSpec((D, D), const2),
        pl.BlockSpec((1, D), const2),
        pl.BlockSpec((ROUTER_ROWS, D), const2),
        pl.BlockSpec((ROUTER_ROWS, D), const2),
        pl.BlockSpec((Tt, Tt), const2),
        pl.BlockSpec((N_EXPERTS, N_EXPERTS), const2),
    ]
    per_tile = lambda b, i: (b, i, 0, 0)
    out_shape = [
        jax.ShapeDtypeStruct((B, S, D), F32),
        jax.ShapeDtypeStruct((B, S, D), BF16),
        jax.ShapeDtypeStruct((B, nt, 8, Tt), F32),
        jax.ShapeDtypeStruct((B, nt, N_EXPERTS, LANES), jnp.int32),
        jax.ShapeDtypeStruct((B, nt, N_EXPERTS, LANES), jnp.int32),
    ]
    out_specs = [
        pl.BlockSpec((1, Tt, D), tile),
        pl.BlockSpec((1, Tt, D), tile),
        pl.BlockSpec((1, 1, 8, Tt), per_tile),
        pl.BlockSpec((1, 1, N_EXPERTS, LANES), per_tile),
        pl.BlockSpec((1, 1, N_EXPERTS, LANES), per_tile),
    ]
    return pl.pallas_call(
        functools.partial(_postmix_kernel, seq_len=S), grid=(B, nt),
        in_specs=in_specs, out_specs=out_specs, out_shape=out_shape,
        scratch_shapes=[pltpu.VMEM((Tt + 2 * POOL_HALO, B_WIDTH), F32)],
        compiler_params=_params(("parallel", "parallel")), name="postmix",
    )(h, oa, oc, ub, ub_rows, ub_rows, lw["w_pool"], lw["s_pool"], lw["g_out_a"], lw["g_out_b"],
      lw["g_out_c"], lw["w_out"], lw["g_ffn"], lw["w_router_hi"], lw["w_router_lo"],
      tabs["before"], tabs["lower"])


def _moe_kernel(plan_ref, m_ref, route_ref, wgu_ref, wd_ref, o_ref, perm_ref, xy_ref, ws_ref, *, n_rows):
    g = pl.program_id(0)
    e = pl.program_id(1)
    n_tiles, _, tm = route_ref.shape
    d = m_ref.shape[1]

    @pl.when(e == 0)
    def _():
        for k in range(n_tiles):
            route = route_ref[k]
            pos1 = route[0:1].astype(jnp.int32)
            pos2 = route[1:2].astype(jnp.int32)
            w1 = route[2:3]
            w2 = route[3:4]
            m = m_ref[k * tm:(k + 1) * tm, :]

            def strip(i, _, k=k, pos1=pos1, pos2=pos2, w1=w1, w2=w2, m=m):
                r0 = pl.multiple_of(i * DISPATCH_STRIP, DISPATCH_STRIP)
                rows = r0 + lax.broadcasted_iota(jnp.int32, (DISPATCH_STRIP, tm), 0)
                hit1 = rows == pos1
                hit2 = rows == pos2
                perm = jnp.where(hit1 | hit2, 1.0, 0.0).astype(BF16)
                perm_ref[k, pl.ds(r0, DISPATCH_STRIP), :] = perm
                ws_ref[k, pl.ds(r0, DISPATCH_STRIP), :] = jnp.sum(
                    jnp.where(hit1, w1, jnp.where(hit2, w2, 0.0)), axis=1, keepdims=True)
                xy_ref[k, pl.ds(r0, DISPATCH_STRIP), :] = jnp.dot(
                    perm, m, preferred_element_type=F32).astype(BF16)
                return 0

            lax.fori_loop(0, n_rows // DISPATCH_STRIP, strip, 0)
            xy_ref[k, n_rows:, :] = jnp.zeros((EXPERT_BLOCK, d), BF16)
            ws_ref[k, n_rows:, :] = jnp.zeros((EXPERT_BLOCK, 1), F32)

    def blocks(items):
        starts = [pl.multiple_of(off + bi * EXPERT_BLOCK, SEG_ALIGN) for _, off, _, bi in items]
        xs = [xy_ref[k, pl.ds(r0, EXPERT_BLOCK), :] for (k, _, _, _), r0 in zip(items, starts)]
        ws = [ws_ref[k, pl.ds(r0, EXPERT_BLOCK), :] for (k, _, _, _), r0 in zip(items, starts)]
        gu = jnp.dot(jnp.concatenate(xs, axis=0), wgu_ref[0], preferred_element_type=F32)
        gate = gu[:, :D_EXPERT]
        hid = gate * jax.nn.sigmoid(gate) * gu[:, D_EXPERT:] * jnp.concatenate(ws, axis=0)
        y = jnp.dot(hid.astype(BF16), wd_ref[0], preferred_element_type=F32).astype(BF16)
        row = lax.broadcasted_iota(jnp.int32, (EXPERT_BLOCK, 1), 0)
        for n, ((k, _, cnt, bi), r0, x) in enumerate(zip(items, starts, xs)):
            in_segment = row < cnt - bi * EXPERT_BLOCK
            xy_ref[k, pl.ds(r0, EXPERT_BLOCK), :] = jnp.where(
                in_segment, y[n * EXPERT_BLOCK:(n + 1) * EXPERT_BLOCK], x)

    plans = [(plan_ref[g * n_tiles + k, e], plan_ref[g * n_tiles + k, N_EXPERTS + e]) for k in range(n_tiles)]
    blocks([(k, off, cnt, 0) for k, (off, cnt) in enumerate(plans)])
    for k, (off, cnt) in enumerate(plans):
        n_blocks = lax.shift_right_logical(cnt + (EXPERT_BLOCK - 1), EXPERT_BLOCK.bit_length() - 1)

        def more(bi, _, k=k, off=off, cnt=cnt):
            blocks([(k, off, cnt, bi)])
            return 0

        lax.fori_loop(1, n_blocks, more, 0)

    @pl.when(e == pl.num_programs(1) - 1)
    def _():
        tn_dims = (((0,), (0,)), ((), ()))
        for k in range(n_tiles):
            y = lax.dot_general(perm_ref[k], xy_ref[k, :n_rows, :], tn_dims, preferred_element_type=F32)
            o_ref[k * tm:(k + 1) * tm, :] = y.astype(BF16)


def _moe(m, route, plan, lw):
    T, D = m.shape
    n_all, _, Tm = route.shape
    group = MOE_TILE_GROUP if n_all % MOE_TILE_GROUP == 0 else 1
    n_rows = 2 * Tm + N_EXPERTS * SEG_ALIGN
    assert n_rows % DISPATCH_STRIP == 0
    tile = lambda g, e, plan: (g, 0)
    grid_spec = pltpu.PrefetchScalarGridSpec(
        num_scalar_prefetch=1, grid=(n_all // group, N_EXPERTS),
        in_specs=[
            pl.BlockSpec((group * Tm, D), tile),
            pl.BlockSpec((group, 8, Tm), lambda g, e, plan: (g, 0, 0)),
            pl.BlockSpec((1, D, 2 * D_EXPERT), lambda g, e, plan: (e, 0, 0)),
            pl.BlockSpec((1, D_EXPERT, D), lambda g, e, plan: (e, 0, 0)),
        ],
        out_specs=pl.BlockSpec((group * Tm, D), tile),
        scratch_shapes=[
            pltpu.VMEM((group, n_rows, Tm), BF16),
            pltpu.VMEM((group, n_rows + EXPERT_BLOCK, D), BF16),
            pltpu.VMEM((group, n_rows + EXPERT_BLOCK, 1), F32),
        ],
    )
    return pl.pallas_call(
        functools.partial(_moe_kernel, n_rows=n_rows), grid_spec=grid_spec,
        out_shape=jax.ShapeDtypeStruct((T, D), BF16),
        compiler_params=_params(("parallel", "arbitrary")), name="moe",
    )(plan, m, route, lw["w_gate_up"], lw["w_down"])


def _ple_kernel(h1_ref, y_ref, p_ref, gple_ref, wgate_ref, wproj_ref, gfinal_ref, o_ref, *, final):
    h = h1_ref[0] + y_ref[0].astype(F32)
    r = (_rms(h) * gple_ref[...]).astype(BF16)
    gate = jax.nn.sigmoid(jnp.dot(r, wgate_ref[...], preferred_element_type=F32))
    emb = jnp.dot(p_ref[0].astype(BF16), wproj_ref[...], preferred_element_type=F32)
    out = h + gate * emb
    if final:
        out = _rms(out) * gfinal_ref[...]
    o_ref[0] = out


def _ple(h1, y, p, lw, g_final, final):
    B, S, D = h1.shape
    Tt = TOKEN_TILE
    tile = lambda b, i: (b, i, 0)
    const2 = lambda b, i: (0, 0)
    return pl.pallas_call(
        functools.partial(_ple_kernel, final=final), grid=(B, S // Tt),
        in_specs=[
            pl.BlockSpec((1, Tt, D), tile),
            pl.BlockSpec((1, Tt, D), tile),
            pl.BlockSpec((1, Tt, PLE_DIM), tile),
            pl.BlockSpec((1, D), const2),
            pl.BlockSpec((D, D), const2),
            pl.BlockSpec((PLE_DIM, D), const2),
            pl.BlockSpec((1, D), const2),
        ],
        out_specs=pl.BlockSpec((1, Tt, D), tile),
        out_shape=jax.ShapeDtypeStruct((B, S, D), F32),
        compiler_params=_params(("parallel", "parallel")), name="ple",
    )(h1, y, p, lw["g_ple"], lw["w_ple_gate"], lw["w_ple_proj"], g_final)


def _rope_tables(seq_len):
    rows_n = seq_len // GRID_W
    row = jnp.repeat(jnp.arange(rows_n), GRID_W).astype(F32)[:, None]
    col = jnp.tile(jnp.arange(GRID_W), rows_n).astype(F32)[:, None]

    def axial(rot_dim):
        n_freq = rot_dim // 4
        inv = ROPE_THETA ** (-jnp.arange(n_freq, dtype=F32) / n_freq)
        ar, ac = row * inv, col * inv
        cos = jnp.concatenate([jnp.cos(ar), jnp.cos(ar), jnp.cos(ac), jnp.cos(ac)], axis=1)
        sin = jnp.concatenate([-jnp.sin(ar), jnp.sin(ar), -jnp.sin(ac), jnp.sin(ac)], axis=1)
        return cos, sin

    cos_a, sin_a = axial(HEAD_DIM)
    cos_r, sin_r = axial(C_ROPE)
    ones = jnp.ones((seq_len, C_NOPE), F32)
    zeros = jnp.zeros((seq_len, C_NOPE), F32)
    pad = jnp.zeros((seq_len, C_QK_PAD - C_NOPE - C_ROPE), F32)
    return dict(
        cos_a=jnp.concatenate([cos_a, cos_a], axis=1), sin_a=jnp.concatenate([sin_a, sin_a], axis=1),
        cos_c=jnp.concatenate([ones, cos_r, pad], axis=1), sin_c=jnp.concatenate([zeros, sin_r, pad], axis=1),
    )


def _pack_weights(w):
    L = DEPTH
    w_in = w["w_in"]
    kr_pad = jnp.pad(w_in[:, :, _KR0:], ((0, 0), (0, 0), (C_NOPE, C_QK_PAD - C_NOPE - C_ROPE)))
    q_scale = jnp.full((A_WIDTH,), HEAD_DIM ** -0.5 * LOG2E, F32)
    k_scale = jnp.ones((A_KV_HEADS * HEAD_DIM,), F32)
    g_qk = jnp.concatenate([jnp.tile(w["g_qa"], (1, A_HEADS)), jnp.tile(w["g_ka"], (1, A_KV_HEADS))], axis=1)
    g_qk = g_qk * jnp.concatenate([q_scale, k_scale])[None, :]
    w_q_up = jnp.pad(w["w_q_up"].reshape(L, C_Q_RANK, C_HEADS, C_NOPE + C_ROPE),
                     ((0, 0), (0, 0), (0, 0), (0, C_QK_PAD - C_NOPE - C_ROPE)))
    kv = w["w_kv_up"].reshape(L, C_KV_RANK, C_HEADS, C_NOPE + C_V)
    k_nope = jnp.pad(kv[..., :C_NOPE], ((0, 0), (0, 0), (0, 0), (0, C_QK_PAD - C_NOPE)))
    w_kv_up = jnp.concatenate([k_nope.reshape(L, C_KV_RANK, C_HEADS * C_QK_PAD),
                               kv[..., C_NOPE:].reshape(L, C_KV_RANK, C_WIDTH)], axis=2)
    eye = jnp.eye(B_GROUPS, dtype=F32)
    w_pool = (w["w_pool"][:, :, :, None, :] * eye[None, :, None, :, None]).reshape(L, B_WIDTH, B_WIDTH)
    w_router = jnp.concatenate([w["w_router_expert"], w["w_router_group"]], axis=2)
    w_router = jnp.pad(jnp.swapaxes(w_router, 1, 2), ((0, 0), (0, ROUTER_ROWS - N_EXPERTS - MOE_GROUPS), (0, 0)))
    w_router_hi = w_router.astype(BF16)
    w_router_lo = (w_router - w_router_hi.astype(F32)).astype(BF16)
    row = lambda a: a[:, None, :]
    return dict(
        g_mix=row(w["g_mix"]), w_in=jnp.concatenate([w_in[:, :, :_KR0], kr_pad], axis=2).astype(BF16),
        g_qk=row(g_qk), g_cq=row(w["g_cq"]), w_q_up=w_q_up.reshape(L, C_Q_RANK, C_HEADS * C_QK_PAD).astype(BF16),
        g_ckv=row(w["g_ckv"]), w_kv_up=w_kv_up.astype(BF16),
        w_pool=w_pool.astype(BF16), s_pool=row(w["s_pool"]),
        g_out_a=row(w["g_out_a"]), g_out_b=row(w["g_out_b"]), g_out_c=row(w["g_out_c"]),
        w_out=w["w_out"].astype(BF16), g_ffn=row(w["g_ffn"]),
        w_router_hi=w_router_hi, w_router_lo=w_router_lo,
        w_gate_up=jnp.concatenate([w["w_gate"], w["w_up"]], axis=3).astype(BF16),
        w_down=w["w_down"].astype(BF16),
        g_ple=row(w["g_ple"]), w_ple_gate=w["w_ple_gate"].astype(BF16), w_ple_proj=w["w_ple_proj"].astype(BF16),
    )


def _trunk(x, p, packed, g_final):
    B, S, D = x.shape
    tabs = _rope_tables(S)
    hs = np.kron(np.eye(_QK_A // HEAD_DIM, dtype=np.float32), np.ones((HEAD_DIM, HEAD_DIM), np.float32))
    tabs["hsum"] = jnp.asarray(hs, BF16)
    tm = min(MOE_TILE, S)
    tabs["before"] = jnp.asarray(np.triu(np.ones((tm, tm), np.float32), 1), BF16)
    tabs["lower"] = jnp.asarray(np.tril(np.ones((N_EXPERTS, N_EXPERTS), np.float32), -1), BF16)
    g_final = g_final[None, :]
    h = x
    for i in range(DEPTH):
        lw = {name: a[i] for name, a in packed.items()}
        qat, ka, vat, ub, qct, kc, vct = _premix(h, lw, tabs)
        oa = _attention(qat, ka, vat, tq=Q_TILE_GQA, group=A_HEADS // A_KV_HEADS, dv=HEAD_DIM)
        oc = _attention(qct, kc, vct, tq=Q_TILE_MLA, group=1, dv=C_V)
        h1, m, route, offs, counts = _postmix(h, oa, oc, ub, lw, tabs)
        n_tiles = offs.shape[0] * offs.shape[1]
        plan = jnp.concatenate([offs[..., 0], counts[..., 0]], axis=-1).reshape(n_tiles, 2 * N_EXPERTS)
        y = _moe(m.reshape(B * S, D), route.reshape(n_tiles, 8, route.shape[-1]), plan, lw)
        h = _ple(h1, y.reshape(B, S, D), p[i], lw, g_final, final=(i == DEPTH - 1))
    return h


def kernel(x_prompt, x_sample, p_prompt, p_sample, g_mix, w_in, g_qa, g_ka, w_pool, s_pool, g_cq, w_q_up, g_ckv, w_kv_up, g_out_a, g_out_b, g_out_c, w_out, g_ffn, w_router_group, w_router_expert, w_gate, w_up, w_down, g_ple, w_ple_gate, w_ple_proj, g_final):
    weights = dict(g_mix=g_mix, w_in=w_in, g_qa=g_qa, g_ka=g_ka, w_pool=w_pool, s_pool=s_pool, g_cq=g_cq,
                   w_q_up=w_q_up, g_ckv=g_ckv, w_kv_up=w_kv_up, g_out_a=g_out_a, g_out_b=g_out_b,
                   g_out_c=g_out_c, w_out=w_out, g_ffn=g_ffn, w_router_group=w_router_group,
                   w_router_expert=w_router_expert, w_gate=w_gate, w_up=w_up, w_down=w_down, g_ple=g_ple,
                   w_ple_gate=w_ple_gate, w_ple_proj=w_ple_proj)
    packed = _pack_weights(weights)
    nb = x_prompt.shape[0]
    x = jnp.concatenate([x_prompt, x_sample], axis=0)
    p = jnp.concatenate([p_prompt, p_sample], axis=1)
    y = _trunk(x, p, packed, g_final)
    return (y[:nb], y[nb:])
```

```python
import functools
import math

import numpy as np
import jax
import jax.numpy as jnp
from jax import lax
from jax.experimental import pallas as pl
from jax.experimental.pallas import tpu as pltpu

F32 = jnp.float32
BF16 = jnp.bfloat16

D_MODEL = 1024
DEPTH = 4
GRID_W = 64
PLE_DIM = 256
HEAD_DIM = 64
ROPE_THETA = 10000.0
EPS = 1e-6
A_HEADS = 6
A_KV_HEADS = 2
A_WIDTH = A_HEADS * HEAD_DIM
B_GROUPS = 4
B_GROUP_DIM = 64
B_WINDOWS = (2, 4, 8, 16)
B_WIDTH = B_GROUPS * B_GROUP_DIM
C_HEADS = 6
C_NOPE = 64
C_ROPE = 32
C_V = 64
C_Q_RANK = 256
C_KV_RANK = 128
C_WIDTH = C_HEADS * C_V
C_QK_PAD = 128
MOE_GROUPS = 4
EXPERTS_PER_GROUP = 8
N_EXPERTS = MOE_GROUPS * EXPERTS_PER_GROUP
D_EXPERT = 256
ROUTER_ROWS = 40
LOG2E = math.log2(math.e)

LANES = 128
VMEM_LIMIT_BYTES = 56 * 1024 * 1024

TOKEN_TILE = 512
Q_TILE_GQA = 256
Q_TILE_MLA = 512
ATTN_STREAMS = 2
KEY_SUB = 256
LOOKAHEAD = 2
MOE_TILE = 1024
MOE_SUB = 256
SEG_ALIGN = 16
EXPERT_BLOCK = 32
MOE_TILE_GROUP = 8
DISPATCH_STRIP = 256
V_PAD = 16
POOL_HALO = 8

_QK_A = A_WIDTH + A_KV_HEADS * HEAD_DIM
_V_A0 = _QK_A
_U_B0 = _V_A0 + A_KV_HEADS * HEAD_DIM
_CQ0 = _U_B0 + B_WIDTH
_CKV0 = _CQ0 + C_Q_RANK
_KR0 = _CKV0 + C_KV_RANK
IN_COLS_PACKED = _KR0 + C_QK_PAD


def _params(sem):
    return pltpu.CompilerParams(dimension_semantics=sem, vmem_limit_bytes=VMEM_LIMIT_BYTES)


def _rms(x):
    return x * lax.rsqrt(jnp.mean(x * x, axis=-1, keepdims=True) + EPS)


def _swap_halves(x, half):
    n = x.shape[-1]
    lane = lax.broadcasted_iota(jnp.int32, x.shape, x.ndim - 1)
    return jnp.where((lane & half) == 0, pltpu.roll(x, n - half, x.ndim - 1), pltpu.roll(x, half, x.ndim - 1))


def _rope_partner(x):
    return pltpu.roll(x, x.shape[-1] - C_ROPE, x.ndim - 1)


def _premix_kernel(h_ref, g_mix_ref, w_in_ref, hsum_ref, g_qk_ref, cosa_ref, sina_ref,
                   g_cq_ref, w_qup_ref, g_ckv_ref, w_kvup_ref, cosc_ref, sinc_ref,
                   qat_ref, ka_ref, vat_ref, ub_ref, qct_ref, kc_ref, vct_ref, *, c_scale):
    a = _rms(h_ref[0]) * g_mix_ref[...]
    z = jnp.dot(a.astype(BF16), w_in_ref[...], preferred_element_type=F32)

    qk = z[:, :_QK_A]
    ss = jnp.dot((qk * qk).astype(BF16), hsum_ref[...], preferred_element_type=F32)
    qk = qk * lax.rsqrt(ss * (1.0 / HEAD_DIM) + EPS) * g_qk_ref[...]
    reps = _QK_A // LANES
    cos_a = jnp.concatenate([cosa_ref[...]] * reps, axis=1)
    sin_a = jnp.concatenate([sina_ref[...]] * reps, axis=1)
    qk = qk * cos_a + _swap_halves(qk, HEAD_DIM // 4) * sin_a
    qat = qk[:, :A_WIDTH].T.astype(BF16)
    tt = qat.shape[1]
    a_group = A_HEADS // A_KV_HEADS
    for hh in range(A_HEADS):
        for qi in range(tt // Q_TILE_GQA):
            col = (qi * a_group + hh % a_group) * Q_TILE_GQA
            qat_ref[0, hh // a_group, :, col:col + Q_TILE_GQA] = qat[
                hh * HEAD_DIM:(hh + 1) * HEAD_DIM, qi * Q_TILE_GQA:(qi + 1) * Q_TILE_GQA]
    ka = qk[:, A_WIDTH:].astype(BF16)
    for hh in range(A_KV_HEADS):
        ka_ref[0, hh] = ka[:, hh * HEAD_DIM:(hh + 1) * HEAD_DIM]
    ones_rows = (lax.broadcasted_iota(jnp.int32, (V_PAD, tt), 0) == 0).astype(BF16)
    vt = z[:, _V_A0:_U_B0].T.astype(BF16)
    for hh in range(A_KV_HEADS):
        vat_ref[0, hh, 0] = jnp.concatenate([vt[hh * HEAD_DIM:(hh + 1) * HEAD_DIM], ones_rows], axis=0)

    ub_ref[0] = z[:, _U_B0:_CQ0]

    cqn = _rms(z[:, _CQ0:_CKV0]) * g_cq_ref[...]
    qc = jnp.dot(cqn.astype(BF16), w_qup_ref[...], preferred_element_type=F32)
    cos_c = cosc_ref[...]
    sin_c = sinc_ref[...]
    cos_q = jnp.concatenate([cos_c * c_scale] * C_HEADS, axis=1)
    sin_q = jnp.concatenate([sin_c * c_scale] * C_HEADS, axis=1)
    qct = (qc * cos_q + _rope_partner(qc) * sin_q).T.astype(BF16)
    ckvn = _rms(z[:, _CKV0:_KR0]) * g_ckv_ref[...]
    kv = jnp.dot(ckvn.astype(BF16), w_kvup_ref[...], preferred_element_type=F32)
    kr = z[:, _KR0:IN_COLS_PACKED]
    kr = kr * cos_c + _rope_partner(kr) * sin_c
    for hh in range(C_HEADS):
        qct_ref[0, hh] = qct[hh * C_QK_PAD:(hh + 1) * C_QK_PAD]
        kc_ref[0, hh] = (kv[:, hh * C_QK_PAD:(hh + 1) * C_QK_PAD] + kr).astype(BF16)
    vct = kv[:, C_HEADS * C_QK_PAD:].T.astype(BF16)
    for hh in range(C_HEADS):
        vct_ref[0, hh, 0] = jnp.concatenate([vct[hh * C_V:(hh + 1) * C_V], ones_rows], axis=0)


def _premix(h, lw, tabs):
    B, S, D = h.shape
    Tt = TOKEN_TILE
    nt = S // Tt
    tile = lambda b, i: (b, i, 0)
    const2 = lambda b, i: (0, 0)
    headmajor = lambda b, i: (b, 0, i, 0)
    headmajor_t = lambda b, i: (b, 0, 0, i)
    vt_map = lambda b, i: (b, 0, i, 0, 0)
    tab_map = lambda b, i: (i, 0)
    in_specs = [
        pl.BlockSpec((1, Tt, D), tile),
        pl.BlockSpec((1, D), const2),
        pl.BlockSpec((D, IN_COLS_PACKED), const2),
        pl.BlockSpec((_QK_A, _QK_A), const2),
        pl.BlockSpec((1, _QK_A), const2),
        pl.BlockSpec((Tt, LANES), tab_map),
        pl.BlockSpec((Tt, LANES), tab_map),
        pl.BlockSpec((1, C_Q_RANK), const2),
        pl.BlockSpec((C_Q_RANK, C_HEADS * C_QK_PAD), const2),
        pl.BlockSpec((1, C_KV_RANK), const2),
        pl.BlockSpec((C_KV_RANK, C_HEADS * C_QK_PAD + C_WIDTH), const2),
        pl.BlockSpec((Tt, LANES), tab_map),
        pl.BlockSpec((Tt, LANES), tab_map),
    ]
    out_shape = [
        jax.ShapeDtypeStruct((B, A_KV_HEADS, HEAD_DIM, S * (A_HEADS // A_KV_HEADS)), BF16),
        jax.ShapeDtypeStruct((B, A_KV_HEADS, S, HEAD_DIM), BF16),
        jax.ShapeDtypeStruct((B, A_KV_HEADS, nt, HEAD_DIM + V_PAD, Tt), BF16),
        jax.ShapeDtypeStruct((B, S, B_WIDTH), F32),
        jax.ShapeDtypeStruct((B, C_HEADS, C_QK_PAD, S), BF16),
        jax.ShapeDtypeStruct((B, C_HEADS, S, C_QK_PAD), BF16),
        jax.ShapeDtypeStruct((B, C_HEADS, nt, C_V + V_PAD, Tt), BF16),
    ]
    out_specs = [
        pl.BlockSpec((1, A_KV_HEADS, HEAD_DIM, Tt * (A_HEADS // A_KV_HEADS)), headmajor_t),
        pl.BlockSpec((1, A_KV_HEADS, Tt, HEAD_DIM), headmajor),
        pl.BlockSpec((1, A_KV_HEADS, 1, HEAD_DIM + V_PAD, Tt), vt_map),
        pl.BlockSpec((1, Tt, B_WIDTH), tile),
        pl.BlockSpec((1, C_HEADS, C_QK_PAD, Tt), headmajor_t),
        pl.BlockSpec((1, C_HEADS, Tt, C_QK_PAD), headmajor),
        pl.BlockSpec((1, C_HEADS, 1, C_V + V_PAD, Tt), vt_map),
    ]
    kern = functools.partial(_premix_kernel, c_scale=(C_NOPE + C_ROPE) ** -0.5 * LOG2E)
    return pl.pallas_call(
        kern, grid=(B, nt), in_specs=in_specs, out_specs=out_specs, out_shape=out_shape,
        compiler_params=_params(("parallel", "parallel")), name="premix",
    )(h, lw["g_mix"], lw["w_in"], tabs["hsum"], lw["g_qk"], tabs["cos_a"], tabs["sin_a"],
      lw["g_cq"], lw["w_q_up"], lw["g_ckv"], lw["w_kv_up"], tabs["cos_c"], tabs["sin_c"])


def _attn_kernel(qt_ref, k_ref, vt_ref, o_ref, s_buf, p_buf, *, order, n_chunks, chunk, sub, dv, group):
    n_streams, _, width = qt_ref.shape[1:]
    rows = vt_ref.shape[3]
    nsub = chunk // sub
    n_pos = len(order)
    per_body = 1 + max(coff for _, coff in order)
    assert LOOKAHEAD < n_pos and n_chunks % per_body == 0

    def score_piece(pos, j, r):
        si = order[pos][0]
        start = pl.multiple_of(j * chunk + r * sub, sub)
        s = jnp.dot(k_ref[0, si, pl.ds(start, sub), :], qt_ref[0, si], preferred_element_type=F32)
        s_buf[pos, r * sub:(r + 1) * sub, :] = s
        return jnp.max(s, axis=0, keepdims=True)

    def prob_piece(pos, r, m_b):
        s = s_buf[pos, r * sub:(r + 1) * sub, :]
        p_buf[pos, r * sub:(r + 1) * sub, :] = jnp.exp2(s - m_b).astype(BF16)

    def all_scores(pos, j):
        cm = None
        for r in range(nsub):
            c = score_piece(pos, j, r)
            cm = c if cm is None else jnp.maximum(cm, c)
        return cm

    def body(b, carry):
        pending, state = carry
        colmax = dict(enumerate(pending))
        state = list(state)
        for i, (si, coff) in enumerate(order):
            m, acc = state[si]
            m_new = jnp.maximum(m, colmax[i])
            m_b = jnp.broadcast_to(m_new, (sub, width))
            tpos = (i + LOOKAHEAD) % n_pos
            tj = jnp.minimum((b + (i + LOOKAHEAD) // n_pos) * per_body + order[tpos][1], n_chunks - 1)
            cm = None
            for r in range(nsub):
                c = score_piece(tpos, tj, r)
                cm = c if cm is None else jnp.maximum(cm, c)
                prob_piece(i, r, m_b)
            colmax[i + LOOKAHEAD] = cm
            pv = jnp.dot(vt_ref[0, si, b * per_body + coff], p_buf[i], preferred_element_type=F32)
            state[si] = (m_new, jnp.exp2(m - m_new) * acc + pv)
        return tuple(colmax[n_pos + k] for k in range(LOOKAHEAD)), tuple(state)

    pending = tuple(all_scores(k, order[k][1]) for k in range(LOOKAHEAD))
    init = tuple((jnp.full((1, width), -jnp.inf, F32), jnp.zeros((rows, width), F32)) for _ in range(n_streams))
    _, final = lax.fori_loop(0, n_chunks // per_body, body, (pending, init))
    tq = width // group
    outs = []
    for _, acc in final:
        out_t = acc[:dv] * (1.0 / acc[dv:dv + 1])
        outs += [out_t[:, g * tq:(g + 1) * tq] for g in range(group)]
    o_ref[0] = jnp.concatenate(outs, axis=0).T


def _attention(qt, k, vt, *, tq, group, dv):
    B, Hkv, dk, _ = qt.shape
    S = k.shape[2]
    nk, rows, Tk = vt.shape[2:]
    n_streams = ATTN_STREAMS
    width = group * tq
    per_body = 2
    order = tuple((si, coff) for coff in range(per_body) for si in range(n_streams))
    kern = functools.partial(_attn_kernel, order=order, n_chunks=nk, chunk=Tk, sub=KEY_SUB, dv=dv, group=group)
    return pl.pallas_call(
        kern, grid=(B, Hkv // n_streams, S // tq),
        in_specs=[
            pl.BlockSpec((1, n_streams, dk, width), lambda b, g, i: (b, g, 0, i)),
            pl.BlockSpec((1, n_streams, S, dk), lambda b, g, i: (b, g, 0, 0)),
            pl.BlockSpec((1, n_streams, nk, rows, Tk), lambda b, g, i: (b, g, 0, 0, 0)),
        ],
        out_specs=pl.BlockSpec((1, tq, n_streams * group * dv), lambda b, g, i: (b, i, g)),
        out_shape=jax.ShapeDtypeStruct((B, S, Hkv * group * dv), F32),
        scratch_shapes=[pltpu.VMEM((len(order), Tk, width), F32), pltpu.VMEM((len(order), Tk, width), BF16)],
        compiler_params=_params(("parallel", "parallel", "arbitrary")), name="attention",
    )(qt, k, vt)


def _route(logits):
    tt = logits.shape[1]
    le = logits[:N_EXPERTS]
    lg = logits[N_EXPERTS:]
    row8 = lax.broadcasted_iota(jnp.int32, (ROUTER_ROWS - N_EXPERTS, tt), 0)
    lg = jnp.where(row8 < MOE_GROUPS, lg, -jnp.inf)
    gmax = jnp.max(lg, axis=0, keepdims=True)
    gsel = jnp.min(jnp.where(lg == gmax, row8, MOE_GROUPS), axis=0, keepdims=True)
    gprob = 1.0 / jnp.sum(jnp.exp(lg - gmax), axis=0, keepdims=True)
    row = lax.broadcasted_iota(jnp.int32, (N_EXPERTS, tt), 0)
    lm = jnp.where((row // EXPERTS_PER_GROUP) == gsel, le, -jnp.inf)
    m1 = jnp.max(lm, axis=0, keepdims=True)
    i1 = jnp.min(jnp.where(lm == m1, row, N_EXPERTS), axis=0, keepdims=True)
    lm2 = jnp.where(row == i1, -jnp.inf, lm)
    m2 = jnp.max(lm2, axis=0, keepdims=True)
    i2 = jnp.min(jnp.where(lm2 == m2, row, N_EXPERTS), axis=0, keepdims=True)
    r = jnp.exp(m2 - m1)
    w1 = gprob / (1.0 + r)
    w2 = w1 * r
    return i1, i2, w1, w2


def _dispatch_plan(i1, i2, w1, w2, before_ref, lower_ref):
    tt = i1.shape[1]
    row = lax.broadcasted_iota(jnp.int32, (N_EXPERTS, tt), 0)
    hit1 = row == i1
    hit2 = row == i2
    assign = jnp.where(hit1 | hit2, 1.0, 0.0)
    rank = jnp.dot(assign.astype(BF16), before_ref[...], preferred_element_type=F32)
    pos, offs, counts = [], [], []
    for u in range(tt // MOE_SUB):
        lanes = slice(u * MOE_SUB, (u + 1) * MOE_SUB)
        count = jnp.sum(assign[:, lanes], axis=1, keepdims=True)
        segs = jnp.floor((count + (SEG_ALIGN - 1.0)) * (1.0 / SEG_ALIGN))
        segs_b = jnp.broadcast_to(segs, (N_EXPERTS, LANES)).astype(BF16)
        off = SEG_ALIGN * jnp.dot(lower_ref[...], segs_b, preferred_element_type=F32)
        pos.append(off[:, :1] + rank[:, lanes])
        offs.append(off.astype(jnp.int32))
        counts.append(jnp.broadcast_to(count, (N_EXPERTS, LANES)).astype(jnp.int32))
    pos = jnp.concatenate(pos, axis=1)
    pos1 = jnp.sum(jnp.where(hit1, pos, 0.0), axis=0, keepdims=True)
    pos2 = jnp.sum(jnp.where(hit2, pos, 0.0), axis=0, keepdims=True)
    route = jnp.concatenate([pos1, pos2, w1, w2, jnp.zeros((4, tt), F32)], axis=0)
    return route, offs, counts


def _postmix_kernel(h_ref, oa_ref, oc_ref, ub_ref, prev_ref, next_ref, wpool_ref, spool_ref,
                    ga_ref, gb_ref, gc_ref, wout_ref, gffn_ref, wr_hi_ref, wr_lo_ref, before_ref, lower_ref,
                    h1_ref, m_ref, route_ref, offs_ref, counts_ref, ext_ref, *, seq_len):
    i = pl.program_id(1)
    nt = pl.num_programs(1)
    tt = h_ref.shape[1]
    half = B_WIDTH // 2

    u = ub_ref[0]
    ext_ref[POOL_HALO:POOL_HALO + tt, :] = u
    ext_ref[:POOL_HALO, :] = jnp.where(i > 0, prev_ref[0, 0], 0.0)
    ext_ref[POOL_HALO + tt:, :] = jnp.where(i < nt - 1, next_ref[0, 0], 0.0)

    def window(lo, hi, lanes):
        acc = None
        for d in range(lo, hi):
            piece = ext_ref[POOL_HALO + d:POOL_HALO + d + tt, lanes]
            acc = piece if acc is None else acc + piece
        return acc

    lo_lanes = slice(0, half)
    hi_lanes = slice(half, B_WIDTH)
    w2 = window(-1, 1, lo_lanes)
    w4 = w2 + window(-2, -1, lo_lanes) + window(1, 2, lo_lanes)
    w8 = window(-4, 4, hi_lanes)
    w16 = w8 + window(-8, -4, hi_lanes) + window(4, 8, hi_lanes)
    lane = lax.broadcasted_iota(jnp.int32, (tt, half), 1)
    first = lane < B_GROUP_DIM
    sums = jnp.concatenate([jnp.where(first, w2, w4), jnp.where(first, w8, w16)], axis=1)
    pos = i * tt + lax.broadcasted_iota(jnp.int32, (tt, B_WIDTH), 0)
    lane_b = lax.broadcasted_iota(jnp.int32, (tt, B_WIDTH), 1)
    hw = jnp.left_shift(1, lane_b // B_GROUP_DIM)
    cnt = jnp.minimum(pos + hw, seq_len) - jnp.maximum(pos - hw, 0)
    dlt = sums / cnt.astype(F32) - u
    ob = jnp.dot(dlt.astype(BF16), wpool_ref[...], preferred_element_type=F32) * spool_ref[...]

    merged = jnp.concatenate([
        (_rms(oa_ref[0]) * ga_ref[...]).astype(BF16),
        (_rms(ob) * gb_ref[...]).astype(BF16),
        (_rms(oc_ref[0]) * gc_ref[...]).astype(BF16)], axis=1)
    h1 = h_ref[0] + jnp.dot(merged, wout_ref[...], preferred_element_type=F32)
    h1_ref[0] = h1

    m = _rms(h1) * gffn_ref[...]
    m_hi = m.astype(BF16)
    m_ref[0] = m_hi
    m_lo = (m - m_hi.astype(F32)).astype(BF16)
    nt_dims = (((1,), (1,)), ((), ()))
    logits = (lax.dot_general(wr_hi_ref[...], m_hi, nt_dims, preferred_element_type=F32)
              + lax.dot_general(wr_lo_ref[...], m_hi, nt_dims, preferred_element_type=F32)
              + lax.dot_general(wr_hi_ref[...], m_lo, nt_dims, preferred_element_type=F32))
    route, offs, counts = _dispatch_plan(*_route(logits), before_ref, lower_ref)
    for u in range(len(offs)):
        route_ref[0, 0, u] = route[:, u * MOE_SUB:(u + 1) * MOE_SUB]
        offs_ref[0, 0, u] = offs[u]
        counts_ref[0, 0, u] = counts[u]


def _postmix(h, oa, oc, ub, lw, tabs):
    B, S, D = h.shape
    Tt = min(MOE_TILE, S)
    assert Tt % MOE_SUB == 0
    nt = S // Tt
    rows = Tt // POOL_HALO
    ub_rows = ub.reshape(B, S // POOL_HALO, POOL_HALO, B_WIDTH)
    tile = lambda b, i: (b, i, 0)
    const2 = lambda b, i: (0, 0)
    in_specs = [
        pl.BlockSpec((1, Tt, D), tile),
        pl.BlockSpec((1, Tt, A_WIDTH), tile),
        pl.BlockSpec((1, Tt, C_WIDTH), tile),
        pl.BlockSpec((1, Tt, B_WIDTH), tile),
        pl.BlockSpec((1, 1, POOL_HALO, B_WIDTH), lambda b, i: (b, jnp.maximum(i * rows - 1, 0), 0, 0)),
        pl.BlockSpec((1, 1, POOL_HALO, B_WIDTH),
                     lambda b, i: (b, jnp.minimum((i + 1) * rows, S // POOL_HALO - 1), 0, 0)),
        pl.BlockSpec((B_WIDTH, B_WIDTH), const2),
        pl.BlockSpec((1, B_WIDTH), const2),
        pl.BlockSpec((1, A_WIDTH), const2),
        pl.BlockSpec((1, B_WIDTH), const2),
        pl.BlockSpec((1, C_WIDTH), const2),
        pl.BlockSpec((D, D), const2),
        pl.BlockSpec((1, D), const2),
        pl.BlockSpec((ROUTER_ROWS, D), const2),
        pl.BlockSpec((ROUTER_ROWS, D), const2),
        pl.BlockSpec((Tt, Tt), const2),
        pl.BlockSpec((N_EXPERTS, N_EXPERTS), const2),
    ]
    per_tile = lambda b, i: (b, i, 0, 0, 0)
    n_sub = Tt // MOE_SUB
    out_shape = [
        jax.ShapeDtypeStruct((B, S, D), F32),
        jax.ShapeDtypeStruct((B, S, D), BF16),
        jax.ShapeDtypeStruct((B, nt, n_sub, 8, MOE_SUB), F32),
        jax.ShapeDtypeStruct((B, nt, n_sub, N_EXPERTS, LANES), jnp.int32),
        jax.ShapeDtypeStruct((B, nt, n_sub, N_EXPERTS, LANES), jnp.int32),
    ]
    out_specs = [
        pl.BlockSpec((1, Tt, D), tile),
        pl.BlockSpec((1, Tt, D), tile),
        pl.BlockSpec((1, 1, n_sub, 8, MOE_SUB), per_tile),
        pl.BlockSpec((1, 1, n_sub, N_EXPERTS, LANES), per_tile),
        pl.BlockSpec((1, 1, n_sub, N_EXPERTS, LANES), per_tile),
    ]
    return pl.pallas_call(
        functools.partial(_postmix_kernel, seq_len=S), grid=(B, nt),
        in_specs=in_specs, out_specs=out_specs, out_shape=out_shape,
        scratch_shapes=[pltpu.VMEM((Tt + 2 * POOL_HALO, B_WIDTH), F32)],
        compiler_params=_params(("parallel", "parallel")), name="postmix",
    )(h, oa, oc, ub, ub_rows, ub_rows, lw["w_pool"], lw["s_pool"], lw["g_out_a"], lw["g_out_b"],
      lw["g_out_c"], lw["w_out"], lw["g_ffn"], lw["w_router_hi"], lw["w_router_lo"],
      tabs["before"], tabs["lower"])


def _moe_kernel(plan_ref, m_ref, route_ref, wgu_ref, wd_ref, o_ref, perm_ref, xy_ref, ws_ref, *, n_rows):
    g = pl.program_id(0)
    e = pl.program_id(1)
    n_tiles, _, tm = route_ref.shape
    d = m_ref.shape[1]

    @pl.when(e == 0)
    def _():
        for k in range(n_tiles):
            route = route_ref[k]
            pos1 = route[0:1].astype(jnp.int32)
            pos2 = route[1:2].astype(jnp.int32)
            w1 = route[2:3]
            w2 = route[3:4]
            m = m_ref[k * tm:(k + 1) * tm, :]

            def strip(i, _, k=k, pos1=pos1, pos2=pos2, w1=w1, w2=w2, m=m):
                r0 = pl.multiple_of(i * DISPATCH_STRIP, DISPATCH_STRIP)
                rows = r0 + lax.broadcasted_iota(jnp.int32, (DISPATCH_STRIP, tm), 0)
                hit1 = rows == pos1
                hit2 = rows == pos2
                perm = jnp.where(hit1 | hit2, 1.0, 0.0).astype(BF16)
                perm_ref[k, pl.ds(r0, DISPATCH_STRIP), :] = perm
                ws_ref[k, pl.ds(r0, DISPATCH_STRIP), :] = jnp.sum(
                    jnp.where(hit1, w1, jnp.where(hit2, w2, 0.0)), axis=1, keepdims=True)
                xy_ref[k, pl.ds(r0, DISPATCH_STRIP), :] = jnp.dot(
                    perm, m, preferred_element_type=F32).astype(BF16)
                return 0

            lax.fori_loop(0, n_rows // DISPATCH_STRIP, strip, 0)
            xy_ref[k, n_rows:, :] = jnp.zeros((EXPERT_BLOCK, d), BF16)
            ws_ref[k, n_rows:, :] = jnp.zeros((EXPERT_BLOCK, 1), F32)

    def blocks(items):
        starts = [pl.multiple_of(off + bi * EXPERT_BLOCK, SEG_ALIGN) for _, off, _, bi in items]
        xs = [xy_ref[k, pl.ds(r0, EXPERT_BLOCK), :] for (k, _, _, _), r0 in zip(items, starts)]
        ws = [ws_ref[k, pl.ds(r0, EXPERT_BLOCK), :] for (k, _, _, _), r0 in zip(items, starts)]
        gu = jnp.dot(jnp.concatenate(xs, axis=0), wgu_ref[0], preferred_element_type=F32)
        gate = gu[:, :D_EXPERT]
        hid = gate * jax.nn.sigmoid(gate) * gu[:, D_EXPERT:] * jnp.concatenate(ws, axis=0)
        y = jnp.dot(hid.astype(BF16), wd_ref[0], preferred_element_type=F32).astype(BF16)
        row = lax.broadcasted_iota(jnp.int32, (EXPERT_BLOCK, 1), 0)
        for n, ((k, _, cnt, bi), r0, x) in enumerate(zip(items, starts, xs)):
            in_segment = row < cnt - bi * EXPERT_BLOCK
            xy_ref[k, pl.ds(r0, EXPERT_BLOCK), :] = jnp.where(
                in_segment, y[n * EXPERT_BLOCK:(n + 1) * EXPERT_BLOCK], x)

    plans = [(plan_ref[g * n_tiles + k, e], plan_ref[g * n_tiles + k, N_EXPERTS + e]) for k in range(n_tiles)]
    blocks([(k, off, cnt, 0) for k, (off, cnt) in enumerate(plans)])
    for k, (off, cnt) in enumerate(plans):
        n_blocks = lax.shift_right_logical(cnt + (EXPERT_BLOCK - 1), EXPERT_BLOCK.bit_length() - 1)

        def more(bi, _, k=k, off=off, cnt=cnt):
            blocks([(k, off, cnt, bi)])
            return 0

        lax.fori_loop(1, n_blocks, more, 0)

    @pl.when(e == pl.num_programs(1) - 1)
    def _():
        tn_dims = (((0,), (0,)), ((), ()))
        for k in range(n_tiles):
            y = lax.dot_general(perm_ref[k], xy_ref[k, :n_rows, :], tn_dims, preferred_element_type=F32)
            o_ref[k * tm:(k + 1) * tm, :] = y.astype(BF16)


def _moe(m, route, plan, lw):
    T, D = m.shape
    n_all, _, Tm = route.shape
    group = math.gcd(MOE_TILE_GROUP, n_all)
    n_rows = 2 * Tm + N_EXPERTS * SEG_ALIGN
    assert n_rows % DISPATCH_STRIP == 0
    tile = lambda g, e, plan: (g, 0)
    grid_spec = pltpu.PrefetchScalarGridSpec(
        num_scalar_prefetch=1, grid=(n_all // group, N_EXPERTS),
        in_specs=[
            pl.BlockSpec((group * Tm, D), tile),
            pl.BlockSpec((group, 8, Tm), lambda g, e, plan: (g, 0, 0)),
            pl.BlockSpec((1, D, 2 * D_EXPERT), lambda g, e, plan: (e, 0, 0)),
            pl.BlockSpec((1, D_EXPERT, D), lambda g, e, plan: (e, 0, 0)),
        ],
        out_specs=pl.BlockSpec((group * Tm, D), tile),
        scratch_shapes=[
            pltpu.VMEM((group, n_rows, Tm), BF16),
            pltpu.VMEM((group, n_rows + EXPERT_BLOCK, D), BF16),
            pltpu.VMEM((group, n_rows + EXPERT_BLOCK, 1), F32),
        ],
    )
    return pl.pallas_call(
        functools.partial(_moe_kernel, n_rows=n_rows), grid_spec=grid_spec,
        out_shape=jax.ShapeDtypeStruct((T, D), BF16),
        compiler_params=_params(("parallel", "arbitrary")), name="moe",
    )(plan, m, route, lw["w_gate_up"], lw["w_down"])


def _ple_kernel(h1_ref, y_ref, p_ref, gple_ref, wgate_ref, wproj_ref, gfinal_ref, o_ref, *, final):
    h = h1_ref[0] + y_ref[0].astype(F32)
    r = (_rms(h) * gple_ref[...]).astype(BF16)
    gate = jax.nn.sigmoid(jnp.dot(r, wgate_ref[...], preferred_element_type=F32))
    emb = jnp.dot(p_ref[0].astype(BF16), wproj_ref[...], preferred_element_type=F32)
    out = h + gate * emb
    if final:
        out = _rms(out) * gfinal_ref[...]
    o_ref[0] = out


def _ple(h1, y, p, lw, g_final, final):
    B, S, D = h1.shape
    Tt = TOKEN_TILE
    tile = lambda b, i: (b, i, 0)
    const2 = lambda b, i: (0, 0)
    return pl.pallas_call(
        functools.partial(_ple_kernel, final=final), grid=(B, S // Tt),
        in_specs=[
            pl.BlockSpec((1, Tt, D), tile),
            pl.BlockSpec((1, Tt, D), tile),
            pl.BlockSpec((1, Tt, PLE_DIM), tile),
            pl.BlockSpec((1, D), const2),
            pl.BlockSpec((D, D), const2),
            pl.BlockSpec((PLE_DIM, D), const2),
            pl.BlockSpec((1, D), const2),
        ],
        out_specs=pl.BlockSpec((1, Tt, D), tile),
        out_shape=jax.ShapeDtypeStruct((B, S, D), F32),
        compiler_params=_params(("parallel", "parallel")), name="ple",
    )(h1, y, p, lw["g_ple"], lw["w_ple_gate"], lw["w_ple_proj"], g_final)


def _rope_tables(seq_len):
    rows_n = seq_len // GRID_W
    row = jnp.repeat(jnp.arange(rows_n), GRID_W).astype(F32)[:, None]
    col = jnp.tile(jnp.arange(GRID_W), rows_n).astype(F32)[:, None]

    def axial(rot_dim):
        n_freq = rot_dim // 4
        inv = ROPE_THETA ** (-jnp.arange(n_freq, dtype=F32) / n_freq)
        ar, ac = row * inv, col * inv
        cos = jnp.concatenate([jnp.cos(ar), jnp.cos(ar), jnp.cos(ac), jnp.cos(ac)], axis=1)
        sin = jnp.concatenate([-jnp.sin(ar), jnp.sin(ar), -jnp.sin(ac), jnp.sin(ac)], axis=1)
        return cos, sin

    cos_a, sin_a = axial(HEAD_DIM)
    cos_r, sin_r = axial(C_ROPE)
    ones = jnp.ones((seq_len, C_NOPE), F32)
    zeros = jnp.zeros((seq_len, C_NOPE), F32)
    pad = jnp.zeros((seq_len, C_QK_PAD - C_NOPE - C_ROPE), F32)
    return dict(
        cos_a=jnp.concatenate([cos_a, cos_a], axis=1), sin_a=jnp.concatenate([sin_a, sin_a], axis=1),
        cos_c=jnp.concatenate([ones, cos_r, pad], axis=1), sin_c=jnp.concatenate([zeros, sin_r, pad], axis=1),
    )


def _pack_weights(w):
    L = DEPTH
    w_in = w["w_in"]
    def with_swapped(rope):
        parts = rope.reshape(rope.shape[:-1] + (2, 2, C_ROPE // 4))
        return jnp.concatenate([rope, parts[..., ::-1, :].reshape(rope.shape)], axis=-1)

    kr_pad = jnp.pad(with_swapped(w_in[:, :, _KR0:]), ((0, 0), (0, 0), (C_NOPE, 0)))
    q_scale = jnp.full((A_WIDTH,), HEAD_DIM ** -0.5 * LOG2E, F32)
    k_scale = jnp.ones((A_KV_HEADS * HEAD_DIM,), F32)
    g_qk = jnp.concatenate([jnp.tile(w["g_qa"], (1, A_HEADS)), jnp.tile(w["g_ka"], (1, A_KV_HEADS))], axis=1)
    g_qk = g_qk * jnp.concatenate([q_scale, k_scale])[None, :]
    w_q_up = w["w_q_up"].reshape(L, C_Q_RANK, C_HEADS, C_NOPE + C_ROPE)
    w_q_up = jnp.concatenate([w_q_up[..., :C_NOPE], with_swapped(w_q_up[..., C_NOPE:])], axis=-1)
    kv = w["w_kv_up"].reshape(L, C_KV_RANK, C_HEADS, C_NOPE + C_V)
    k_nope = jnp.pad(kv[..., :C_NOPE], ((0, 0), (0, 0), (0, 0), (0, C_QK_PAD - C_NOPE)))
    w_kv_up = jnp.concatenate([k_nope.reshape(L, C_KV_RANK, C_HEADS * C_QK_PAD),
                               kv[..., C_NOPE:].reshape(L, C_KV_RANK, C_WIDTH)], axis=2)
    eye = jnp.eye(B_GROUPS, dtype=F32)
    w_pool = (w["w_pool"][:, :, :, None, :] * eye[None, :, None, :, None]).reshape(L, B_WIDTH, B_WIDTH)
    w_router = jnp.concatenate([w["w_router_expert"], w["w_router_group"]], axis=2)
    w_router = jnp.pad(jnp.swapaxes(w_router, 1, 2), ((0, 0), (0, ROUTER_ROWS - N_EXPERTS - MOE_GROUPS), (0, 0)))
    w_router_hi = w_router.astype(BF16)
    w_router_lo = (w_router - w_router_hi.astype(F32)).astype(BF16)
    row = lambda a: a[:, None, :]
    return dict(
        g_mix=row(w["g_mix"]), w_in=jnp.concatenate([w_in[:, :, :_KR0], kr_pad], axis=2).astype(BF16),
        g_qk=row(g_qk), g_cq=row(w["g_cq"]), w_q_up=w_q_up.reshape(L, C_Q_RANK, C_HEADS * C_QK_PAD).astype(BF16),
        g_ckv=row(w["g_ckv"]), w_kv_up=w_kv_up.astype(BF16),
        w_pool=w_pool.astype(BF16), s_pool=row(w["s_pool"]),
        g_out_a=row(w["g_out_a"]), g_out_b=row(w["g_out_b"]), g_out_c=row(w["g_out_c"]),
        w_out=w["w_out"].astype(BF16), g_ffn=row(w["g_ffn"]),
        w_router_hi=w_router_hi, w_router_lo=w_router_lo,
        w_gate_up=jnp.concatenate([w["w_gate"], w["w_up"]], axis=3).astype(BF16),
        w_down=w["w_down"].astype(BF16),
        g_ple=row(w["g_ple"]), w_ple_gate=w["w_ple_gate"].astype(BF16), w_ple_proj=w["w_ple_proj"].astype(BF16),
    )


def _trunk(x, p, packed, g_final):
    B, S, D = x.shape
    tabs = _rope_tables(S)
    hs = np.kron(np.eye(_QK_A // HEAD_DIM, dtype=np.float32), np.ones((HEAD_DIM, HEAD_DIM), np.float32))
    tabs["hsum"] = jnp.asarray(hs, BF16)
    tm = min(MOE_TILE, S)
    same_sub = np.kron(np.eye(tm // MOE_SUB, dtype=np.float32), np.ones((MOE_SUB, MOE_SUB), np.float32))
    tabs["before"] = jnp.asarray(np.triu(same_sub, 1), BF16)
    tabs["lower"] = jnp.asarray(np.tril(np.ones((N_EXPERTS, N_EXPERTS), np.float32), -1), BF16)
    g_final = g_final[None, :]
    h = x
    for i in range(DEPTH):
        lw = {name: a[i] for name, a in packed.items()}
        qat, ka, vat, ub, qct, kc, vct = _premix(h, lw, tabs)
        oa = _attention(qat, ka, vat, tq=Q_TILE_GQA, group=A_HEADS // A_KV_HEADS, dv=HEAD_DIM)
        oc = _attention(qct, kc, vct, tq=Q_TILE_MLA, group=1, dv=C_V)
        h1, m, route, offs, counts = _postmix(h, oa, oc, ub, lw, tabs)
        plan = jnp.concatenate([offs[..., 0], counts[..., 0]], axis=-1).reshape(-1, 2 * N_EXPERTS)
        y = _moe(m.reshape(B * S, D), route.reshape(-1, 8, MOE_SUB), plan, lw)
        h = _ple(h1, y.reshape(B, S, D), p[i], lw, g_final, final=(i == DEPTH - 1))
    return h


def kernel(x_prompt, x_sample, p_prompt, p_sample, g_mix, w_in, g_qa, g_ka, w_pool, s_pool, g_cq, w_q_up, g_ckv, w_kv_up, g_out_a, g_out_b, g_out_c, w_out, g_ffn, w_router_group, w_router_expert, w_gate, w_up, w_down, g_ple, w_ple_gate, w_ple_proj, g_final):
    weights = dict(g_mix=g_mix, w_in=w_in, g_qa=g_qa, g_ka=g_ka, w_pool=w_pool, s_pool=s_pool, g_cq=g_cq,
                   w_q_up=w_q_up, g_ckv=g_ckv, w_kv_up=w_kv_up, g_out_a=g_out_a, g_out_b=g_out_b,
                   g_out_c=g_out_c, w_out=w_out, g_ffn=g_ffn, w_router_group=w_router_group,
                   w_router_expert=w_router_expert, w_gate=w_gate, w_up=w_up, w_down=w_down, g_ple=g_ple,
                   w_ple_gate=w_ple_gate, w_ple_proj=w_ple_proj)
    packed = _pack_weights(weights)
    return (_trunk(x_prompt, p_prompt, packed, g_final), _trunk(x_sample, p_sample, packed, g_final))
```

```python
import functools
import math

import numpy as np
import jax
import jax.numpy as jnp
from jax import lax
from jax.experimental import pallas as pl
from jax.experimental.pallas import tpu as pltpu

F32 = jnp.float32
BF16 = jnp.bfloat16

D_MODEL = 1024
DEPTH = 4
GRID_W = 64
PLE_DIM = 256
HEAD_DIM = 64
ROPE_THETA = 10000.0
EPS = 1e-6
A_HEADS = 6
A_KV_HEADS = 2
A_WIDTH = A_HEADS * HEAD_DIM
B_GROUPS = 4
B_GROUP_DIM = 64
B_WINDOWS = (2, 4, 8, 16)
B_WIDTH = B_GROUPS * B_GROUP_DIM
C_HEADS = 6
C_NOPE = 64
C_ROPE = 32
C_V = 64
C_Q_RANK = 256
C_KV_RANK = 128
C_WIDTH = C_HEADS * C_V
C_QK_PAD = 128
MOE_GROUPS = 4
EXPERTS_PER_GROUP = 8
N_EXPERTS = MOE_GROUPS * EXPERTS_PER_GROUP
D_EXPERT = 256
ROUTER_ROWS = 40
LOG2E = math.log2(math.e)

LANES = 128
VMEM_LIMIT_BYTES = 56 * 1024 * 1024

TOKEN_TILE = 512
Q_TILE_GQA = 256
Q_TILE_MLA = 512
ATTN_STREAMS = 2
KEY_SUB = 256
LOOKAHEAD = 2
MOE_TILE = 1024
MOE_SUB = 256
SEG_ALIGN = 16
EXPERT_BLOCK = 32
MOE_TILE_GROUP = 8
EXPERTS_PER_STEP = 2
DISPATCH_STRIP = 256
V_PAD = 16
POOL_HALO = 8

_QK_A = A_WIDTH + A_KV_HEADS * HEAD_DIM
_V_A0 = _QK_A
_U_B0 = _V_A0 + A_KV_HEADS * HEAD_DIM
_CQ0 = _U_B0 + B_WIDTH
_CKV0 = _CQ0 + C_Q_RANK
_KR0 = _CKV0 + C_KV_RANK
IN_COLS_PACKED = _KR0 + C_QK_PAD


def _params(sem):
    return pltpu.CompilerParams(dimension_semantics=sem, vmem_limit_bytes=VMEM_LIMIT_BYTES)


def _rms(x):
    return x * lax.rsqrt(jnp.mean(x * x, axis=-1, keepdims=True) + EPS)


def _swap_halves(x, half):
    n = x.shape[-1]
    lane = lax.broadcasted_iota(jnp.int32, x.shape, x.ndim - 1)
    return jnp.where((lane & half) == 0, pltpu.roll(x, n - half, x.ndim - 1), pltpu.roll(x, half, x.ndim - 1))


def _rope_partner(x):
    return pltpu.roll(x, x.shape[-1] - C_ROPE, x.ndim - 1)


def _premix_kernel(h_ref, g_mix_ref, w_in_ref, hsum_ref, g_qk_ref, cosa_ref, sina_ref,
                   g_cq_ref, w_qup_ref, g_ckv_ref, w_kvup_ref, cosc_ref, sinc_ref,
                   qat_ref, ka_ref, vat_ref, ub_ref, qct_ref, kc_ref, vct_ref, *, c_scale):
    a = _rms(h_ref[0]) * g_mix_ref[...]
    z = jnp.dot(a.astype(BF16), w_in_ref[...], preferred_element_type=F32)

    qk = z[:, :_QK_A]
    ss = jnp.dot((qk * qk).astype(BF16), hsum_ref[...], preferred_element_type=F32)
    qk = qk * lax.rsqrt(ss * (1.0 / HEAD_DIM) + EPS) * g_qk_ref[...]
    reps = _QK_A // LANES
    cos_a = jnp.concatenate([cosa_ref[...]] * reps, axis=1)
    sin_a = jnp.concatenate([sina_ref[...]] * reps, axis=1)
    qk = qk * cos_a + _swap_halves(qk, HEAD_DIM // 4) * sin_a
    qat = qk[:, :A_WIDTH].T.astype(BF16)
    tt = qat.shape[1]
    a_group = A_HEADS // A_KV_HEADS
    for hh in range(A_HEADS):
        for qi in range(tt // Q_TILE_GQA):
            col = (qi * a_group + hh % a_group) * Q_TILE_GQA
            qat_ref[0, hh // a_group, :, col:col + Q_TILE_GQA] = qat[
                hh * HEAD_DIM:(hh + 1) * HEAD_DIM, qi * Q_TILE_GQA:(qi + 1) * Q_TILE_GQA]
    ka = qk[:, A_WIDTH:].astype(BF16)
    for hh in range(A_KV_HEADS):
        ka_ref[0, hh] = ka[:, hh * HEAD_DIM:(hh + 1) * HEAD_DIM]
    ones_rows = (lax.broadcasted_iota(jnp.int32, (V_PAD, tt), 0) == 0).astype(BF16)
    vt = z[:, _V_A0:_U_B0].T.astype(BF16)
    for hh in range(A_KV_HEADS):
        vat_ref[0, hh, 0] = jnp.concatenate([vt[hh * HEAD_DIM:(hh + 1) * HEAD_DIM], ones_rows], axis=0)

    ub_ref[0] = z[:, _U_B0:_CQ0]

    cqn = _rms(z[:, _CQ0:_CKV0]) * g_cq_ref[...]
    qc = jnp.dot(cqn.astype(BF16), w_qup_ref[...], preferred_element_type=F32)
    cos_c = cosc_ref[...]
    sin_c = sinc_ref[...]
    cos_q = jnp.concatenate([cos_c * c_scale] * C_HEADS, axis=1)
    sin_q = jnp.concatenate([sin_c * c_scale] * C_HEADS, axis=1)
    qct = (qc * cos_q + _rope_partner(qc) * sin_q).T.astype(BF16)
    ckvn = _rms(z[:, _CKV0:_KR0]) * g_ckv_ref[...]
    kv = jnp.dot(ckvn.astype(BF16), w_kvup_ref[...], preferred_element_type=F32)
    kr = z[:, _KR0:IN_COLS_PACKED]
    kr = kr * cos_c + _rope_partner(kr) * sin_c
    for hh in range(C_HEADS):
        qct_ref[0, hh] = qct[hh * C_QK_PAD:(hh + 1) * C_QK_PAD]
        kc_ref[0, hh] = (kv[:, hh * C_QK_PAD:(hh + 1) * C_QK_PAD] + kr).astype(BF16)
    vct = kv[:, C_HEADS * C_QK_PAD:].T.astype(BF16)
    for hh in range(C_HEADS):
        vct_ref[0, hh, 0] = jnp.concatenate([vct[hh * C_V:(hh + 1) * C_V], ones_rows], axis=0)


def _premix(h, lw, tabs):
    B, S, D = h.shape
    Tt = TOKEN_TILE
    nt = S // Tt
    tile = lambda b, i: (b, i, 0)
    const2 = lambda b, i: (0, 0)
    headmajor = lambda b, i: (b, 0, i, 0)
    headmajor_t = lambda b, i: (b, 0, 0, i)
    vt_map = lambda b, i: (b, 0, i, 0, 0)
    tab_map = lambda b, i: (i, 0)
    in_specs = [
        pl.BlockSpec((1, Tt, D), tile),
        pl.BlockSpec((1, D), const2),
        pl.BlockSpec((D, IN_COLS_PACKED), const2),
        pl.BlockSpec((_QK_A, _QK_A), const2),
        pl.BlockSpec((1, _QK_A), const2),
        pl.BlockSpec((Tt, LANES), tab_map),
        pl.BlockSpec((Tt, LANES), tab_map),
        pl.BlockSpec((1, C_Q_RANK), const2),
        pl.BlockSpec((C_Q_RANK, C_HEADS * C_QK_PAD), const2),
        pl.BlockSpec((1, C_KV_RANK), const2),
        pl.BlockSpec((C_KV_RANK, C_HEADS * C_QK_PAD + C_WIDTH), const2),
        pl.BlockSpec((Tt, LANES), tab_map),
        pl.BlockSpec((Tt, LANES), tab_map),
    ]
    out_shape = [
        jax.ShapeDtypeStruct((B, A_KV_HEADS, HEAD_DIM, S * (A_HEADS // A_KV_HEADS)), BF16),
        jax.ShapeDtypeStruct((B, A_KV_HEADS, S, HEAD_DIM), BF16),
        jax.ShapeDtypeStruct((B, A_KV_HEADS, nt, HEAD_DIM + V_PAD, Tt), BF16),
        jax.ShapeDtypeStruct((B, S, B_WIDTH), F32),
        jax.ShapeDtypeStruct((B, C_HEADS, C_QK_PAD, S), BF16),
        jax.ShapeDtypeStruct((B, C_HEADS, S, C_QK_PAD), BF16),
        jax.ShapeDtypeStruct((B, C_HEADS, nt, C_V + V_PAD, Tt), BF16),
    ]
    out_specs = [
        pl.BlockSpec((1, A_KV_HEADS, HEAD_DIM, Tt * (A_HEADS // A_KV_HEADS)), headmajor_t),
        pl.BlockSpec((1, A_KV_HEADS, Tt, HEAD_DIM), headmajor),
        pl.BlockSpec((1, A_KV_HEADS, 1, HEAD_DIM + V_PAD, Tt), vt_map),
        pl.BlockSpec((1, Tt, B_WIDTH), tile),
        pl.BlockSpec((1, C_HEADS, C_QK_PAD, Tt), headmajor_t),
        pl.BlockSpec((1, C_HEADS, Tt, C_QK_PAD), headmajor),
        pl.BlockSpec((1, C_HEADS, 1, C_V + V_PAD, Tt), vt_map),
    ]
    kern = functools.partial(_premix_kernel, c_scale=(C_NOPE + C_ROPE) ** -0.5 * LOG2E)
    return pl.pallas_call(
        kern, grid=(B, nt), in_specs=in_specs, out_specs=out_specs, out_shape=out_shape,
        compiler_params=_params(("parallel", "parallel")), name="premix",
    )(h, lw["g_mix"], lw["w_in"], tabs["hsum"], lw["g_qk"], tabs["cos_a"], tabs["sin_a"],
      lw["g_cq"], lw["w_q_up"], lw["g_ckv"], lw["w_kv_up"], tabs["cos_c"], tabs["sin_c"])


def _attn_kernel(qt_ref, k_ref, vt_ref, o_ref, s_buf, p_buf, *, order, n_chunks, chunk, sub, dv, group):
    n_streams, _, width = qt_ref.shape[1:]
    rows = vt_ref.shape[3]
    nsub = chunk // sub
    n_pos = len(order)
    per_body = 1 + max(coff for _, coff in order)
    assert LOOKAHEAD < n_pos and n_chunks % per_body == 0

    def score_piece(pos, j, r):
        si = order[pos][0]
        start = pl.multiple_of(j * chunk + r * sub, sub)
        s = jnp.dot(k_ref[0, si, pl.ds(start, sub), :], qt_ref[0, si], preferred_element_type=F32)
        s_buf[pos, r * sub:(r + 1) * sub, :] = s
        return jnp.max(s, axis=0, keepdims=True)

    def prob_piece(pos, r, m_b):
        s = s_buf[pos, r * sub:(r + 1) * sub, :]
        p_buf[pos, r * sub:(r + 1) * sub, :] = jnp.exp2(s - m_b).astype(BF16)

    def all_scores(pos, j):
        cm = None
        for r in range(nsub):
            c = score_piece(pos, j, r)
            cm = c if cm is None else jnp.maximum(cm, c)
        return cm

    def body(b, carry):
        pending, state = carry
        colmax = dict(enumerate(pending))
        state = list(state)
        for i, (si, coff) in enumerate(order):
            m, acc = state[si]
            m_new = jnp.maximum(m, colmax[i])
            m_b = jnp.broadcast_to(m_new, (sub, width))
            tpos = (i + LOOKAHEAD) % n_pos
            tj = jnp.minimum((b + (i + LOOKAHEAD) // n_pos) * per_body + order[tpos][1], n_chunks - 1)
            cm = None
            for r in range(nsub):
                c = score_piece(tpos, tj, r)
                cm = c if cm is None else jnp.maximum(cm, c)
                prob_piece(i, r, m_b)
            colmax[i + LOOKAHEAD] = cm
            pv = jnp.dot(vt_ref[0, si, b * per_body + coff], p_buf[i], preferred_element_type=F32)
            state[si] = (m_new, jnp.exp2(m - m_new) * acc + pv)
        return tuple(colmax[n_pos + k] for k in range(LOOKAHEAD)), tuple(state)

    pending = tuple(all_scores(k, order[k][1]) for k in range(LOOKAHEAD))
    init = tuple((jnp.full((1, width), -jnp.inf, F32), jnp.zeros((rows, width), F32)) for _ in range(n_streams))
    _, final = lax.fori_loop(0, n_chunks // per_body, body, (pending, init))
    tq = width // group
    outs = []
    for _, acc in final:
        out_t = acc[:dv] * (1.0 / acc[dv:dv + 1])
        outs += [out_t[:, g * tq:(g + 1) * tq] for g in range(group)]
    o_ref[0] = jnp.concatenate(outs, axis=0).T


def _attention(qt, k, vt, *, tq, group, dv):
    B, Hkv, dk, _ = qt.shape
    S = k.shape[2]
    nk, rows, Tk = vt.shape[2:]
    n_streams = ATTN_STREAMS
    width = group * tq
    per_body = 2
    order = tuple((si, coff) for coff in range(per_body) for si in range(n_streams))
    kern = functools.partial(_attn_kernel, order=order, n_chunks=nk, chunk=Tk, sub=KEY_SUB, dv=dv, group=group)
    return pl.pallas_call(
        kern, grid=(B, Hkv // n_streams, S // tq),
        in_specs=[
            pl.BlockSpec((1, n_streams, dk, width), lambda b, g, i: (b, g, 0, i)),
            pl.BlockSpec((1, n_streams, S, dk), lambda b, g, i: (b, g, 0, 0)),
            pl.BlockSpec((1, n_streams, nk, rows, Tk), lambda b, g, i: (b, g, 0, 0, 0)),
        ],
        out_specs=pl.BlockSpec((1, tq, n_streams * group * dv), lambda b, g, i: (b, i, g)),
        out_shape=jax.ShapeDtypeStruct((B, S, Hkv * group * dv), F32),
        scratch_shapes=[pltpu.VMEM((len(order), Tk, width), F32), pltpu.VMEM((len(order), Tk, width), BF16)],
        compiler_params=_params(("parallel", "parallel", "arbitrary")), name="attention",
    )(qt, k, vt)


def _route(logits):
    tt = logits.shape[1]
    le = logits[:N_EXPERTS]
    lg = logits[N_EXPERTS:]
    row8 = lax.broadcasted_iota(jnp.int32, (ROUTER_ROWS - N_EXPERTS, tt), 0)
    lg = jnp.where(row8 < MOE_GROUPS, lg, -jnp.inf)
    gmax = jnp.max(lg, axis=0, keepdims=True)
    gsel = jnp.min(jnp.where(lg == gmax, row8, MOE_GROUPS), axis=0, keepdims=True)
    gprob = 1.0 / jnp.sum(jnp.exp(lg - gmax), axis=0, keepdims=True)
    row = lax.broadcasted_iota(jnp.int32, (N_EXPERTS, tt), 0)
    lm = jnp.where((row // EXPERTS_PER_GROUP) == gsel, le, -jnp.inf)
    m1 = jnp.max(lm, axis=0, keepdims=True)
    i1 = jnp.min(jnp.where(lm == m1, row, N_EXPERTS), axis=0, keepdims=True)
    lm2 = jnp.where(row == i1, -jnp.inf, lm)
    m2 = jnp.max(lm2, axis=0, keepdims=True)
    i2 = jnp.min(jnp.where(lm2 == m2, row, N_EXPERTS), axis=0, keepdims=True)
    r = jnp.exp(m2 - m1)
    w1 = gprob / (1.0 + r)
    w2 = w1 * r
    return i1, i2, w1, w2


def _dispatch_plan(i1, i2, w1, w2, before_ref, lower_ref):
    tt = i1.shape[1]
    row = lax.broadcasted_iota(jnp.int32, (N_EXPERTS, tt), 0)
    hit1 = row == i1
    hit2 = row == i2
    assign = jnp.where(hit1 | hit2, 1.0, 0.0)
    rank = jnp.dot(assign.astype(BF16), before_ref[...], preferred_element_type=F32)
    pos, offs, counts = [], [], []
    for u in range(tt // MOE_SUB):
        lanes = slice(u * MOE_SUB, (u + 1) * MOE_SUB)
        count = jnp.sum(assign[:, lanes], axis=1, keepdims=True)
        segs = jnp.floor((count + (SEG_ALIGN - 1.0)) * (1.0 / SEG_ALIGN))
        segs_b = jnp.broadcast_to(segs, (N_EXPERTS, LANES)).astype(BF16)
        off = SEG_ALIGN * jnp.dot(lower_ref[...], segs_b, preferred_element_type=F32)
        pos.append(off[:, :1] + rank[:, lanes])
        offs.append(off.astype(jnp.int32))
        counts.append(jnp.broadcast_to(count, (N_EXPERTS, LANES)).astype(jnp.int32))
    pos = jnp.concatenate(pos, axis=1)
    pos1 = jnp.sum(jnp.where(hit1, pos, 0.0), axis=0, keepdims=True)
    pos2 = jnp.sum(jnp.where(hit2, pos, 0.0), axis=0, keepdims=True)
    route = jnp.concatenate([pos1, pos2, w1, w2, jnp.zeros((4, tt), F32)], axis=0)
    return route, offs, counts


def _postmix_kernel(h_ref, oa_ref, oc_ref, ub_ref, prev_ref, next_ref, wpool_ref, spool_ref,
                    ga_ref, gb_ref, gc_ref, wout_ref, gffn_ref, wr_hi_ref, wr_lo_ref, before_ref, lower_ref,
                    h1_ref, m_ref, route_ref, offs_ref, counts_ref, ext_ref, *, seq_len):
    i = pl.program_id(1)
    nt = pl.num_programs(1)
    tt = h_ref.shape[1]
    half = B_WIDTH // 2

    u = ub_ref[0]
    ext_ref[POOL_HALO:POOL_HALO + tt, :] = u
    ext_ref[:POOL_HALO, :] = jnp.where(i > 0, prev_ref[0, 0], 0.0)
    ext_ref[POOL_HALO + tt:, :] = jnp.where(i < nt - 1, next_ref[0, 0], 0.0)

    def window(lo, hi, lanes):
        acc = None
        for d in range(lo, hi):
            piece = ext_ref[POOL_HALO + d:POOL_HALO + d + tt, lanes]
            acc = piece if acc is None else acc + piece
        return acc

    lo_lanes = slice(0, half)
    hi_lanes = slice(half, B_WIDTH)
    w2 = window(-1, 1, lo_lanes)
    w4 = w2 + window(-2, -1, lo_lanes) + window(1, 2, lo_lanes)
    w8 = window(-4, 4, hi_lanes)
    w16 = w8 + window(-8, -4, hi_lanes) + window(4, 8, hi_lanes)
    lane = lax.broadcasted_iota(jnp.int32, (tt, half), 1)
    first = lane < B_GROUP_DIM
    sums = jnp.concatenate([jnp.where(first, w2, w4), jnp.where(first, w8, w16)], axis=1)
    pos = i * tt + lax.broadcasted_iota(jnp.int32, (tt, B_WIDTH), 0)
    lane_b = lax.broadcasted_iota(jnp.int32, (tt, B_WIDTH), 1)
    hw = jnp.left_shift(1, lane_b // B_GROUP_DIM)
    cnt = jnp.minimum(pos + hw, seq_len) - jnp.maximum(pos - hw, 0)
    dlt = sums / cnt.astype(F32) - u
    ob = jnp.dot(dlt.astype(BF16), wpool_ref[...], preferred_element_type=F32) * spool_ref[...]

    merged = jnp.concatenate([
        (_rms(oa_ref[0]) * ga_ref[...]).astype(BF16),
        (_rms(ob) * gb_ref[...]).astype(BF16),
        (_rms(oc_ref[0]) * gc_ref[...]).astype(BF16)], axis=1)
    h1 = h_ref[0] + jnp.dot(merged, wout_ref[...], preferred_element_type=F32)
    h1_ref[0] = h1

    m = _rms(h1) * gffn_ref[...]
    m_hi = m.astype(BF16)
    m_ref[0] = m_hi
    m_lo = (m - m_hi.astype(F32)).astype(BF16)
    nt_dims = (((1,), (1,)), ((), ()))
    logits = (lax.dot_general(wr_hi_ref[...], m_hi, nt_dims, preferred_element_type=F32)
              + lax.dot_general(wr_lo_ref[...], m_hi, nt_dims, preferred_element_type=F32)
              + lax.dot_general(wr_hi_ref[...], m_lo, nt_dims, preferred_element_type=F32))
    route, offs, counts = _dispatch_plan(*_route(logits), before_ref, lower_ref)
    for u in range(len(offs)):
        route_ref[0, 0, u] = route[:, u * MOE_SUB:(u + 1) * MOE_SUB]
        offs_ref[0, 0, u] = offs[u]
        counts_ref[0, 0, u] = counts[u]


def _postmix(h, oa, oc, ub, lw, tabs):
    B, S, D = h.shape
    Tt = min(MOE_TILE, S)
    assert Tt % MOE_SUB == 0
    nt = S // Tt
    rows = Tt // POOL_HALO
    ub_rows = ub.reshape(B, S // POOL_HALO, POOL_HALO, B_WIDTH)
    tile = lambda b, i: (b, i, 0)
    const2 = lambda b, i: (0, 0)
    in_specs = [
        pl.BlockSpec((1, Tt, D), tile),
        pl.BlockSpec((1, Tt, A_WIDTH), tile),
        pl.BlockSpec((1, Tt, C_WIDTH), tile),
        pl.BlockSpec((1, Tt, B_WIDTH), tile),
        pl.BlockSpec((1, 1, POOL_HALO, B_WIDTH), lambda b, i: (b, jnp.maximum(i * rows - 1, 0), 0, 0)),
        pl.BlockSpec((1, 1, POOL_HALO, B_WIDTH),
                     lambda b, i: (b, jnp.minimum((i + 1) * rows, S // POOL_HALO - 1), 0, 0)),
        pl.BlockSpec((B_WIDTH, B_WIDTH), const2),
        pl.BlockSpec((1, B_WIDTH), const2),
        pl.BlockSpec((1, A_WIDTH), const2),
        pl.BlockSpec((1, B_WIDTH), const2),
        pl.BlockSpec((1, C_WIDTH), const2),
        pl.BlockSpec((D, D), const2),
        pl.BlockSpec((1, D), const2),
        pl.BlockSpec((ROUTER_ROWS, D), const2),
        pl.BlockSpec((ROUTER_ROWS, D), const2),
        pl.BlockSpec((Tt, Tt), const2),
        pl.BlockSpec((N_EXPERTS, N_EXPERTS), const2),
    ]
    per_tile = lambda b, i: (b, i, 0, 0, 0)
    n_sub = Tt // MOE_SUB
    out_shape = [
        jax.ShapeDtypeStruct((B, S, D), F32),
        jax.ShapeDtypeStruct((B, S, D), BF16),
        jax.ShapeDtypeStruct((B, nt, n_sub, 8, MOE_SUB), F32),
        jax.ShapeDtypeStruct((B, nt, n_sub, N_EXPERTS, LANES), jnp.int32),
        jax.ShapeDtypeStruct((B, nt, n_sub, N_EXPERTS, LANES), jnp.int32),
    ]
    out_specs = [
        pl.BlockSpec((1, Tt, D), tile),
        pl.BlockSpec((1, Tt, D), tile),
        pl.BlockSpec((1, 1, n_sub, 8, MOE_SUB), per_tile),
        pl.BlockSpec((1, 1, n_sub, N_EXPERTS, LANES), per_tile),
        pl.BlockSpec((1, 1, n_sub, N_EXPERTS, LANES), per_tile),
    ]
    return pl.pallas_call(
        functools.partial(_postmix_kernel, seq_len=S), grid=(B, nt),
        in_specs=in_specs, out_specs=out_specs, out_shape=out_shape,
        scratch_shapes=[pltpu.VMEM((Tt + 2 * POOL_HALO, B_WIDTH), F32)],
        compiler_params=_params(("parallel", "parallel")), name="postmix",
    )(h, oa, oc, ub, ub_rows, ub_rows, lw["w_pool"], lw["s_pool"], lw["g_out_a"], lw["g_out_b"],
      lw["g_out_c"], lw["w_out"], lw["g_ffn"], lw["w_router_hi"], lw["w_router_lo"],
      tabs["before"], tabs["lower"])


def _moe_kernel(plan_ref, m_ref, route_ref, wgu_ref, wd_ref, o_ref, perm_ref, xy_ref, ws_ref, *, n_rows):
    g = pl.program_id(0)
    e = pl.program_id(1)
    n_tiles, _, tm = route_ref.shape
    d = m_ref.shape[1]

    @pl.when(e == 0)
    def _():
        for k in range(n_tiles):
            route = route_ref[k]
            pos1 = route[0:1].astype(jnp.int32)
            pos2 = route[1:2].astype(jnp.int32)
            w1 = route[2:3]
            w2 = route[3:4]
            m = m_ref[k * tm:(k + 1) * tm, :]

            def strip(i, _, k=k, pos1=pos1, pos2=pos2, w1=w1, w2=w2, m=m):
                r0 = pl.multiple_of(i * DISPATCH_STRIP, DISPATCH_STRIP)
                rows = r0 + lax.broadcasted_iota(jnp.int32, (DISPATCH_STRIP, tm), 0)
                hit1 = rows == pos1
                hit2 = rows == pos2
                perm = jnp.where(hit1 | hit2, 1.0, 0.0).astype(BF16)
                perm_ref[k, pl.ds(r0, DISPATCH_STRIP), :] = perm
                ws_ref[k, pl.ds(r0, DISPATCH_STRIP), :] = jnp.sum(
                    jnp.where(hit1, w1, jnp.where(hit2, w2, 0.0)), axis=1, keepdims=True)
                xy_ref[k, pl.ds(r0, DISPATCH_STRIP), :] = jnp.dot(
                    perm, m, preferred_element_type=F32).astype(BF16)
                return 0

            lax.fori_loop(0, n_rows // DISPATCH_STRIP, strip, 0)
            xy_ref[k, n_rows:, :] = jnp.zeros((EXPERT_BLOCK, d), BF16)
            ws_ref[k, n_rows:, :] = jnp.zeros((EXPERT_BLOCK, 1), F32)

    def blocks(j, plans, bi):
        starts = [pl.multiple_of(jnp.minimum(off + bi * EXPERT_BLOCK, n_rows), SEG_ALIGN) for off, _ in plans]
        xs = [xy_ref[k, pl.ds(r0, EXPERT_BLOCK), :] for k, r0 in enumerate(starts)]
        ws = [ws_ref[k, pl.ds(r0, EXPERT_BLOCK), :] for k, r0 in enumerate(starts)]
        gu = jnp.dot(jnp.concatenate(xs, axis=0), wgu_ref[j], preferred_element_type=F32)
        gate = gu[:, :D_EXPERT]
        hid = gate * jax.nn.sigmoid(gate) * gu[:, D_EXPERT:] * jnp.concatenate(ws, axis=0)
        y = jnp.dot(hid.astype(BF16), wd_ref[j], preferred_element_type=F32).astype(BF16)
        row = lax.broadcasted_iota(jnp.int32, (EXPERT_BLOCK, 1), 0)
        for k, ((_, cnt), r0, x) in enumerate(zip(plans, starts, xs)):
            in_segment = row < cnt - bi * EXPERT_BLOCK
            xy_ref[k, pl.ds(r0, EXPERT_BLOCK), :] = jnp.where(
                in_segment, y[k * EXPERT_BLOCK:(k + 1) * EXPERT_BLOCK], x)

    for j in range(wgu_ref.shape[0]):
        expert = e * wgu_ref.shape[0] + j
        plans = [(plan_ref[g * n_tiles + k, expert], plan_ref[g * n_tiles + k, N_EXPERTS + expert])
                 for k in range(n_tiles)]
        blocks(j, plans, 0)
        longest = functools.reduce(jnp.maximum, [cnt for _, cnt in plans])
        n_blocks = lax.shift_right_logical(longest + (EXPERT_BLOCK - 1), EXPERT_BLOCK.bit_length() - 1)

        def more(bi, _, j=j, plans=plans):
            blocks(j, plans, bi)
            return 0

        lax.fori_loop(1, n_blocks, more, 0)

    @pl.when(e == pl.num_programs(1) - 1)
    def _():
        tn_dims = (((0,), (0,)), ((), ()))
        for k in range(n_tiles):
            y = lax.dot_general(perm_ref[k], xy_ref[k, :n_rows, :], tn_dims, preferred_element_type=F32)
            o_ref[k * tm:(k + 1) * tm, :] = y.astype(BF16)


def _moe(m, route, plan, lw):
    T, D = m.shape
    n_all, _, Tm = route.shape
    group = math.gcd(MOE_TILE_GROUP, n_all)
    n_rows = 2 * Tm + N_EXPERTS * SEG_ALIGN
    assert n_rows % DISPATCH_STRIP == 0
    tile = lambda g, e, plan: (g, 0)
    grid_spec = pltpu.PrefetchScalarGridSpec(
        num_scalar_prefetch=1, grid=(n_all // group, N_EXPERTS // EXPERTS_PER_STEP),
        in_specs=[
            pl.BlockSpec((group * Tm, D), tile),
            pl.BlockSpec((group, 8, Tm), lambda g, e, plan: (g, 0, 0)),
            pl.BlockSpec((EXPERTS_PER_STEP, D, 2 * D_EXPERT), lambda g, e, plan: (e, 0, 0)),
            pl.BlockSpec((EXPERTS_PER_STEP, D_EXPERT, D), lambda g, e, plan: (e, 0, 0)),
        ],
        out_specs=pl.BlockSpec((group * Tm, D), tile),
        scratch_shapes=[
            pltpu.VMEM((group, n_rows, Tm), BF16),
            pltpu.VMEM((group, n_rows + EXPERT_BLOCK, D), BF16),
            pltpu.VMEM((group, n_rows + EXPERT_BLOCK, 1), F32),
        ],
    )
    return pl.pallas_call(
        functools.partial(_moe_kernel, n_rows=n_rows), grid_spec=grid_spec,
        out_shape=jax.ShapeDtypeStruct((T, D), BF16),
        compiler_params=_params(("parallel", "arbitrary")), name="moe",
    )(plan, m, route, lw["w_gate_up"], lw["w_down"])


def _ple_kernel(h1_ref, y_ref, p_ref, gple_ref, wgate_ref, wproj_ref, gfinal_ref, o_ref, *, final):
    h = h1_ref[0] + y_ref[0].astype(F32)
    r = (_rms(h) * gple_ref[...]).astype(BF16)
    gate = jax.nn.sigmoid(jnp.dot(r, wgate_ref[...], preferred_element_type=F32))
    emb = jnp.dot(p_ref[0, 0].astype(BF16), wproj_ref[...], preferred_element_type=F32)
    out = h + gate * emb
    if final:
        out = _rms(out) * gfinal_ref[...]
    o_ref[0] = out


def _ple(h1, y, p, layer, lw, g_final, final):
    B, S, D = h1.shape
    Tt = TOKEN_TILE
    tile = lambda b, i: (b, i, 0)
    const2 = lambda b, i: (0, 0)
    return pl.pallas_call(
        functools.partial(_ple_kernel, final=final), grid=(B, S // Tt),
        in_specs=[
            pl.BlockSpec((1, Tt, D), tile),
            pl.BlockSpec((1, Tt, D), tile),
            pl.BlockSpec((1, 1, Tt, PLE_DIM), lambda b, i: (layer, b, i, 0)),
            pl.BlockSpec((1, D), const2),
            pl.BlockSpec((D, D), const2),
            pl.BlockSpec((PLE_DIM, D), const2),
            pl.BlockSpec((1, D), const2),
        ],
        out_specs=pl.BlockSpec((1, Tt, D), tile),
        out_shape=jax.ShapeDtypeStruct((B, S, D), F32),
        compiler_params=_params(("parallel", "parallel")), name="ple",
    )(h1, y, p, lw["g_ple"], lw["w_ple_gate"], lw["w_ple_proj"], g_final)


def _rope_tables(seq_len):
    rows_n = seq_len // GRID_W
    row = jnp.repeat(jnp.arange(rows_n), GRID_W).astype(F32)[:, None]
    col = jnp.tile(jnp.arange(GRID_W), rows_n).astype(F32)[:, None]

    def axial(rot_dim):
        n_freq = rot_dim // 4
        inv = ROPE_THETA ** (-jnp.arange(n_freq, dtype=F32) / n_freq)
        ar, ac = row * inv, col * inv
        cos = jnp.concatenate([jnp.cos(ar), jnp.cos(ar), jnp.cos(ac), jnp.cos(ac)], axis=1)
        sin = jnp.concatenate([-jnp.sin(ar), jnp.sin(ar), -jnp.sin(ac), jnp.sin(ac)], axis=1)
        return cos, sin

    cos_a, sin_a = axial(HEAD_DIM)
    cos_r, sin_r = axial(C_ROPE)
    ones = jnp.ones((seq_len, C_NOPE), F32)
    zeros = jnp.zeros((seq_len, C_NOPE), F32)
    pad = jnp.zeros((seq_len, C_QK_PAD - C_NOPE - C_ROPE), F32)
    return dict(
        cos_a=jnp.concatenate([cos_a, cos_a], axis=1), sin_a=jnp.concatenate([sin_a, sin_a], axis=1),
        cos_c=jnp.concatenate([ones, cos_r, pad], axis=1), sin_c=jnp.concatenate([zeros, sin_r, pad], axis=1),
    )


def _pack_weights(w):
    L = DEPTH
    w_in = w["w_in"]
    def with_swapped(rope):
        parts = rope.reshape(rope.shape[:-1] + (2, 2, C_ROPE // 4))
        return jnp.concatenate([rope, parts[..., ::-1, :].reshape(rope.shape)], axis=-1)

    kr_pad = jnp.pad(with_swapped(w_in[:, :, _KR0:]), ((0, 0), (0, 0), (C_NOPE, 0)))
    q_scale = jnp.full((A_WIDTH,), HEAD_DIM ** -0.5 * LOG2E, F32)
    k_scale = jnp.ones((A_KV_HEADS * HEAD_DIM,), F32)
    g_qk = jnp.concatenate([jnp.tile(w["g_qa"], (1, A_HEADS)), jnp.tile(w["g_ka"], (1, A_KV_HEADS))], axis=1)
    g_qk = g_qk * jnp.concatenate([q_scale, k_scale])[None, :]
    w_q_up = w["w_q_up"].reshape(L, C_Q_RANK, C_HEADS, C_NOPE + C_ROPE)
    w_q_up = jnp.concatenate([w_q_up[..., :C_NOPE], with_swapped(w_q_up[..., C_NOPE:])], axis=-1)
    kv = w["w_kv_up"].reshape(L, C_KV_RANK, C_HEADS, C_NOPE + C_V)
    k_nope = jnp.pad(kv[..., :C_NOPE], ((0, 0), (0, 0), (0, 0), (0, C_QK_PAD - C_NOPE)))
    w_kv_up = jnp.concatenate([k_nope.reshape(L, C_KV_RANK, C_HEADS * C_QK_PAD),
                               kv[..., C_NOPE:].reshape(L, C_KV_RANK, C_WIDTH)], axis=2)
    eye = jnp.eye(B_GROUPS, dtype=F32)
    w_pool = (w["w_pool"][:, :, :, None, :] * eye[None, :, None, :, None]).reshape(L, B_WIDTH, B_WIDTH)
    w_router = jnp.concatenate([w["w_router_expert"], w["w_router_group"]], axis=2)
    w_router = jnp.pad(jnp.swapaxes(w_router, 1, 2), ((0, 0), (0, ROUTER_ROWS - N_EXPERTS - MOE_GROUPS), (0, 0)))
    w_router_hi = w_router.astype(BF16)
    w_router_lo = (w_router - w_router_hi.astype(F32)).astype(BF16)
    row = lambda a: a[:, None, :]
    return dict(
        g_mix=row(w["g_mix"]), w_in=jnp.concatenate([w_in[:, :, :_KR0], kr_pad], axis=2).astype(BF16),
        g_qk=row(g_qk), g_cq=row(w["g_cq"]), w_q_up=w_q_up.reshape(L, C_Q_RANK, C_HEADS * C_QK_PAD).astype(BF16),
        g_ckv=row(w["g_ckv"]), w_kv_up=w_kv_up.astype(BF16),
        w_pool=w_pool.astype(BF16), s_pool=row(w["s_pool"]),
        g_out_a=row(w["g_out_a"]), g_out_b=row(w["g_out_b"]), g_out_c=row(w["g_out_c"]),
        w_out=w["w_out"].astype(BF16), g_ffn=row(w["g_ffn"]),
        w_router_hi=w_router_hi, w_router_lo=w_router_lo,
        w_gate_up=jnp.concatenate([w["w_gate"], w["w_up"]], axis=3).astype(BF16),
        w_down=w["w_down"].astype(BF16),
        g_ple=row(w["g_ple"]), w_ple_gate=w["w_ple_gate"].astype(BF16), w_ple_proj=w["w_ple_proj"].astype(BF16),
    )


def _trunk(x, p, packed, g_final):
    B, S, D = x.shape
    tabs = _rope_tables(S)
    hs = np.kron(np.eye(_QK_A // HEAD_DIM, dtype=np.float32), np.ones((HEAD_DIM, HEAD_DIM), np.float32))
    tabs["hsum"] = jnp.asarray(hs, BF16)
    tm = min(MOE_TILE, S)
    same_sub = np.kron(np.eye(tm // MOE_SUB, dtype=np.float32), np.ones((MOE_SUB, MOE_SUB), np.float32))
    tabs["before"] = jnp.asarray(np.triu(same_sub, 1), BF16)
    tabs["lower"] = jnp.asarray(np.tril(np.ones((N_EXPERTS, N_EXPERTS), np.float32), -1), BF16)
    g_final = g_final[None, :]
    h = x
    for i in range(DEPTH):
        lw = {name: a[i] for name, a in packed.items()}
        qat, ka, vat, ub, qct, kc, vct = _premix(h, lw, tabs)
        oa = _attention(qat, ka, vat, tq=Q_TILE_GQA, group=A_HEADS // A_KV_HEADS, dv=HEAD_DIM)
        oc = _attention(qct, kc, vct, tq=Q_TILE_MLA, group=1, dv=C_V)
        h1, m, route, offs, counts = _postmix(h, oa, oc, ub, lw, tabs)
        plan = jnp.concatenate([offs[..., 0], counts[..., 0]], axis=-1).reshape(-1, 2 * N_EXPERTS)
        y = _moe(m.reshape(B * S, D), route.reshape(-1, 8, MOE_SUB), plan, lw)
        h = _ple(h1, y.reshape(B, S, D), p, i, lw, g_final, final=(i == DEPTH - 1))
    return h


def kernel(x_prompt, x_sample, p_prompt, p_sample, g_mix, w_in, g_qa, g_ka, w_pool, s_pool, g_cq, w_q_up, g_ckv, w_kv_up, g_out_a, g_out_b, g_out_c, w_out, g_ffn, w_router_group, w_router_expert, w_gate, w_up, w_down, g_ple, w_ple_gate, w_ple_proj, g_final):
    weights = dict(g_mix=g_mix, w_in=w_in, g_qa=g_qa, g_ka=g_ka, w_pool=w_pool, s_pool=s_pool, g_cq=g_cq,
                   w_q_up=w_q_up, g_ckv=g_ckv, w_kv_up=w_kv_up, g_out_a=g_out_a, g_out_b=g_out_b,
                   g_out_c=g_out_c, w_out=w_out, g_ffn=g_ffn, w_router_group=w_router_group,
                   w_router_expert=w_router_expert, w_gate=w_gate, w_up=w_up, w_down=w_down, g_ple=g_ple,
                   w_ple_gate=w_ple_gate, w_ple_proj=w_ple_proj)
    packed = _pack_weights(weights)
    return (_trunk(x_prompt, p_prompt, packed, g_final), _trunk(x_sample, p_sample, packed, g_final))
```

```python
import functools
import math

import numpy as np
import jax
import jax.numpy as jnp
from jax import lax
from jax.experimental import pallas as pl
from jax.experimental.pallas import tpu as pltpu

F32 = jnp.float32
BF16 = jnp.bfloat16

D_MODEL = 1024
DEPTH = 4
GRID_W = 64
PLE_DIM = 256
HEAD_DIM = 64
ROPE_THETA = 10000.0
EPS = 1e-6
A_HEADS = 6
A_KV_HEADS = 2
A_WIDTH = A_HEADS * HEAD_DIM
B_GROUPS = 4
B_GROUP_DIM = 64
B_WINDOWS = (2, 4, 8, 16)
B_WIDTH = B_GROUPS * B_GROUP_DIM
C_HEADS = 6
C_NOPE = 64
C_ROPE = 32
C_V = 64
C_Q_RANK = 256
C_KV_RANK = 128
C_WIDTH = C_HEADS * C_V
C_QK_PAD = 128
MOE_GROUPS = 4
EXPERTS_PER_GROUP = 8
N_EXPERTS = MOE_GROUPS * EXPERTS_PER_GROUP
D_EXPERT = 256
ROUTER_ROWS = 48
LOG2E = math.log2(math.e)

LANES = 128
VMEM_LIMIT_BYTES = 56 * 1024 * 1024

TOKEN_TILE = 512
Q_TILE_GQA = 256
Q_TILE_MLA = 512
ATTN_STREAMS = 2
KEY_SUB = 512
LOOKAHEAD = 2
MOE_TILE = 1024
MOE_SUB = 256
SEG_ALIGN = 16
EXPERT_BLOCK = 32
MOE_TILE_GROUP = 8
EXPERTS_PER_STEP = 2
DISPATCH_STRIP = 256
V_PAD = 16
POOL_HALO = 8

_QK_A = A_WIDTH + A_KV_HEADS * HEAD_DIM
_V_A0 = _QK_A
_U_B0 = _V_A0 + A_KV_HEADS * HEAD_DIM
_CQ0 = _U_B0 + B_WIDTH
_CKV0 = _CQ0 + C_Q_RANK
_KR0 = _CKV0 + C_KV_RANK
_QK_SW0 = _KR0 + C_QK_PAD
IN_COLS_PACKED = _QK_SW0 + _QK_A


def _params(sem):
    return pltpu.CompilerParams(dimension_semantics=sem, vmem_limit_bytes=VMEM_LIMIT_BYTES)


def _rms(x):
    return x * lax.rsqrt(jnp.mean(x * x, axis=-1, keepdims=True) + EPS)


def _rope_partner(x):
    return pltpu.roll(x, x.shape[-1] - C_ROPE, x.ndim - 1)


def _premix_kernel(h_ref, g_mix_ref, w_in_ref, hsum_ref, g_qk_ref, g_qk_sw_ref, cosa_ref, sina_ref,
                   g_cq_ref, w_qup_ref, g_ckv_ref, w_kvup_ref, cosc_ref, sinc_ref,
                   qat_ref, ka_ref, vat_ref, ub_ref, qct_ref, kc_ref, vct_ref, *, c_scale):
    a = _rms(h_ref[0]) * g_mix_ref[...]
    z = jnp.dot(a.astype(BF16), w_in_ref[...], preferred_element_type=F32)

    qk = z[:, :_QK_A]
    qk_sw = z[:, _QK_SW0:]
    ss = jnp.dot((qk * qk).astype(BF16), hsum_ref[...], preferred_element_type=F32)
    reps = _QK_A // LANES
    cos_a = jnp.concatenate([cosa_ref[...]] * reps, axis=1) * g_qk_ref[...]
    sin_a = jnp.concatenate([sina_ref[...]] * reps, axis=1) * g_qk_sw_ref[...]
    qk = lax.rsqrt(ss * (1.0 / HEAD_DIM) + EPS) * (qk * cos_a + qk_sw * sin_a)
    qat = qk[:, :A_WIDTH].T.astype(BF16)
    tt = qat.shape[1]
    a_group = A_HEADS // A_KV_HEADS
    for hh in range(A_HEADS):
        for qi in range(tt // Q_TILE_GQA):
            col = (qi * a_group + hh % a_group) * Q_TILE_GQA
            qat_ref[0, hh // a_group, :, col:col + Q_TILE_GQA] = qat[
                hh * HEAD_DIM:(hh + 1) * HEAD_DIM, qi * Q_TILE_GQA:(qi + 1) * Q_TILE_GQA]
    ka = qk[:, A_WIDTH:].astype(BF16)
    for hh in range(A_KV_HEADS):
        ka_ref[0, hh] = ka[:, hh * HEAD_DIM:(hh + 1) * HEAD_DIM]
    ones_rows = (lax.broadcasted_iota(jnp.int32, (V_PAD, tt), 0) == 0).astype(BF16)
    vt = z[:, _V_A0:_U_B0].T.astype(BF16)
    for hh in range(A_KV_HEADS):
        vat_ref[0, hh, 0] = jnp.concatenate([vt[hh * HEAD_DIM:(hh + 1) * HEAD_DIM], ones_rows], axis=0)

    ub_ref[0] = z[:, _U_B0:_CQ0]

    cqn = _rms(z[:, _CQ0:_CKV0]) * g_cq_ref[...]
    qc = jnp.dot(cqn.astype(BF16), w_qup_ref[...], preferred_element_type=F32)
    cos_c = cosc_ref[...]
    sin_c = sinc_ref[...]
    cos_q = jnp.concatenate([cos_c * c_scale] * C_HEADS, axis=1)
    sin_q = jnp.concatenate([sin_c * c_scale] * C_HEADS, axis=1)
    qct = (qc * cos_q + _rope_partner(qc) * sin_q).T.astype(BF16)
    ckvn = _rms(z[:, _CKV0:_KR0]) * g_ckv_ref[...]
    kv = jnp.dot(ckvn.astype(BF16), w_kvup_ref[...], preferred_element_type=F32)
    kr = z[:, _KR0:_QK_SW0]
    kr = kr * cos_c + _rope_partner(kr) * sin_c
    for hh in range(C_HEADS):
        qct_ref[0, hh] = qct[hh * C_QK_PAD:(hh + 1) * C_QK_PAD]
        kc_ref[0, hh] = (kv[:, hh * C_QK_PAD:(hh + 1) * C_QK_PAD] + kr).astype(BF16)
    vct = kv[:, C_HEADS * C_QK_PAD:].T.astype(BF16)
    for hh in range(C_HEADS):
        vct_ref[0, hh, 0] = jnp.concatenate([vct[hh * C_V:(hh + 1) * C_V], ones_rows], axis=0)


def _premix(h, lw, tabs):
    B, S, D = h.shape
    Tt = TOKEN_TILE
    nt = S // Tt
    tile = lambda b, i: (b, i, 0)
    const2 = lambda b, i: (0, 0)
    headmajor = lambda b, i: (b, 0, i, 0)
    headmajor_t = lambda b, i: (b, 0, 0, i)
    vt_map = lambda b, i: (b, 0, i, 0, 0)
    tab_map = lambda b, i: (i, 0)
    in_specs = [
        pl.BlockSpec((1, Tt, D), tile),
        pl.BlockSpec((1, D), const2),
        pl.BlockSpec((D, IN_COLS_PACKED), const2),
        pl.BlockSpec((_QK_A, _QK_A), const2),
        pl.BlockSpec((1, _QK_A), const2),
        pl.BlockSpec((1, _QK_A), const2),
        pl.BlockSpec((Tt, LANES), tab_map),
        pl.BlockSpec((Tt, LANES), tab_map),
        pl.BlockSpec((1, C_Q_RANK), const2),
        pl.BlockSpec((C_Q_RANK, C_HEADS * C_QK_PAD), const2),
        pl.BlockSpec((1, C_KV_RANK), const2),
        pl.BlockSpec((C_KV_RANK, C_HEADS * C_QK_PAD + C_WIDTH), const2),
        pl.BlockSpec((Tt, LANES), tab_map),
        pl.BlockSpec((Tt, LANES), tab_map),
    ]
    out_shape = [
        jax.ShapeDtypeStruct((B, A_KV_HEADS, HEAD_DIM, S * (A_HEADS // A_KV_HEADS)), BF16),
        jax.ShapeDtypeStruct((B, A_KV_HEADS, S, HEAD_DIM), BF16),
        jax.ShapeDtypeStruct((B, A_KV_HEADS, nt, HEAD_DIM + V_PAD, Tt), BF16),
        jax.ShapeDtypeStruct((B, S, B_WIDTH), F32),
        jax.ShapeDtypeStruct((B, C_HEADS, C_QK_PAD, S), BF16),
        jax.ShapeDtypeStruct((B, C_HEADS, S, C_QK_PAD), BF16),
        jax.ShapeDtypeStruct((B, C_HEADS, nt, C_V + V_PAD, Tt), BF16),
    ]
    out_specs = [
        pl.BlockSpec((1, A_KV_HEADS, HEAD_DIM, Tt * (A_HEADS // A_KV_HEADS)), headmajor_t),
        pl.BlockSpec((1, A_KV_HEADS, Tt, HEAD_DIM), headmajor),
        pl.BlockSpec((1, A_KV_HEADS, 1, HEAD_DIM + V_PAD, Tt), vt_map),
        pl.BlockSpec((1, Tt, B_WIDTH), tile),
        pl.BlockSpec((1, C_HEADS, C_QK_PAD, Tt), headmajor_t),
        pl.BlockSpec((1, C_HEADS, Tt, C_QK_PAD), headmajor),
        pl.BlockSpec((1, C_HEADS, 1, C_V + V_PAD, Tt), vt_map),
    ]
    kern = functools.partial(_premix_kernel, c_scale=(C_NOPE + C_ROPE) ** -0.5 * LOG2E)
    return pl.pallas_call(
        kern, grid=(B, nt), in_specs=in_specs, out_specs=out_specs, out_shape=out_shape,
        compiler_params=_params(("parallel", "parallel")), name="premix",
    )(h, lw["g_mix"], lw["w_in"], tabs["hsum"], lw["g_qk"], lw["g_qk_sw"], tabs["cos_a"], tabs["sin_a"],
      lw["g_cq"], lw["w_q_up"], lw["g_ckv"], lw["w_kv_up"], tabs["cos_c"], tabs["sin_c"])


def _attn_kernel(qt_ref, k_ref, vt_ref, o_ref, s_buf, p_buf, *, order, n_chunks, chunk, sub, dv, group):
    n_streams, _, width = qt_ref.shape[1:]
    rows = vt_ref.shape[3]
    nsub = chunk // sub
    n_pos = len(order)
    per_body = 1 + max(coff for _, coff in order)
    assert LOOKAHEAD < n_pos and n_chunks % per_body == 0

    def score_piece(pos, j, r):
        si = order[pos][0]
        start = pl.multiple_of(j * chunk + r * sub, sub)
        s = jnp.dot(k_ref[0, si, pl.ds(start, sub), :], qt_ref[0, si], preferred_element_type=F32)
        s_buf[pos, r * sub:(r + 1) * sub, :] = s
        return jnp.max(s, axis=0, keepdims=True)

    def prob_piece(pos, r, m_b):
        s = s_buf[pos, r * sub:(r + 1) * sub, :]
        p_buf[pos, r * sub:(r + 1) * sub, :] = jnp.exp2(s - m_b).astype(BF16)

    def all_scores(pos, j):
        cm = None
        for r in range(nsub):
            c = score_piece(pos, j, r)
            cm = c if cm is None else jnp.maximum(cm, c)
        return cm

    def body(b, carry):
        pending, state = carry
        colmax = dict(enumerate(pending))
        state = list(state)
        for i, (si, coff) in enumerate(order):
            m, acc = state[si]
            m_new = jnp.maximum(m, colmax[i])
            m_b = jnp.broadcast_to(m_new, (sub, width))
            tpos = (i + LOOKAHEAD) % n_pos
            tj = jnp.minimum((b + (i + LOOKAHEAD) // n_pos) * per_body + order[tpos][1], n_chunks - 1)
            cm = None
            for r in range(nsub):
                c = score_piece(tpos, tj, r)
                cm = c if cm is None else jnp.maximum(cm, c)
                prob_piece(i, r, m_b)
            colmax[i + LOOKAHEAD] = cm
            pv = jnp.dot(vt_ref[0, si, b * per_body + coff], p_buf[i], preferred_element_type=F32)
            state[si] = (m_new, jnp.exp2(m - m_new) * acc + pv)
        return tuple(colmax[n_pos + k] for k in range(LOOKAHEAD)), tuple(state)

    pending = tuple(all_scores(k, order[k][1]) for k in range(LOOKAHEAD))
    init = tuple((jnp.full((1, width), -jnp.inf, F32), jnp.zeros((rows, width), F32)) for _ in range(n_streams))
    _, final = lax.fori_loop(0, n_chunks // per_body, body, (pending, init))
    tq = width // group
    outs = []
    for _, acc in final:
        out_t = acc[:dv] * (1.0 / acc[dv:dv + 1])
        outs += [out_t[:, g * tq:(g + 1) * tq] for g in range(group)]
    o_ref[0] = jnp.concatenate(outs, axis=0).T


def _attention(qt, k, vt, *, tq, group, dv):
    B, Hkv, dk, _ = qt.shape
    S = k.shape[2]
    nk, rows, Tk = vt.shape[2:]
    n_streams = ATTN_STREAMS
    width = group * tq
    per_body = 2
    order = tuple((si, coff) for coff in range(per_body) for si in range(n_streams))
    kern = functools.partial(_attn_kernel, order=order, n_chunks=nk, chunk=Tk, sub=KEY_SUB, dv=dv, group=group)
    return pl.pallas_call(
        kern, grid=(B, Hkv // n_streams, S // tq),
        in_specs=[
            pl.BlockSpec((1, n_streams, dk, width), lambda b, g, i: (b, g, 0, i)),
            pl.BlockSpec((1, n_streams, S, dk), lambda b, g, i: (b, g, 0, 0)),
            pl.BlockSpec((1, n_streams, nk, rows, Tk), lambda b, g, i: (b, g, 0, 0, 0)),
        ],
        out_specs=pl.BlockSpec((1, tq, n_streams * group * dv), lambda b, g, i: (b, i, g)),
        out_shape=jax.ShapeDtypeStruct((B, S, Hkv * group * dv), F32),
        scratch_shapes=[pltpu.VMEM((len(order), Tk, width), F32), pltpu.VMEM((len(order), Tk, width), BF16)],
        compiler_params=_params(("parallel", "parallel", "arbitrary")), name="attention",
    )(qt, k, vt)


def _route(logits):
    tt = logits.shape[1]
    le = logits[:N_EXPERTS]
    lg = logits[N_EXPERTS:]
    row8 = lax.broadcasted_iota(jnp.int32, (ROUTER_ROWS - N_EXPERTS, tt), 0)
    lg = jnp.where(row8 < MOE_GROUPS, lg, -jnp.inf)
    gmax = jnp.max(lg, axis=0, keepdims=True)
    gsel = jnp.min(jnp.where(lg == gmax, row8, MOE_GROUPS), axis=0, keepdims=True)
    gprob = 1.0 / jnp.sum(jnp.exp(lg - gmax), axis=0, keepdims=True)
    row = lax.broadcasted_iota(jnp.int32, (N_EXPERTS, tt), 0)
    lm = jnp.where((row // EXPERTS_PER_GROUP) == gsel, le, -jnp.inf)
    m1 = jnp.max(lm, axis=0, keepdims=True)
    i1 = jnp.min(jnp.where(lm == m1, row, N_EXPERTS), axis=0, keepdims=True)
    lm2 = jnp.where(row == i1, -jnp.inf, lm)
    m2 = jnp.max(lm2, axis=0, keepdims=True)
    i2 = jnp.min(jnp.where(lm2 == m2, row, N_EXPERTS), axis=0, keepdims=True)
    r = jnp.exp(m2 - m1)
    w1 = gprob / (1.0 + r)
    w2 = w1 * r
    return i1, i2, w1, w2


def _dispatch_plan(i1, i2, w1, w2, before_ref, lower_ref):
    tt = i1.shape[1]
    row = lax.broadcasted_iota(jnp.int32, (N_EXPERTS, tt), 0)
    hit1 = row == i1
    hit2 = row == i2
    assign = jnp.where(hit1 | hit2, 1.0, 0.0)
    rank = jnp.dot(assign.astype(BF16), before_ref[...], preferred_element_type=F32)
    pos, offs, counts = [], [], []
    for u in range(tt // MOE_SUB):
        lanes = slice(u * MOE_SUB, (u + 1) * MOE_SUB)
        count = jnp.sum(assign[:, lanes], axis=1, keepdims=True)
        segs = jnp.floor((count + (SEG_ALIGN - 1.0)) * (1.0 / SEG_ALIGN))
        segs_b = jnp.broadcast_to(segs, (N_EXPERTS, LANES)).astype(BF16)
        off = SEG_ALIGN * jnp.dot(lower_ref[...], segs_b, preferred_element_type=F32)
        pos.append(off[:, :1] + rank[:, lanes])
        offs.append(off.astype(jnp.int32))
        counts.append(jnp.broadcast_to(count, (N_EXPERTS, LANES)).astype(jnp.int32))
    pos = jnp.concatenate(pos, axis=1)
    pos1 = jnp.sum(jnp.where(hit1, pos, 0.0), axis=0, keepdims=True)
    pos2 = jnp.sum(jnp.where(hit2, pos, 0.0), axis=0, keepdims=True)
    route = jnp.concatenate([pos1, pos2, w1, w2, jnp.zeros((4, tt), F32)], axis=0)
    return route, offs, counts


def _postmix_kernel(h_ref, oa_ref, oc_ref, ub_ref, prev_ref, next_ref, wpool_ref, spool_ref,
                    ga_ref, gb_ref, gc_ref, wout_ref, gffn_ref, wr_ref, before_ref, lower_ref,
                    h1_ref, m_ref, route_ref, offs_ref, counts_ref, ext_ref, *, seq_len):
    i = pl.program_id(1)
    nt = pl.num_programs(1)
    tt = h_ref.shape[1]
    half = B_WIDTH // 2

    u = ub_ref[0]
    ext_ref[POOL_HALO:POOL_HALO + tt, :] = u
    ext_ref[:POOL_HALO, :] = jnp.where(i > 0, prev_ref[0, 0], 0.0)
    ext_ref[POOL_HALO + tt:, :] = jnp.where(i < nt - 1, next_ref[0, 0], 0.0)

    def window(lo, hi, lanes):
        acc = None
        for d in range(lo, hi):
            piece = ext_ref[POOL_HALO + d:POOL_HALO + d + tt, lanes]
            acc = piece if acc is None else acc + piece
        return acc

    lo_lanes = slice(0, half)
    hi_lanes = slice(half, B_WIDTH)
    w2 = window(-1, 1, lo_lanes)
    w4 = w2 + window(-2, -1, lo_lanes) + window(1, 2, lo_lanes)
    w8 = window(-4, 4, hi_lanes)
    w16 = w8 + window(-8, -4, hi_lanes) + window(4, 8, hi_lanes)
    lane = lax.broadcasted_iota(jnp.int32, (tt, half), 1)
    first = lane < B_GROUP_DIM
    sums = jnp.concatenate([jnp.where(first, w2, w4), jnp.where(first, w8, w16)], axis=1)
    pos = i * tt + lax.broadcasted_iota(jnp.int32, (tt, B_WIDTH), 0)
    lane_b = lax.broadcasted_iota(jnp.int32, (tt, B_WIDTH), 1)
    hw = jnp.left_shift(1, lane_b // B_GROUP_DIM)
    cnt = jnp.minimum(pos + hw, seq_len) - jnp.maximum(pos - hw, 0)
    dlt = sums / cnt.astype(F32) - u
    ob = jnp.dot(dlt.astype(BF16), wpool_ref[...], preferred_element_type=F32) * spool_ref[...]

    merged = jnp.concatenate([
        (_rms(oa_ref[0]) * ga_ref[...]).astype(BF16),
        (_rms(ob) * gb_ref[...]).astype(BF16),
        (_rms(oc_ref[0]) * gc_ref[...]).astype(BF16)], axis=1)
    h1 = h_ref[0] + jnp.dot(merged, wout_ref[...], preferred_element_type=F32)
    h1_ref[0] = h1

    m = _rms(h1) * gffn_ref[...]
    m_hi = m.astype(BF16)
    m_ref[0] = m_hi
    m_lo = (m - m_hi.astype(F32)).astype(BF16)
    nt_dims = (((1,), (1,)), ((), ()))
    both = lax.dot_general(wr_ref[...], m_hi, nt_dims, preferred_element_type=F32)
    logits = (both[:ROUTER_ROWS] + both[ROUTER_ROWS:]
              + lax.dot_general(wr_ref[:ROUTER_ROWS, :], m_lo, nt_dims, preferred_element_type=F32))
    route, offs, counts = _dispatch_plan(*_route(logits), before_ref, lower_ref)
    for u in range(len(offs)):
        route_ref[0, 0, u] = route[:, u * MOE_SUB:(u + 1) * MOE_SUB]
        offs_ref[0, 0, u] = offs[u]
        counts_ref[0, 0, u] = counts[u]


def _postmix(h, oa, oc, ub, lw, tabs):
    B, S, D = h.shape
    Tt = min(MOE_TILE, S)
    assert Tt % MOE_SUB == 0
    nt = S // Tt
    rows = Tt // POOL_HALO
    ub_rows = ub.reshape(B, S // POOL_HALO, POOL_HALO, B_WIDTH)
    tile = lambda b, i: (b, i, 0)
    const2 = lambda b, i: (0, 0)
    in_specs = [
        pl.BlockSpec((1, Tt, D), tile),
        pl.BlockSpec((1, Tt, A_WIDTH), tile),
        pl.BlockSpec((1, Tt, C_WIDTH), tile),
        pl.BlockSpec((1, Tt, B_WIDTH), tile),
        pl.BlockSpec((1, 1, POOL_HALO, B_WIDTH), lambda b, i: (b, jnp.maximum(i * rows - 1, 0), 0, 0)),
        pl.BlockSpec((1, 1, POOL_HALO, B_WIDTH),
                     lambda b, i: (b, jnp.minimum((i + 1) * rows, S // POOL_HALO - 1), 0, 0)),
        pl.BlockSpec((B_WIDTH, B_WIDTH), const2),
        pl.BlockSpec((1, B_WIDTH), const2),
        pl.BlockSpec((1, A_WIDTH), const2),
        pl.BlockSpec((1, B_WIDTH), const2),
        pl.BlockSpec((1, C_WIDTH), const2),
        pl.BlockSpec((D, D), const2),
        pl.BlockSpec((1, D), const2),
        pl.BlockSpec((2 * ROUTER_ROWS, D), const2),
        pl.BlockSpec((Tt, Tt), const2),
        pl.BlockSpec((N_EXPERTS, N_EXPERTS), const2),
    ]
    per_tile = lambda b, i: (b, i, 0, 0, 0)
    n_sub = Tt // MOE_SUB
    out_shape = [
        jax.ShapeDtypeStruct((B, S, D), F32),
        jax.ShapeDtypeStruct((B, S, D), BF16),
        jax.ShapeDtypeStruct((B, nt, n_sub, 8, MOE_SUB), F32),
        jax.ShapeDtypeStruct((B, nt, n_sub, N_EXPERTS, LANES), jnp.int32),
        jax.ShapeDtypeStruct((B, nt, n_sub, N_EXPERTS, LANES), jnp.int32),
    ]
    out_specs = [
        pl.BlockSpec((1, Tt, D), tile),
        pl.BlockSpec((1, Tt, D), tile),
        pl.BlockSpec((1, 1, n_sub, 8, MOE_SUB), per_tile),
        pl.BlockSpec((1, 1, n_sub, N_EXPERTS, LANES), per_tile),
        pl.BlockSpec((1, 1, n_sub, N_EXPERTS, LANES), per_tile),
    ]
    return pl.pallas_call(
        functools.partial(_postmix_kernel, seq_len=S), grid=(B, nt),
        in_specs=in_specs, out_specs=out_specs, out_shape=out_shape,
        scratch_shapes=[pltpu.VMEM((Tt + 2 * POOL_HALO, B_WIDTH), F32)],
        compiler_params=_params(("parallel", "parallel")), name="postmix",
    )(h, oa, oc, ub, ub_rows, ub_rows, lw["w_pool"], lw["s_pool"], lw["g_out_a"], lw["g_out_b"],
      lw["g_out_c"], lw["w_out"], lw["g_ffn"], lw["w_router"],
      tabs["before"], tabs["lower"])


def _moe_kernel(plan_ref, m_ref, route_ref, wgu_ref, wd_ref, o_ref, perm_ref, xy_ref, ws_ref, *, n_rows):
    g = pl.program_id(0)
    e = pl.program_id(1)
    n_tiles, _, tm = route_ref.shape
    d = m_ref.shape[1]

    @pl.when(e == 0)
    def _():
        for k in range(n_tiles):
            route = route_ref[k]
            pos1 = route[0:1].astype(jnp.int32)
            pos2 = route[1:2].astype(jnp.int32)
            w1 = route[2:3]
            w2 = route[3:4]
            m = m_ref[k * tm:(k + 1) * tm, :]

            def strip(i, _, k=k, pos1=pos1, pos2=pos2, w1=w1, w2=w2, m=m):
                r0 = pl.multiple_of(i * DISPATCH_STRIP, DISPATCH_STRIP)
                rows = r0 + lax.broadcasted_iota(jnp.int32, (DISPATCH_STRIP, tm), 0)
                hit1 = rows == pos1
                hit2 = rows == pos2
                perm = jnp.where(hit1 | hit2, 1.0, 0.0).astype(BF16)
                perm_ref[k, pl.ds(r0, DISPATCH_STRIP), :] = perm
                ws_ref[k, pl.ds(r0, DISPATCH_STRIP), :] = jnp.sum(
                    jnp.where(hit1, w1, jnp.where(hit2, w2, 0.0)), axis=1, keepdims=True)
                xy_ref[k, pl.ds(r0, DISPATCH_STRIP), :] = jnp.dot(
                    perm, m, preferred_element_type=F32).astype(BF16)
                return 0

            lax.fori_loop(0, n_rows // DISPATCH_STRIP, strip, 0, unroll=True)
            xy_ref[k, n_rows:, :] = jnp.zeros((EXPERT_BLOCK, d), BF16)
            ws_ref[k, n_rows:, :] = jnp.zeros((EXPERT_BLOCK, 1), F32)

    def blocks(j, plans, bi):
        starts = [pl.multiple_of(jnp.minimum(off + bi * EXPERT_BLOCK, n_rows), SEG_ALIGN) for off, _ in plans]
        xs = [xy_ref[k, pl.ds(r0, EXPERT_BLOCK), :] for k, r0 in enumerate(starts)]
        ws = [ws_ref[k, pl.ds(r0, EXPERT_BLOCK), :] for k, r0 in enumerate(starts)]
        gu = jnp.dot(jnp.concatenate(xs, axis=0), wgu_ref[j], preferred_element_type=F32)
        gate = gu[:, :D_EXPERT]
        hid = gate * jax.nn.sigmoid(gate) * gu[:, D_EXPERT:] * jnp.concatenate(ws, axis=0)
        y = jnp.dot(hid.astype(BF16), wd_ref[j], preferred_element_type=F32).astype(BF16)
        row = lax.broadcasted_iota(jnp.int32, (EXPERT_BLOCK, 1), 0)
        for k, ((_, cnt), r0, x) in enumerate(zip(plans, starts, xs)):
            in_segment = row < cnt - bi * EXPERT_BLOCK
            xy_ref[k, pl.ds(r0, EXPERT_BLOCK), :] = jnp.where(
                in_segment, y[k * EXPERT_BLOCK:(k + 1) * EXPERT_BLOCK], x)

    for j in range(wgu_ref.shape[0]):
        expert = e * wgu_ref.shape[0] + j
        plans = [(plan_ref[g * n_tiles + k, expert], plan_ref[g * n_tiles + k, N_EXPERTS + expert])
                 for k in range(n_tiles)]
        blocks(j, plans, 0)
        longest = functools.reduce(jnp.maximum, [cnt for _, cnt in plans])
        n_blocks = lax.shift_right_logical(longest + (EXPERT_BLOCK - 1), EXPERT_BLOCK.bit_length() - 1)

        def more(bi, _, j=j, plans=plans):
            blocks(j, plans, bi)
            return 0

        lax.fori_loop(1, n_blocks, more, 0)

    @pl.when(e == pl.num_programs(1) - 1)
    def _():
        tn_dims = (((0,), (0,)), ((), ()))
        for k in range(n_tiles):
            y = lax.dot_general(perm_ref[k], xy_ref[k, :n_rows, :], tn_dims, preferred_element_type=F32)
            o_ref[k * tm:(k + 1) * tm, :] = y.astype(BF16)


def _moe(m, route, plan, lw):
    T, D = m.shape
    n_all, _, Tm = route.shape
    group = math.gcd(MOE_TILE_GROUP, n_all)
    n_rows = 2 * Tm + N_EXPERTS * SEG_ALIGN
    assert n_rows % DISPATCH_STRIP == 0
    tile = lambda g, e, plan: (g, 0)
    grid_spec = pltpu.PrefetchScalarGridSpec(
        num_scalar_prefetch=1, grid=(n_all // group, N_EXPERTS // EXPERTS_PER_STEP),
        in_specs=[
            pl.BlockSpec((group * Tm, D), tile),
            pl.BlockSpec((group, 8, Tm), lambda g, e, plan: (g, 0, 0)),
            pl.BlockSpec((EXPERTS_PER_STEP, D, 2 * D_EXPERT), lambda g, e, plan: (e, 0, 0)),
            pl.BlockSpec((EXPERTS_PER_STEP, D_EXPERT, D), lambda g, e, plan: (e, 0, 0)),
        ],
        out_specs=pl.BlockSpec((group * Tm, D), tile),
        scratch_shapes=[
            pltpu.VMEM((group, n_rows, Tm), BF16),
            pltpu.VMEM((group, n_rows + EXPERT_BLOCK, D), BF16),
            pltpu.VMEM((group, n_rows + EXPERT_BLOCK, 1), F32),
        ],
    )
    return pl.pallas_call(
        functools.partial(_moe_kernel, n_rows=n_rows), grid_spec=grid_spec,
        out_shape=jax.ShapeDtypeStruct((T, D), BF16),
        compiler_params=_params(("parallel", "arbitrary")), name="moe",
    )(plan, m, route, lw["w_gate_up"], lw["w_down"])


def _ple_kernel(h1_ref, y_ref, p_ref, gple_ref, wgate_ref, wproj_ref, gfinal_ref, o_ref, *, final):
    h = h1_ref[0] + y_ref[0].astype(F32)
    r = (_rms(h) * gple_ref[...]).astype(BF16)
    gate = jax.nn.sigmoid(jnp.dot(r, wgate_ref[...], preferred_element_type=F32))
    emb = jnp.dot(p_ref[0, 0].astype(BF16), wproj_ref[...], preferred_element_type=F32)
    out = h + gate * emb
    if final:
        out = _rms(out) * gfinal_ref[...]
    o_ref[0] = out


def _ple(h1, y, p, layer, lw, g_final, final):
    B, S, D = h1.shape
    Tt = TOKEN_TILE
    tile = lambda b, i: (b, i, 0)
    const2 = lambda b, i: (0, 0)
    return pl.pallas_call(
        functools.partial(_ple_kernel, final=final), grid=(B, S // Tt),
        in_specs=[
            pl.BlockSpec((1, Tt, D), tile),
            pl.BlockSpec((1, Tt, D), tile),
            pl.BlockSpec((1, 1, Tt, PLE_DIM), lambda b, i: (layer, b, i, 0)),
            pl.BlockSpec((1, D), const2),
            pl.BlockSpec((D, D), const2),
            pl.BlockSpec((PLE_DIM, D), const2),
            pl.BlockSpec((1, D), const2),
        ],
        out_specs=pl.BlockSpec((1, Tt, D), tile),
        out_shape=jax.ShapeDtypeStruct((B, S, D), F32),
        compiler_params=_params(("parallel", "parallel")), name="ple",
    )(h1, y, p, lw["g_ple"], lw["w_ple_gate"], lw["w_ple_proj"], g_final)


def _rope_tables(seq_len):
    rows_n = seq_len // GRID_W
    row = jnp.repeat(jnp.arange(rows_n), GRID_W).astype(F32)[:, None]
    col = jnp.tile(jnp.arange(GRID_W), rows_n).astype(F32)[:, None]

    def axial(rot_dim):
        n_freq = rot_dim // 4
        inv = ROPE_THETA ** (-jnp.arange(n_freq, dtype=F32) / n_freq)
        ar, ac = row * inv, col * inv
        cos = jnp.concatenate([jnp.cos(ar), jnp.cos(ar), jnp.cos(ac), jnp.cos(ac)], axis=1)
        sin = jnp.concatenate([-jnp.sin(ar), jnp.sin(ar), -jnp.sin(ac), jnp.sin(ac)], axis=1)
        return cos, sin

    cos_a, sin_a = axial(HEAD_DIM)
    cos_r, sin_r = axial(C_ROPE)
    ones = jnp.ones((seq_len, C_NOPE), F32)
    zeros = jnp.zeros((seq_len, C_NOPE), F32)
    pad = jnp.zeros((seq_len, C_QK_PAD - C_NOPE - C_ROPE), F32)
    return dict(
        cos_a=jnp.concatenate([cos_a, cos_a], axis=1), sin_a=jnp.concatenate([sin_a, sin_a], axis=1),
        cos_c=jnp.concatenate([ones, cos_r, pad], axis=1), sin_c=jnp.concatenate([zeros, sin_r, pad], axis=1),
    )


def _pack_weights(w):
    L = DEPTH
    w_in = w["w_in"]
    def with_swapped(rope):
        parts = rope.reshape(rope.shape[:-1] + (2, 2, C_ROPE // 4))
        return jnp.concatenate([rope, parts[..., ::-1, :].reshape(rope.shape)], axis=-1)

    kr_pad = jnp.pad(with_swapped(w_in[:, :, _KR0:]), ((0, 0), (0, 0), (C_NOPE, 0)))
    q_scale = jnp.full((A_WIDTH,), HEAD_DIM ** -0.5 * LOG2E, F32)
    k_scale = jnp.ones((A_KV_HEADS * HEAD_DIM,), F32)
    g_qk = jnp.concatenate([jnp.tile(w["g_qa"], (1, A_HEADS)), jnp.tile(w["g_ka"], (1, A_KV_HEADS))], axis=1)
    g_qk = g_qk * jnp.concatenate([q_scale, k_scale])[None, :]

    def halves_swapped(cols):
        parts = cols.reshape(cols.shape[:-1] + (_QK_A // (HEAD_DIM // 2), 2, HEAD_DIM // 4))
        return parts[..., ::-1, :].reshape(cols.shape)
    w_q_up = w["w_q_up"].reshape(L, C_Q_RANK, C_HEADS, C_NOPE + C_ROPE)
    w_q_up = jnp.concatenate([w_q_up[..., :C_NOPE], with_swapped(w_q_up[..., C_NOPE:])], axis=-1)
    kv = w["w_kv_up"].reshape(L, C_KV_RANK, C_HEADS, C_NOPE + C_V)
    k_nope = jnp.pad(kv[..., :C_NOPE], ((0, 0), (0, 0), (0, 0), (0, C_QK_PAD - C_NOPE)))
    w_kv_up = jnp.concatenate([k_nope.reshape(L, C_KV_RANK, C_HEADS * C_QK_PAD),
                               kv[..., C_NOPE:].reshape(L, C_KV_RANK, C_WIDTH)], axis=2)
    eye = jnp.eye(B_GROUPS, dtype=F32)
    w_pool = (w["w_pool"][:, :, :, None, :] * eye[None, :, None, :, None]).reshape(L, B_WIDTH, B_WIDTH)
    w_router = jnp.concatenate([w["w_router_expert"], w["w_router_group"]], axis=2)
    w_router = jnp.pad(jnp.swapaxes(w_router, 1, 2), ((0, 0), (0, ROUTER_ROWS - N_EXPERTS - MOE_GROUPS), (0, 0)))
    w_router_hi = w_router.astype(BF16)
    w_router_lo = (w_router - w_router_hi.astype(F32)).astype(BF16)
    row = lambda a: a[:, None, :]
    return dict(
        g_mix=row(w["g_mix"]),
        w_in=jnp.concatenate([w_in[:, :, :_KR0], kr_pad, halves_swapped(w_in[:, :, :_QK_A])], axis=2).astype(BF16),
        g_qk=row(g_qk), g_qk_sw=row(halves_swapped(g_qk)),
        g_cq=row(w["g_cq"]), w_q_up=w_q_up.reshape(L, C_Q_RANK, C_HEADS * C_QK_PAD).astype(BF16),
        g_ckv=row(w["g_ckv"]), w_kv_up=w_kv_up.astype(BF16),
        w_pool=w_pool.astype(BF16), s_pool=row(w["s_pool"]),
        g_out_a=row(w["g_out_a"]), g_out_b=row(w["g_out_b"]), g_out_c=row(w["g_out_c"]),
        w_out=w["w_out"].astype(BF16), g_ffn=row(w["g_ffn"]),
        w_router=jnp.concatenate([w_router_hi, w_router_lo], axis=1),
        w_gate_up=jnp.concatenate([w["w_gate"], w["w_up"]], axis=3).astype(BF16),
        w_down=w["w_down"].astype(BF16),
        g_ple=row(w["g_ple"]), w_ple_gate=w["w_ple_gate"].astype(BF16), w_ple_proj=w["w_ple_proj"].astype(BF16),
    )


def _trunk(x, p, packed, g_final):
    B, S, D = x.shape
    tabs = _rope_tables(S)
    hs = np.kron(np.eye(_QK_A // HEAD_DIM, dtype=np.float32), np.ones((HEAD_DIM, HEAD_DIM), np.float32))
    tabs["hsum"] = jnp.asarray(hs, BF16)
    tm = min(MOE_TILE, S)
    same_sub = np.kron(np.eye(tm // MOE_SUB, dtype=np.float32), np.ones((MOE_SUB, MOE_SUB), np.float32))
    tabs["before"] = jnp.asarray(np.triu(same_sub, 1), BF16)
    tabs["lower"] = jnp.asarray(np.tril(np.ones((N_EXPERTS, N_EXPERTS), np.float32), -1), BF16)
    g_final = g_final[None, :]
    h = x
    for i in range(DEPTH):
        lw = {name: a[i] for name, a in packed.items()}
        qat, ka, vat, ub, qct, kc, vct = _premix(h, lw, tabs)
        oa = _attention(qat, ka, vat, tq=Q_TILE_GQA, group=A_HEADS // A_KV_HEADS, dv=HEAD_DIM)
        oc = _attention(qct, kc, vct, tq=Q_TILE_MLA, group=1, dv=C_V)
        h1, m, route, offs, counts = _postmix(h, oa, oc, ub, lw, tabs)
        plan = jnp.concatenate([offs[..., 0], counts[..., 0]], axis=-1).reshape(-1, 2 * N_EXPERTS)
        y = _moe(m.reshape(B * S, D), route.reshape(-1, 8, MOE_SUB), plan, lw)
        h = _ple(h1, y.reshape(B, S, D), p, i, lw, g_final, final=(i == DEPTH - 1))
    return h


def kernel(x_prompt, x_sample, p_prompt, p_sample, g_mix, w_in, g_qa, g_ka, w_pool, s_pool, g_cq, w_q_up, g_ckv, w_kv_up, g_out_a, g_out_b, g_out_c, w_out, g_ffn, w_router_group, w_router_expert, w_gate, w_up, w_down, g_ple, w_ple_gate, w_ple_proj, g_final):
    weights = dict(g_mix=g_mix, w_in=w_in, g_qa=g_qa, g_ka=g_ka, w_pool=w_pool, s_pool=s_pool, g_cq=g_cq,
                   w_q_up=w_q_up, g_ckv=g_ckv, w_kv_up=w_kv_up, g_out_a=g_out_a, g_out_b=g_out_b,
                   g_out_c=g_out_c, w_out=w_out, g_ffn=g_ffn, w_router_group=w_router_group,
                   w_router_expert=w_router_expert, w_gate=w_gate, w_up=w_up, w_down=w_down, g_ple=g_ple,
                   w_ple_gate=w_ple_gate, w_ple_proj=w_ple_proj)
    packed = _pack_weights(weights)
    return (_trunk(x_prompt, p_prompt, packed, g_final), _trunk(x_sample, p_sample, packed, g_final))
```

```python
import functools
import math

import numpy as np
import jax
import jax.numpy as jnp
from jax import lax
from jax.experimental import pallas as pl
from jax.experimental.pallas import tpu as pltpu

F32 = jnp.float32
BF16 = jnp.bfloat16

D_MODEL = 1024
DEPTH = 4
GRID_W = 64
PLE_DIM = 256
HEAD_DIM = 64
ROPE_THETA = 10000.0
EPS = 1e-6
A_HEADS = 6
A_KV_HEADS = 2
A_WIDTH = A_HEADS * HEAD_DIM
B_GROUPS = 4
B_GROUP_DIM = 64
B_WINDOWS = (2, 4, 8, 16)
B_WIDTH = B_GROUPS * B_GROUP_DIM
C_HEADS = 6
C_NOPE = 64
C_ROPE = 32
C_V = 64
C_Q_RANK = 256
C_KV_RANK = 128
C_WIDTH = C_HEADS * C_V
C_QK_PAD = 128
MOE_GROUPS = 4
EXPERTS_PER_GROUP = 8
N_EXPERTS = MOE_GROUPS * EXPERTS_PER_GROUP
D_EXPERT = 256
ROUTER_ROWS = 48
LOG2E = math.log2(math.e)

LANES = 128
VMEM_LIMIT_BYTES = 56 * 1024 * 1024

TOKEN_TILE = 512
Q_TILE_GQA = 256
Q_TILE_MLA = 512
ATTN_STREAMS = 2
KEY_SUB = 512
LOOKAHEAD = 2
MOE_TILE = 1024
MOE_SUB = 256
SEG_ALIGN = 16
EXPERT_BLOCK = 32
MOE_TILE_GROUP = 8
EXPERTS_PER_STEP = 2
DISPATCH_STRIP = 256
V_PAD = 16
POOL_HALO = 8

_QK_A = A_WIDTH + A_KV_HEADS * HEAD_DIM
_V_A0 = _QK_A
_U_B0 = _V_A0 + A_KV_HEADS * HEAD_DIM
_CQ0 = _U_B0 + B_WIDTH
_CKV0 = _CQ0 + C_Q_RANK
_KR0 = _CKV0 + C_KV_RANK
_QK_SW0 = _KR0 + C_QK_PAD
IN_COLS_PACKED = _QK_SW0 + _QK_A


def _params(sem):
    return pltpu.CompilerParams(dimension_semantics=sem, vmem_limit_bytes=VMEM_LIMIT_BYTES)


def _rms(x):
    return x * lax.rsqrt(jnp.mean(x * x, axis=-1, keepdims=True) + EPS)


def _rope_partner(x):
    return pltpu.roll(x, x.shape[-1] - C_ROPE, x.ndim - 1)


def _premix_kernel(h_ref, g_mix_ref, w_in_ref, hsum_ref, g_qk_ref, g_qk_sw_ref, cosa_ref, sina_ref,
                   g_cq_ref, w_qup_ref, g_ckv_ref, w_kvup_ref, cosc_ref, sinc_ref,
                   qat_ref, ka_ref, vat_ref, ub_ref, qct_ref, kc_ref, vct_ref, *, c_scale):
    a = _rms(h_ref[0]) * g_mix_ref[...]
    z = jnp.dot(a.astype(BF16), w_in_ref[...], preferred_element_type=F32)

    qk = z[:, :_QK_A]
    qk_sw = z[:, _QK_SW0:]
    ss = jnp.dot((qk * qk).astype(BF16), hsum_ref[...], preferred_element_type=F32)
    reps = _QK_A // LANES
    cos_a = jnp.concatenate([cosa_ref[...]] * reps, axis=1) * g_qk_ref[...]
    sin_a = jnp.concatenate([sina_ref[...]] * reps, axis=1) * g_qk_sw_ref[...]
    qk = lax.rsqrt(ss * (1.0 / HEAD_DIM) + EPS) * (qk * cos_a + qk_sw * sin_a)
    qat = qk[:, :A_WIDTH].T.astype(BF16)
    tt = qat.shape[1]
    a_group = A_HEADS // A_KV_HEADS
    for hh in range(A_HEADS):
        for qi in range(tt // Q_TILE_GQA):
            col = (qi * a_group + hh % a_group) * Q_TILE_GQA
            qat_ref[0, hh // a_group, :, col:col + Q_TILE_GQA] = qat[
                hh * HEAD_DIM:(hh + 1) * HEAD_DIM, qi * Q_TILE_GQA:(qi + 1) * Q_TILE_GQA]
    ka = qk[:, A_WIDTH:].astype(BF16)
    for hh in range(A_KV_HEADS):
        ka_ref[0, hh] = ka[:, hh * HEAD_DIM:(hh + 1) * HEAD_DIM]
    ones_rows = (lax.broadcasted_iota(jnp.int32, (V_PAD, tt), 0) == 0).astype(BF16)
    vt = z[:, _V_A0:_U_B0].T.astype(BF16)
    for hh in range(A_KV_HEADS):
        vat_ref[0, hh, 0] = jnp.concatenate([vt[hh * HEAD_DIM:(hh + 1) * HEAD_DIM], ones_rows], axis=0)

    ub_ref[0] = z[:, _U_B0:_CQ0]

    cqn = _rms(z[:, _CQ0:_CKV0]) * g_cq_ref[...]
    qc = jnp.dot(cqn.astype(BF16), w_qup_ref[...], preferred_element_type=F32)
    cos_c = cosc_ref[...]
    sin_c = sinc_ref[...]
    cos_q = jnp.concatenate([cos_c * c_scale] * C_HEADS, axis=1)
    sin_q = jnp.concatenate([sin_c * c_scale] * C_HEADS, axis=1)
    qct = (qc * cos_q + _rope_partner(qc) * sin_q).T.astype(BF16)
    ckvn = _rms(z[:, _CKV0:_KR0]) * g_ckv_ref[...]
    kv = jnp.dot(ckvn.astype(BF16), w_kvup_ref[...], preferred_element_type=F32)
    kr = z[:, _KR0:_QK_SW0]
    kr = kr * cos_c + _rope_partner(kr) * sin_c
    for hh in range(C_HEADS):
        qct_ref[0, hh] = qct[hh * C_QK_PAD:(hh + 1) * C_QK_PAD]
        kc_ref[0, hh] = (kv[:, hh * C_QK_PAD:(hh + 1) * C_QK_PAD] + kr).astype(BF16)
    vct = kv[:, C_HEADS * C_QK_PAD:].T.astype(BF16)
    for hh in range(C_HEADS):
        vct_ref[0, hh, 0] = jnp.concatenate([vct[hh * C_V:(hh + 1) * C_V], ones_rows], axis=0)


def _premix(h, lw, tabs):
    B, S, D = h.shape
    Tt = TOKEN_TILE
    nt = S // Tt
    tile = lambda b, i: (b, i, 0)
    const2 = lambda b, i: (0, 0)
    headmajor = lambda b, i: (b, 0, i, 0)
    headmajor_t = lambda b, i: (b, 0, 0, i)
    vt_map = lambda b, i: (b, 0, i, 0, 0)
    tab_map = lambda b, i: (i, 0)
    in_specs = [
        pl.BlockSpec((1, Tt, D), tile),
        pl.BlockSpec((1, D), const2),
        pl.BlockSpec((D, IN_COLS_PACKED), const2),
        pl.BlockSpec((_QK_A, _QK_A), const2),
        pl.BlockSpec((1, _QK_A), const2),
        pl.BlockSpec((1, _QK_A), const2),
        pl.BlockSpec((Tt, LANES), tab_map),
        pl.BlockSpec((Tt, LANES), tab_map),
        pl.BlockSpec((1, C_Q_RANK), const2),
        pl.BlockSpec((C_Q_RANK, C_HEADS * C_QK_PAD), const2),
        pl.BlockSpec((1, C_KV_RANK), const2),
        pl.BlockSpec((C_KV_RANK, C_HEADS * C_QK_PAD + C_WIDTH), const2),
        pl.BlockSpec((Tt, LANES), tab_map),
        pl.BlockSpec((Tt, LANES), tab_map),
    ]
    out_shape = [
        jax.ShapeDtypeStruct((B, A_KV_HEADS, HEAD_DIM, S * (A_HEADS // A_KV_HEADS)), BF16),
        jax.ShapeDtypeStruct((B, A_KV_HEADS, S, HEAD_DIM), BF16),
        jax.ShapeDtypeStruct((B, A_KV_HEADS, nt, HEAD_DIM + V_PAD, Tt), BF16),
        jax.ShapeDtypeStruct((B, S, B_WIDTH), F32),
        jax.ShapeDtypeStruct((B, C_HEADS, C_QK_PAD, S), BF16),
        jax.ShapeDtypeStruct((B, C_HEADS, S, C_QK_PAD), BF16),
        jax.ShapeDtypeStruct((B, C_HEADS, nt, C_V + V_PAD, Tt), BF16),
    ]
    out_specs = [
        pl.BlockSpec((1, A_KV_HEADS, HEAD_DIM, Tt * (A_HEADS // A_KV_HEADS)), headmajor_t),
        pl.BlockSpec((1, A_KV_HEADS, Tt, HEAD_DIM), headmajor),
        pl.BlockSpec((1, A_KV_HEADS, 1, HEAD_DIM + V_PAD, Tt), vt_map),
        pl.BlockSpec((1, Tt, B_WIDTH), tile),
        pl.BlockSpec((1, C_HEADS, C_QK_PAD, Tt), headmajor_t),
        pl.BlockSpec((1, C_HEADS, Tt, C_QK_PAD), headmajor),
        pl.BlockSpec((1, C_HEADS, 1, C_V + V_PAD, Tt), vt_map),
    ]
    kern = functools.partial(_premix_kernel, c_scale=(C_NOPE + C_ROPE) ** -0.5 * LOG2E)
    return pl.pallas_call(
        kern, grid=(B, nt), in_specs=in_specs, out_specs=out_specs, out_shape=out_shape,
        compiler_params=_params(("parallel", "parallel")), name="premix",
    )(h, lw["g_mix"], lw["w_in"], tabs["hsum"], lw["g_qk"], lw["g_qk_sw"], tabs["cos_a"], tabs["sin_a"],
      lw["g_cq"], lw["w_q_up"], lw["g_ckv"], lw["w_kv_up"], tabs["cos_c"], tabs["sin_c"])


def _attn_kernel(qt_ref, k_ref, vt_ref, o_ref, s_buf, p_buf, *, order, n_chunks, chunk, sub, dv, group):
    n_streams, _, width = qt_ref.shape[1:]
    rows = vt_ref.shape[3]
    nsub = chunk // sub
    n_pos = len(order)
    per_body = 1 + max(coff for _, coff in order)
    assert LOOKAHEAD < n_pos and n_chunks % per_body == 0

    def score_piece(pos, j, r):
        si = order[pos][0]
        start = pl.multiple_of(j * chunk + r * sub, sub)
        s = jnp.dot(k_ref[0, si, pl.ds(start, sub), :], qt_ref[0, si], preferred_element_type=F32)
        s_buf[pos, r * sub:(r + 1) * sub, :] = s
        return jnp.max(s, axis=0, keepdims=True)

    def prob_piece(pos, r, m_b):
        s = s_buf[pos, r * sub:(r + 1) * sub, :]
        p_buf[pos, r * sub:(r + 1) * sub, :] = jnp.exp2(s - m_b).astype(BF16)

    def all_scores(pos, j):
        cm = None
        for r in range(nsub):
            c = score_piece(pos, j, r)
            cm = c if cm is None else jnp.maximum(cm, c)
        return cm

    def body(b, carry):
        pending, state = carry
        colmax = dict(enumerate(pending))
        state = list(state)
        for i, (si, coff) in enumerate(order):
            m, acc = state[si]
            m_new = jnp.maximum(m, colmax[i])
            m_b = jnp.broadcast_to(m_new, (sub, width))
            tpos = (i + LOOKAHEAD) % n_pos
            tj = jnp.minimum((b + (i + LOOKAHEAD) // n_pos) * per_body + order[tpos][1], n_chunks - 1)
            cm = None
            for r in range(nsub):
                c = score_piece(tpos, tj, r)
                cm = c if cm is None else jnp.maximum(cm, c)
                prob_piece(i, r, m_b)
            colmax[i + LOOKAHEAD] = cm
            pv = jnp.dot(vt_ref[0, si, b * per_body + coff], p_buf[i], preferred_element_type=F32)
            state[si] = (m_new, jnp.exp2(m - m_new) * acc + pv)
        return tuple(colmax[n_pos + k] for k in range(LOOKAHEAD)), tuple(state)

    pending = tuple(all_scores(k, order[k][1]) for k in range(LOOKAHEAD))
    init = tuple((jnp.full((1, width), -jnp.inf, F32), jnp.zeros((rows, width), F32)) for _ in range(n_streams))
    _, final = lax.fori_loop(0, n_chunks // per_body, body, (pending, init))
    tq = width // group
    outs = []
    for _, acc in final:
        out_t = acc[:dv] * (1.0 / acc[dv:dv + 1])
        outs += [out_t[:, g * tq:(g + 1) * tq] for g in range(group)]
    o_ref[0] = jnp.concatenate(outs, axis=0).T


def _attention(qt, k, vt, *, tq, group, dv):
    B, Hkv, dk, _ = qt.shape
    S = k.shape[2]
    nk, rows, Tk = vt.shape[2:]
    n_streams = ATTN_STREAMS
    width = group * tq
    per_body = 2
    order = tuple((si, coff) for coff in range(per_body) for si in range(n_streams))
    kern = functools.partial(_attn_kernel, order=order, n_chunks=nk, chunk=Tk, sub=KEY_SUB, dv=dv, group=group)
    return pl.pallas_call(
        kern, grid=(B, Hkv // n_streams, S // tq),
        in_specs=[
            pl.BlockSpec((1, n_streams, dk, width), lambda b, g, i: (b, g, 0, i)),
            pl.BlockSpec((1, n_streams, S, dk), lambda b, g, i: (b, g, 0, 0)),
            pl.BlockSpec((1, n_streams, nk, rows, Tk), lambda b, g, i: (b, g, 0, 0, 0)),
        ],
        out_specs=pl.BlockSpec((1, tq, n_streams * group * dv), lambda b, g, i: (b, i, g)),
        out_shape=jax.ShapeDtypeStruct((B, S, Hkv * group * dv), F32),
        scratch_shapes=[pltpu.VMEM((len(order), Tk, width), F32), pltpu.VMEM((len(order), Tk, width), BF16)],
        compiler_params=_params(("parallel", "parallel", "arbitrary")), name="attention",
    )(qt, k, vt)


def _route(logits):
    tt = logits.shape[1]
    le = logits[:N_EXPERTS]
    lg = logits[N_EXPERTS:]
    row8 = lax.broadcasted_iota(jnp.int32, (ROUTER_ROWS - N_EXPERTS, tt), 0)
    lg = jnp.where(row8 < MOE_GROUPS, lg, -jnp.inf)
    gmax = jnp.max(lg, axis=0, keepdims=True)
    gsel = jnp.min(jnp.where(lg == gmax, row8, MOE_GROUPS), axis=0, keepdims=True)
    gprob = 1.0 / jnp.sum(jnp.exp(lg - gmax), axis=0, keepdims=True)
    row = lax.broadcasted_iota(jnp.int32, (N_EXPERTS, tt), 0)
    lm = jnp.where((row // EXPERTS_PER_GROUP) == gsel, le, -jnp.inf)
    m1 = jnp.max(lm, axis=0, keepdims=True)
    i1 = jnp.min(jnp.where(lm == m1, row, N_EXPERTS), axis=0, keepdims=True)
    lm2 = jnp.where(row == i1, -jnp.inf, lm)
    m2 = jnp.max(lm2, axis=0, keepdims=True)
    i2 = jnp.min(jnp.where(lm2 == m2, row, N_EXPERTS), axis=0, keepdims=True)
    r = jnp.exp(m2 - m1)
    w1 = gprob / (1.0 + r)
    w2 = w1 * r
    return i1, i2, w1, w2


def _dispatch_plan(i1, i2, w1, w2, before_ref, lower_ref):
    tt = i1.shape[1]
    row = lax.broadcasted_iota(jnp.int32, (N_EXPERTS, tt), 0)
    hit1 = row == i1
    hit2 = row == i2
    assign = jnp.where(hit1 | hit2, 1.0, 0.0)
    rank = jnp.dot(assign.astype(BF16), before_ref[...], preferred_element_type=F32)
    pos, offs, counts = [], [], []
    for u in range(tt // MOE_SUB):
        lanes = slice(u * MOE_SUB, (u + 1) * MOE_SUB)
        count = jnp.sum(assign[:, lanes], axis=1, keepdims=True)
        segs = jnp.floor((count + (SEG_ALIGN - 1.0)) * (1.0 / SEG_ALIGN))
        segs_b = jnp.broadcast_to(segs, (N_EXPERTS, LANES)).astype(BF16)
        off = SEG_ALIGN * jnp.dot(lower_ref[...], segs_b, preferred_element_type=F32)
        pos.append(off[:, :1] + rank[:, lanes])
        offs.append(off.astype(jnp.int32))
        counts.append(jnp.broadcast_to(count, (N_EXPERTS, LANES)).astype(jnp.int32))
    pos = jnp.concatenate(pos, axis=1)
    pos1 = jnp.sum(jnp.where(hit1, pos, 0.0), axis=0, keepdims=True)
    pos2 = jnp.sum(jnp.where(hit2, pos, 0.0), axis=0, keepdims=True)
    route = jnp.concatenate([pos1, pos2, w1, w2, jnp.zeros((4, tt), F32)], axis=0)
    return route, offs, counts


def _postmix_kernel(h_ref, oa_ref, oc_ref, ub_ref, prev_ref, next_ref, wpool_ref, spool_ref,
                    ga_ref, gb_ref, gc_ref, wout_ref, gffn_ref, wr_ref, before_ref, lower_ref,
                    h1_ref, m_ref, route_ref, offs_ref, counts_ref, ext_ref, *, seq_len):
    i = pl.program_id(1)
    nt = pl.num_programs(1)
    tt = h_ref.shape[1]
    half = B_WIDTH // 2

    u = ub_ref[0]
    ext_ref[POOL_HALO:POOL_HALO + tt, :] = u
    ext_ref[:POOL_HALO, :] = jnp.where(i > 0, prev_ref[0, 0], 0.0)
    ext_ref[POOL_HALO + tt:, :] = jnp.where(i < nt - 1, next_ref[0, 0], 0.0)

    def window(lo, hi, lanes):
        acc = None
        for d in range(lo, hi):
            piece = ext_ref[POOL_HALO + d:POOL_HALO + d + tt, lanes]
            acc = piece if acc is None else acc + piece
        return acc

    lo_lanes = slice(0, half)
    hi_lanes = slice(half, B_WIDTH)
    w2 = window(-1, 1, lo_lanes)
    w4 = w2 + window(-2, -1, lo_lanes) + window(1, 2, lo_lanes)
    w8 = window(-4, 4, hi_lanes)
    w16 = w8 + window(-8, -4, hi_lanes) + window(4, 8, hi_lanes)
    lane = lax.broadcasted_iota(jnp.int32, (tt, half), 1)
    first = lane < B_GROUP_DIM
    sums = jnp.concatenate([jnp.where(first, w2, w4), jnp.where(first, w8, w16)], axis=1)
    pos = i * tt + lax.broadcasted_iota(jnp.int32, (tt, B_WIDTH), 0)
    lane_b = lax.broadcasted_iota(jnp.int32, (tt, B_WIDTH), 1)
    hw = jnp.left_shift(1, lane_b // B_GROUP_DIM)
    cnt = jnp.minimum(pos + hw, seq_len) - jnp.maximum(pos - hw, 0)
    dlt = sums / cnt.astype(F32) - u
    ob = jnp.dot(dlt.astype(BF16), wpool_ref[...], preferred_element_type=F32) * spool_ref[...]

    merged = jnp.concatenate([
        (_rms(oa_ref[0]) * ga_ref[...]).astype(BF16),
        (_rms(ob) * gb_ref[...]).astype(BF16),
        (_rms(oc_ref[0]) * gc_ref[...]).astype(BF16)], axis=1)
    h1 = h_ref[0] + jnp.dot(merged, wout_ref[...], preferred_element_type=F32)
    h1_ref[0] = h1

    m = _rms(h1) * gffn_ref[...]
    m_hi = m.astype(BF16)
    m_ref[0] = m_hi
    m_lo = (m - m_hi.astype(F32)).astype(BF16)
    nt_dims = (((1,), (1,)), ((), ()))
    both = lax.dot_general(wr_ref[...], m_hi, nt_dims, preferred_element_type=F32)
    logits = (both[:ROUTER_ROWS] + both[ROUTER_ROWS:]
              + lax.dot_general(wr_ref[:ROUTER_ROWS, :], m_lo, nt_dims, preferred_element_type=F32))
    route, offs, counts = _dispatch_plan(*_route(logits), before_ref, lower_ref)
    for u in range(len(offs)):
        route_ref[0, 0, u] = route[:, u * MOE_SUB:(u + 1) * MOE_SUB]
        offs_ref[0, 0, u] = offs[u]
        counts_ref[0, 0, u] = counts[u]


def _postmix(h, oa, oc, ub, lw, tabs):
    B, S, D = h.shape
    Tt = min(MOE_TILE, S)
    assert Tt % MOE_SUB == 0
    nt = S // Tt
    rows = Tt // POOL_HALO
    ub_rows = ub.reshape(B, S // POOL_HALO, POOL_HALO, B_WIDTH)
    tile = lambda b, i: (b, i, 0)
    const2 = lambda b, i: (0, 0)
    in_specs = [
        pl.BlockSpec((1, Tt, D), tile),
        pl.BlockSpec((1, Tt, A_WIDTH), tile),
        pl.BlockSpec((1, Tt, C_WIDTH), tile),
        pl.BlockSpec((1, Tt, B_WIDTH), tile),
        pl.BlockSpec((1, 1, POOL_HALO, B_WIDTH), lambda b, i: (b, jnp.maximum(i * rows - 1, 0), 0, 0)),
        pl.BlockSpec((1, 1, POOL_HALO, B_WIDTH),
                     lambda b, i: (b, jnp.minimum((i + 1) * rows, S // POOL_HALO - 1), 0, 0)),
        pl.BlockSpec((B_WIDTH, B_WIDTH), const2),
        pl.BlockSpec((1, B_WIDTH), const2),
        pl.BlockSpec((1, A_WIDTH), const2),
        pl.BlockSpec((1, B_WIDTH), const2),
        pl.BlockSpec((1, C_WIDTH), const2),
        pl.BlockSpec((D, D), const2),
        pl.BlockSpec((1, D), const2),
        pl.BlockSpec((2 * ROUTER_ROWS, D), const2),
        pl.BlockSpec((Tt, Tt), const2),
        pl.BlockSpec((N_EXPERTS, N_EXPERTS), const2),
    ]
    per_tile = lambda b, i: (b, i, 0, 0, 0)
    n_sub = Tt // MOE_SUB
    out_shape = [
        jax.ShapeDtypeStruct((B, S, D), F32),
        jax.ShapeDtypeStruct((B, S, D), BF16),
        jax.ShapeDtypeStruct((B, nt, n_sub, 8, MOE_SUB), F32),
        jax.ShapeDtypeStruct((B, nt, n_sub, N_EXPERTS, LANES), jnp.int32),
        jax.ShapeDtypeStruct((B, nt, n_sub, N_EXPERTS, LANES), jnp.int32),
    ]
    out_specs = [
        pl.BlockSpec((1, Tt, D), tile),
        pl.BlockSpec((1, Tt, D), tile),
        pl.BlockSpec((1, 1, n_sub, 8, MOE_SUB), per_tile),
        pl.BlockSpec((1, 1, n_sub, N_EXPERTS, LANES), per_tile),
        pl.BlockSpec((1, 1, n_sub, N_EXPERTS, LANES), per_tile),
    ]
    return pl.pallas_call(
        functools.partial(_postmix_kernel, seq_len=S), grid=(B, nt),
        in_specs=in_specs, out_specs=out_specs, out_shape=out_shape,
        scratch_shapes=[pltpu.VMEM((Tt + 2 * POOL_HALO, B_WIDTH), F32)],
        compiler_params=_params(("parallel", "parallel")), name="postmix",
    )(h, oa, oc, ub, ub_rows, ub_rows, lw["w_pool"], lw["s_pool"], lw["g_out_a"], lw["g_out_b"],
      lw["g_out_c"], lw["w_out"], lw["g_ffn"], lw["w_router"],
      tabs["before"], tabs["lower"])


def _moe_kernel(plan_ref, m_ref, route_ref, wgu_ref, wd_ref, o_ref, perm_ref, xy_ref, ws_ref, *, n_rows):
    g = pl.program_id(0)
    e = pl.program_id(1)
    n_tiles, _, tm = route_ref.shape
    d = m_ref.shape[1]

    @pl.when(e == 0)
    def _():
        for k in range(n_tiles):
            route = route_ref[k]
            pos1 = route[0:1].astype(jnp.int32)
            pos2 = route[1:2].astype(jnp.int32)
            w1 = route[2:3]
            w2 = route[3:4]
            m = m_ref[k * tm:(k + 1) * tm, :]

            def strip(i, _, k=k, pos1=pos1, pos2=pos2, w1=w1, w2=w2, m=m):
                r0 = pl.multiple_of(i * DISPATCH_STRIP, DISPATCH_STRIP)
                rows = r0 + lax.broadcasted_iota(jnp.int32, (DISPATCH_STRIP, tm), 0)
                hit1 = rows == pos1
                hit2 = rows == pos2
                perm = jnp.where(hit1 | hit2, 1.0, 0.0).astype(BF16)
                perm_ref[k, pl.ds(r0, DISPATCH_STRIP), :] = perm
                ws_ref[k, pl.ds(r0, DISPATCH_STRIP), :] = jnp.sum(
                    jnp.where(hit1, w1, jnp.where(hit2, w2, 0.0)), axis=1, keepdims=True)
                xy_ref[k, pl.ds(r0, DISPATCH_STRIP), :] = jnp.dot(
                    perm, m, preferred_element_type=F32).astype(BF16)
                return 0

            lax.fori_loop(0, n_rows // DISPATCH_STRIP, strip, 0, unroll=True)
            xy_ref[k, n_rows:, :] = jnp.zeros((EXPERT_BLOCK, d), BF16)
            ws_ref[k, n_rows:, :] = jnp.zeros((EXPERT_BLOCK, 1), F32)

    n_exp = wgu_ref.shape[0]
    plans = [[(plan_ref[g * n_tiles + k, e * n_exp + j], plan_ref[g * n_tiles + k, N_EXPERTS + e * n_exp + j])
              for k in range(n_tiles)] for j in range(n_exp)]

    def blocks(bi):
        starts = [[pl.multiple_of(jnp.minimum(off + bi * EXPERT_BLOCK, n_rows), SEG_ALIGN) for off, _ in pj]
                  for pj in plans]
        xs = [[xy_ref[k, pl.ds(r0, EXPERT_BLOCK), :] for k, r0 in enumerate(sj)] for sj in starts]
        ws = [[ws_ref[k, pl.ds(r0, EXPERT_BLOCK), :] for k, r0 in enumerate(sj)] for sj in starts]
        gus = [jnp.dot(jnp.concatenate(xs[j], axis=0), wgu_ref[j], preferred_element_type=F32)
               for j in range(n_exp)]
        hids = [gu[:, :D_EXPERT] * jax.nn.sigmoid(gu[:, :D_EXPERT]) * gu[:, D_EXPERT:]
                * jnp.concatenate(ws[j], axis=0) for j, gu in enumerate(gus)]
        ys = [jnp.dot(hid.astype(BF16), wd_ref[j], preferred_element_type=F32).astype(BF16)
              for j, hid in enumerate(hids)]
        row = lax.broadcasted_iota(jnp.int32, (EXPERT_BLOCK, 1), 0)
        for j in range(n_exp):
            for k, ((_, cnt), r0, x) in enumerate(zip(plans[j], starts[j], xs[j])):
                in_segment = row < cnt - bi * EXPERT_BLOCK
                xy_ref[k, pl.ds(r0, EXPERT_BLOCK), :] = jnp.where(
                    in_segment, ys[j][k * EXPERT_BLOCK:(k + 1) * EXPERT_BLOCK], x)

    blocks(0)
    longest = functools.reduce(jnp.maximum, [cnt for pj in plans for _, cnt in pj])
    n_blocks = lax.shift_right_logical(longest + (EXPERT_BLOCK - 1), EXPERT_BLOCK.bit_length() - 1)

    def more(bi, _):
        blocks(bi)
        return 0

    lax.fori_loop(1, n_blocks, more, 0)

    @pl.when(e == pl.num_programs(1) - 1)
    def _():
        tn_dims = (((0,), (0,)), ((), ()))
        for k in range(n_tiles):
            y = lax.dot_general(perm_ref[k], xy_ref[k, :n_rows, :], tn_dims, preferred_element_type=F32)
            o_ref[k * tm:(k + 1) * tm, :] = y.astype(BF16)


def _moe(m, route, plan, lw):
    T, D = m.shape
    n_all, _, Tm = route.shape
    group = math.gcd(MOE_TILE_GROUP, n_all)
    n_rows = 2 * Tm + N_EXPERTS * SEG_ALIGN
    assert n_rows % DISPATCH_STRIP == 0
    tile = lambda g, e, plan: (g, 0)
    grid_spec = pltpu.PrefetchScalarGridSpec(
        num_scalar_prefetch=1, grid=(n_all // group, N_EXPERTS // EXPERTS_PER_STEP),
        in_specs=[
            pl.BlockSpec((group * Tm, D), tile),
            pl.BlockSpec((group, 8, Tm), lambda g, e, plan: (g, 0, 0)),
            pl.BlockSpec((EXPERTS_PER_STEP, D, 2 * D_EXPERT), lambda g, e, plan: (e, 0, 0)),
            pl.BlockSpec((EXPERTS_PER_STEP, D_EXPERT, D), lambda g, e, plan: (e, 0, 0)),
        ],
        out_specs=pl.BlockSpec((group * Tm, D), tile),
        scratch_shapes=[
            pltpu.VMEM((group, n_rows, Tm), BF16),
            pltpu.VMEM((group, n_rows + EXPERT_BLOCK, D), BF16),
            pltpu.VMEM((group, n_rows + EXPERT_BLOCK, 1), F32),
        ],
    )
    return pl.pallas_call(
        functools.partial(_moe_kernel, n_rows=n_rows), grid_spec=grid_spec,
        out_shape=jax.ShapeDtypeStruct((T, D), BF16),
        compiler_params=_params(("parallel", "arbitrary")), name="moe",
    )(plan, m, route, lw["w_gate_up"], lw["w_down"])


def _ple_kernel(h1_ref, y_ref, p_ref, gple_ref, wgate_ref, wproj_ref, gfinal_ref, o_ref, *, final):
    h = h1_ref[0] + y_ref[0].astype(F32)
    r = (_rms(h) * gple_ref[...]).astype(BF16)
    gate = jax.nn.sigmoid(jnp.dot(r, wgate_ref[...], preferred_element_type=F32))
    emb = jnp.dot(p_ref[0, 0].astype(BF16), wproj_ref[...], preferred_element_type=F32)
    out = h + gate * emb
    if final:
        out = _rms(out) * gfinal_ref[...]
    o_ref[0] = out


def _ple(h1, y, p, layer, lw, g_final, final):
    B, S, D = h1.shape
    Tt = TOKEN_TILE
    tile = lambda b, i: (b, i, 0)
    const2 = lambda b, i: (0, 0)
    return pl.pallas_call(
        functools.partial(_ple_kernel, final=final), grid=(B, S // Tt),
        in_specs=[
            pl.BlockSpec((1, Tt, D), tile),
            pl.BlockSpec((1, Tt, D), tile),
            pl.BlockSpec((1, 1, Tt, PLE_DIM), lambda b, i: (layer, b, i, 0)),
            pl.BlockSpec((1, D), const2),
            pl.BlockSpec((D, D), const2),
            pl.BlockSpec((PLE_DIM, D), const2),
            pl.BlockSpec((1, D), const2),
        ],
        out_specs=pl.BlockSpec((1, Tt, D), tile),
        out_shape=jax.ShapeDtypeStruct((B, S, D), F32),
        compiler_params=_params(("parallel", "parallel")), name="ple",
    )(h1, y, p, lw["g_ple"], lw["w_ple_gate"], lw["w_ple_proj"], g_final)


def _rope_tables(seq_len):
    rows_n = seq_len // GRID_W
    row = jnp.repeat(jnp.arange(rows_n), GRID_W).astype(F32)[:, None]
    col = jnp.tile(jnp.arange(GRID_W), rows_n).astype(F32)[:, None]

    def axial(rot_dim):
        n_freq = rot_dim // 4
        inv = ROPE_THETA ** (-jnp.arange(n_freq, dtype=F32) / n_freq)
        ar, ac = row * inv, col * inv
        cos = jnp.concatenate([jnp.cos(ar), jnp.cos(ar), jnp.cos(ac), jnp.cos(ac)], axis=1)
        sin = jnp.concatenate([-jnp.sin(ar), jnp.sin(ar), -jnp.sin(ac), jnp.sin(ac)], axis=1)
        return cos, sin

    cos_a, sin_a = axial(HEAD_DIM)
    cos_r, sin_r = axial(C_ROPE)
    ones = jnp.ones((seq_len, C_NOPE), F32)
    zeros = jnp.zeros((seq_len, C_NOPE), F32)
    pad = jnp.zeros((seq_len, C_QK_PAD - C_NOPE - C_ROPE), F32)
    return dict(
        cos_a=jnp.concatenate([cos_a, cos_a], axis=1), sin_a=jnp.concatenate([sin_a, sin_a], axis=1),
        cos_c=jnp.concatenate([ones, cos_r, pad], axis=1), sin_c=jnp.concatenate([zeros, sin_r, pad], axis=1),
    )


def _pack_weights(w):
    L = DEPTH
    w_in = w["w_in"]
    def with_swapped(rope):
        parts = rope.reshape(rope.shape[:-1] + (2, 2, C_ROPE // 4))
        return jnp.concatenate([rope, parts[..., ::-1, :].reshape(rope.shape)], axis=-1)

    kr_pad = jnp.pad(with_swapped(w_in[:, :, _KR0:]), ((0, 0), (0, 0), (C_NOPE, 0)))
    q_scale = jnp.full((A_WIDTH,), HEAD_DIM ** -0.5 * LOG2E, F32)
    k_scale = jnp.ones((A_KV_HEADS * HEAD_DIM,), F32)
    g_qk = jnp.concatenate([jnp.tile(w["g_qa"], (1, A_HEADS)), jnp.tile(w["g_ka"], (1, A_KV_HEADS))], axis=1)
    g_qk = g_qk * jnp.concatenate([q_scale, k_scale])[None, :]

    def halves_swapped(cols):
        parts = cols.reshape(cols.shape[:-1] + (_QK_A // (HEAD_DIM // 2), 2, HEAD_DIM // 4))
        return parts[..., ::-1, :].reshape(cols.shape)
    w_q_up = w["w_q_up"].reshape(L, C_Q_RANK, C_HEADS, C_NOPE + C_ROPE)
    w_q_up = jnp.concatenate([w_q_up[..., :C_NOPE], with_swapped(w_q_up[..., C_NOPE:])], axis=-1)
    kv = w["w_kv_up"].reshape(L, C_KV_RANK, C_HEADS, C_NOPE + C_V)
    k_nope = jnp.pad(kv[..., :C_NOPE], ((0, 0), (0, 0), (0, 0), (0, C_QK_PAD - C_NOPE)))
    w_kv_up = jnp.concatenate([k_nope.reshape(L, C_KV_RANK, C_HEADS * C_QK_PAD),
                               kv[..., C_NOPE:].reshape(L, C_KV_RANK, C_WIDTH)], axis=2)
    eye = jnp.eye(B_GROUPS, dtype=F32)
    w_pool = (w["w_pool"][:, :, :, None, :] * eye[None, :, None, :, None]).reshape(L, B_WIDTH, B_WIDTH)
    w_router = jnp.concatenate([w["w_router_expert"], w["w_router_group"]], axis=2)
    w_router = jnp.pad(jnp.swapaxes(w_router, 1, 2), ((0, 0), (0, ROUTER_ROWS - N_EXPERTS - MOE_GROUPS), (0, 0)))
    w_router_hi = w_router.astype(BF16)
    w_router_lo = (w_router - w_router_hi.astype(F32)).astype(BF16)
    row = lambda a: a[:, None, :]
    return dict(
        g_mix=row(w["g_mix"]),
        w_in=jnp.concatenate([w_in[:, :, :_KR0], kr_pad, halves_swapped(w_in[:, :, :_QK_A])], axis=2).astype(BF16),
        g_qk=row(g_qk), g_qk_sw=row(halves_swapped(g_qk)),
        g_cq=row(w["g_cq"]), w_q_up=w_q_up.reshape(L, C_Q_RANK, C_HEADS * C_QK_PAD).astype(BF16),
        g_ckv=row(w["g_ckv"]), w_kv_up=w_kv_up.astype(BF16),
        w_pool=w_pool.astype(BF16), s_pool=row(w["s_pool"]),
        g_out_a=row(w["g_out_a"]), g_out_b=row(w["g_out_b"]), g_out_c=row(w["g_out_c"]),
        w_out=w["w_out"].astype(BF16), g_ffn=row(w["g_ffn"]),
        w_router=jnp.concatenate([w_router_hi, w_router_lo], axis=1),
        w_gate_up=jnp.concatenate([w["w_gate"], w["w_up"]], axis=3).astype(BF16),
        w_down=w["w_down"].astype(BF16),
        g_ple=row(w["g_ple"]), w_ple_gate=w["w_ple_gate"].astype(BF16), w_ple_proj=w["w_ple_proj"].astype(BF16),
    )


def _trunk(x, p, packed, g_final):
    B, S, D = x.shape
    tabs = _rope_tables(S)
    hs = np.kron(np.eye(_QK_A // HEAD_DIM, dtype=np.float32), np.ones((HEAD_DIM, HEAD_DIM), np.float32))
    tabs["hsum"] = jnp.asarray(hs, BF16)
    tm = min(MOE_TILE, S)
    same_sub = np.kron(np.eye(tm // MOE_SUB, dtype=np.float32), np.ones((MOE_SUB, MOE_SUB), np.float32))
    tabs["before"] = jnp.asarray(np.triu(same_sub, 1), BF16)
    tabs["lower"] = jnp.asarray(np.tril(np.ones((N_EXPERTS, N_EXPERTS), np.float32), -1), BF16)
    g_final = g_final[None, :]
    h = x
    for i in range(DEPTH):
        lw = {name: a[i] for name, a in packed.items()}
        qat, ka, vat, ub, qct, kc, vct = _premix(h, lw, tabs)
        oa = _attention(qat, ka, vat, tq=Q_TILE_GQA, group=A_HEADS // A_KV_HEADS, dv=HEAD_DIM)
        oc = _attention(qct, kc, vct, tq=Q_TILE_MLA, group=1, dv=C_V)
        h1, m, route, offs, counts = _postmix(h, oa, oc, ub, lw, tabs)
        plan = jnp.concatenate([offs[..., 0], counts[..., 0]], axis=-1).reshape(-1, 2 * N_EXPERTS)
        y = _moe(m.reshape(B * S, D), route.reshape(-1, 8, MOE_SUB), plan, lw)
        h = _ple(h1, y.reshape(B, S, D), p, i, lw, g_final, final=(i == DEPTH - 1))
    return h


def kernel(x_prompt, x_sample, p_prompt, p_sample, g_mix, w_in, g_qa, g_ka, w_pool, s_pool, g_cq, w_q_up, g_ckv, w_kv_up, g_out_a, g_out_b, g_out_c, w_out, g_ffn, w_router_group, w_router_expert, w_gate, w_up, w_down, g_ple, w_ple_gate, w_ple_proj, g_final):
    weights = dict(g_mix=g_mix, w_in=w_in, g_qa=g_qa, g_ka=g_ka, w_pool=w_pool, s_pool=s_pool, g_cq=g_cq,
                   w_q_up=w_q_up, g_ckv=g_ckv, w_kv_up=w_kv_up, g_out_a=g_out_a, g_out_b=g_out_b,
                   g_out_c=g_out_c, w_out=w_out, g_ffn=g_ffn, w_router_group=w_router_group,
                   w_router_expert=w_router_expert, w_gate=w_gate, w_up=w_up, w_down=w_down, g_ple=g_ple,
                   w_ple_gate=w_ple_gate, w_ple_proj=w_ple_proj)
    packed = _pack_weights(weights)
    return (_trunk(x_prompt, p_prompt, packed, g_final), _trunk(x_sample, p_sample, packed, g_final))
```

```python
import functools
import math

import numpy as np
import jax
import jax.numpy as jnp
from jax import lax
from jax.experimental import pallas as pl
from jax.experimental.pallas import tpu as pltpu

F32 = jnp.float32
BF16 = jnp.bfloat16

D_MODEL = 1024
DEPTH = 4
GRID_W = 64
PLE_DIM = 256
HEAD_DIM = 64
ROPE_THETA = 10000.0
EPS = 1e-6
A_HEADS = 6
A_KV_HEADS = 2
A_WIDTH = A_HEADS * HEAD_DIM
B_GROUPS = 4
B_GROUP_DIM = 64
B_WINDOWS = (2, 4, 8, 16)
B_WIDTH = B_GROUPS * B_GROUP_DIM
C_HEADS = 6
C_NOPE = 64
C_ROPE = 32
C_V = 64
C_Q_RANK = 256
C_KV_RANK = 128
C_WIDTH = C_HEADS * C_V
C_QK_PAD = 128
MOE_GROUPS = 4
EXPERTS_PER_GROUP = 8
N_EXPERTS = MOE_GROUPS * EXPERTS_PER_GROUP
D_EXPERT = 256
ROUTER_ROWS = 48
LOG2E = math.log2(math.e)

LANES = 128
VMEM_BYTES = 64 * 1024 * 1024
VMEM_LIMIT_BYTES = VMEM_BYTES - 4 * 1024 * 1024

TOKEN_TILE = 512
Q_TILE_GQA = 256
Q_TILE_MLA = 512
ATTN_STREAMS = 2
KEY_SUB = 512
LOOKAHEAD = 2
MOE_TILE = 1024
MOE_SUB = 256
SEG_ALIGN = 16
EXPERT_BLOCK = 32
MOE_TILE_GROUP = 8
EXPERTS_PER_STEP = 4
DISPATCH_STRIP = 256
V_PAD = 16
POOL_HALO = 8

_QK_A = A_WIDTH + A_KV_HEADS * HEAD_DIM
_V_A0 = _QK_A
_U_B0 = _V_A0 + A_KV_HEADS * HEAD_DIM
_CQ0 = _U_B0 + B_WIDTH
_CKV0 = _CQ0 + C_Q_RANK
_KR0 = _CKV0 + C_KV_RANK
_QK_SW0 = _KR0 + C_QK_PAD
IN_COLS_PACKED = _QK_SW0 + _QK_A


def _params(sem):
    return pltpu.CompilerParams(dimension_semantics=sem, vmem_limit_bytes=VMEM_LIMIT_BYTES)


def _rms(x):
    return x * lax.rsqrt(jnp.mean(x * x, axis=-1, keepdims=True) + EPS)


def _rope_partner(x):
    return pltpu.roll(x, x.shape[-1] - C_ROPE, x.ndim - 1)


def _premix_kernel(h_ref, g_mix_ref, w_in_ref, hsum_ref, g_qk_ref, g_qk_sw_ref, cosa_ref, sina_ref,
                   g_cq_ref, w_qup_ref, g_ckv_ref, w_kvup_ref, cosc_ref, sinc_ref,
                   qat_ref, ka_ref, vat_ref, ub_ref, qct_ref, kc_ref, vct_ref, *, c_scale):
    a = _rms(h_ref[0]) * g_mix_ref[...]
    z = jnp.dot(a.astype(BF16), w_in_ref[...], preferred_element_type=F32)

    qk = z[:, :_QK_A]
    qk_sw = z[:, _QK_SW0:]
    ss = jnp.dot((qk * qk).astype(BF16), hsum_ref[...], preferred_element_type=F32)
    reps = _QK_A // LANES
    cos_a = jnp.concatenate([cosa_ref[...]] * reps, axis=1) * g_qk_ref[...]
    sin_a = jnp.concatenate([sina_ref[...]] * reps, axis=1) * g_qk_sw_ref[...]
    qk = lax.rsqrt(ss * (1.0 / HEAD_DIM) + EPS) * (qk * cos_a + qk_sw * sin_a)
    qat = qk[:, :A_WIDTH].T.astype(BF16)
    tt = qat.shape[1]
    a_group = A_HEADS // A_KV_HEADS
    for hh in range(A_HEADS):
        for qi in range(tt // Q_TILE_GQA):
            col = (qi * a_group + hh % a_group) * Q_TILE_GQA
            qat_ref[0, hh // a_group, :, col:col + Q_TILE_GQA] = qat[
                hh * HEAD_DIM:(hh + 1) * HEAD_DIM, qi * Q_TILE_GQA:(qi + 1) * Q_TILE_GQA]
    ka = qk[:, A_WIDTH:].astype(BF16)
    for hh in range(A_KV_HEADS):
        ka_ref[0, hh] = ka[:, hh * HEAD_DIM:(hh + 1) * HEAD_DIM]
    ones_rows = (lax.broadcasted_iota(jnp.int32, (V_PAD, tt), 0) == 0).astype(BF16)
    vt = z[:, _V_A0:_U_B0].T.astype(BF16)
    for hh in range(A_KV_HEADS):
        vat_ref[0, hh, 0] = jnp.concatenate([vt[hh * HEAD_DIM:(hh + 1) * HEAD_DIM], ones_rows], axis=0)

    ub_ref[0] = z[:, _U_B0:_CQ0]

    cqn = _rms(z[:, _CQ0:_CKV0]) * g_cq_ref[...]
    qc = jnp.dot(cqn.astype(BF16), w_qup_ref[...], preferred_element_type=F32)
    cos_c = cosc_ref[...]
    sin_c = sinc_ref[...]
    cos_q = jnp.concatenate([cos_c * c_scale] * C_HEADS, axis=1)
    sin_q = jnp.concatenate([sin_c * c_scale] * C_HEADS, axis=1)
    qct = (qc * cos_q + _rope_partner(qc) * sin_q).T.astype(BF16)
    ckvn = _rms(z[:, _CKV0:_KR0]) * g_ckv_ref[...]
    kv = jnp.dot(ckvn.astype(BF16), w_kvup_ref[...], preferred_element_type=F32)
    kr = z[:, _KR0:_QK_SW0]
    kr = kr * cos_c + _rope_partner(kr) * sin_c
    for hh in range(C_HEADS):
        qct_ref[0, hh] = qct[hh * C_QK_PAD:(hh + 1) * C_QK_PAD]
        kc_ref[0, hh] = (kv[:, hh * C_QK_PAD:(hh + 1) * C_QK_PAD] + kr).astype(BF16)
    vct = kv[:, C_HEADS * C_QK_PAD:].T.astype(BF16)
    for hh in range(C_HEADS):
        vct_ref[0, hh, 0] = jnp.concatenate([vct[hh * C_V:(hh + 1) * C_V], ones_rows], axis=0)


def _premix(h, lw, tabs):
    B, S, D = h.shape
    Tt = TOKEN_TILE
    nt = S // Tt
    tile = lambda b, i: (b, i, 0)
    const2 = lambda b, i: (0, 0)
    headmajor = lambda b, i: (b, 0, i, 0)
    headmajor_t = lambda b, i: (b, 0, 0, i)
    vt_map = lambda b, i: (b, 0, i, 0, 0)
    tab_map = lambda b, i: (i, 0)
    in_specs = [
        pl.BlockSpec((1, Tt, D), tile),
        pl.BlockSpec((1, D), const2),
        pl.BlockSpec((D, IN_COLS_PACKED), const2),
        pl.BlockSpec((_QK_A, _QK_A), const2),
        pl.BlockSpec((1, _QK_A), const2),
        pl.BlockSpec((1, _QK_A), const2),
        pl.BlockSpec((Tt, LANES), tab_map),
        pl.BlockSpec((Tt, LANES), tab_map),
        pl.BlockSpec((1, C_Q_RANK), const2),
        pl.BlockSpec((C_Q_RANK, C_HEADS * C_QK_PAD), const2),
        pl.BlockSpec((1, C_KV_RANK), const2),
        pl.BlockSpec((C_KV_RANK, C_HEADS * C_QK_PAD + C_WIDTH), const2),
        pl.BlockSpec((Tt, LANES), tab_map),
        pl.BlockSpec((Tt, LANES), tab_map),
    ]
    out_shape = [
        jax.ShapeDtypeStruct((B, A_KV_HEADS, HEAD_DIM, S * (A_HEADS // A_KV_HEADS)), BF16),
        jax.ShapeDtypeStruct((B, A_KV_HEADS, S, HEAD_DIM), BF16),
        jax.ShapeDtypeStruct((B, A_KV_HEADS, nt, HEAD_DIM + V_PAD, Tt), BF16),
        jax.ShapeDtypeStruct((B, S, B_WIDTH), F32),
        jax.ShapeDtypeStruct((B, C_HEADS, C_QK_PAD, S), BF16),
        jax.ShapeDtypeStruct((B, C_HEADS, S, C_QK_PAD), BF16),
        jax.ShapeDtypeStruct((B, C_HEADS, nt, C_V + V_PAD, Tt), BF16),
    ]
    out_specs = [
        pl.BlockSpec((1, A_KV_HEADS, HEAD_DIM, Tt * (A_HEADS // A_KV_HEADS)), headmajor_t),
        pl.BlockSpec((1, A_KV_HEADS, Tt, HEAD_DIM), headmajor),
        pl.BlockSpec((1, A_KV_HEADS, 1, HEAD_DIM + V_PAD, Tt), vt_map),
        pl.BlockSpec((1, Tt, B_WIDTH), tile),
        pl.BlockSpec((1, C_HEADS, C_QK_PAD, Tt), headmajor_t),
        pl.BlockSpec((1, C_HEADS, Tt, C_QK_PAD), headmajor),
        pl.BlockSpec((1, C_HEADS, 1, C_V + V_PAD, Tt), vt_map),
    ]
    kern = functools.partial(_premix_kernel, c_scale=(C_NOPE + C_ROPE) ** -0.5 * LOG2E)
    return pl.pallas_call(
        kern, grid=(B, nt), in_specs=in_specs, out_specs=out_specs, out_shape=out_shape,
        compiler_params=_params(("parallel", "parallel")), name="premix",
    )(h, lw["g_mix"], lw["w_in"], tabs["hsum"], lw["g_qk"], lw["g_qk_sw"], tabs["cos_a"], tabs["sin_a"],
      lw["g_cq"], lw["w_q_up"], lw["g_ckv"], lw["w_kv_up"], tabs["cos_c"], tabs["sin_c"])


def _attn_kernel(qt_ref, k_ref, vt_ref, o_ref, s_buf, p_buf, *, order, n_chunks, chunk, sub, dv, group):
    n_streams, _, width = qt_ref.shape[1:]
    rows = vt_ref.shape[3]
    nsub = chunk // sub
    n_pos = len(order)
    per_body = 1 + max(coff for _, coff in order)
    assert LOOKAHEAD < n_pos and n_chunks % per_body == 0

    def score_piece(pos, j, r):
        si = order[pos][0]
        start = pl.multiple_of(j * chunk + r * sub, sub)
        s = jnp.dot(k_ref[0, si, pl.ds(start, sub), :], qt_ref[0, si], preferred_element_type=F32)
        s_buf[pos, r * sub:(r + 1) * sub, :] = s
        return jnp.max(s, axis=0, keepdims=True)

    def prob_piece(pos, r, m_b):
        s = s_buf[pos, r * sub:(r + 1) * sub, :]
        p_buf[pos, r * sub:(r + 1) * sub, :] = jnp.exp2(s - m_b).astype(BF16)

    def all_scores(pos, j):
        cm = None
        for r in range(nsub):
            c = score_piece(pos, j, r)
            cm = c if cm is None else jnp.maximum(cm, c)
        return cm

    def body(b, carry):
        pending, state = carry
        colmax = dict(enumerate(pending))
        state = list(state)
        for i, (si, coff) in enumerate(order):
            m, acc = state[si]
            m_new = jnp.maximum(m, colmax[i])
            m_b = jnp.broadcast_to(m_new, (sub, width))
            tpos = (i + LOOKAHEAD) % n_pos
            tj = jnp.minimum((b + (i + LOOKAHEAD) // n_pos) * per_body + order[tpos][1], n_chunks - 1)
            cm = None
            for r in range(nsub):
                c = score_piece(tpos, tj, r)
                cm = c if cm is None else jnp.maximum(cm, c)
                prob_piece(i, r, m_b)
            colmax[i + LOOKAHEAD] = cm
            pv = jnp.dot(vt_ref[0, si, b * per_body + coff], p_buf[i], preferred_element_type=F32)
            state[si] = (m_new, jnp.exp2(m - m_new) * acc + pv)
        return tuple(colmax[n_pos + k] for k in range(LOOKAHEAD)), tuple(state)

    pending = tuple(all_scores(k, order[k][1]) for k in range(LOOKAHEAD))
    init = tuple((jnp.full((1, width), -jnp.inf, F32), jnp.zeros((rows, width), F32)) for _ in range(n_streams))
    _, final = lax.fori_loop(0, n_chunks // per_body, body, (pending, init))
    tq = width // group
    outs = []
    for _, acc in final:
        out_t = acc[:dv] * (1.0 / acc[dv:dv + 1])
        outs += [out_t[:, g * tq:(g + 1) * tq] for g in range(group)]
    o_ref[0] = jnp.concatenate(outs, axis=0).T


def _attention(qt, k, vt, *, tq, group, dv):
    B, Hkv, dk, _ = qt.shape
    S = k.shape[2]
    nk, rows, Tk = vt.shape[2:]
    n_streams = ATTN_STREAMS
    width = group * tq
    per_body = 2
    order = tuple((si, coff) for coff in range(per_body) for si in range(n_streams))
    kern = functools.partial(_attn_kernel, order=order, n_chunks=nk, chunk=Tk, sub=KEY_SUB, dv=dv, group=group)
    return pl.pallas_call(
        kern, grid=(B, Hkv // n_streams, S // tq),
        in_specs=[
            pl.BlockSpec((1, n_streams, dk, width), lambda b, g, i: (b, g, 0, i)),
            pl.BlockSpec((1, n_streams, S, dk), lambda b, g, i: (b, g, 0, 0)),
            pl.BlockSpec((1, n_streams, nk, rows, Tk), lambda b, g, i: (b, g, 0, 0, 0)),
        ],
        out_specs=pl.BlockSpec((1, tq, n_streams * group * dv), lambda b, g, i: (b, i, g)),
        out_shape=jax.ShapeDtypeStruct((B, S, Hkv * group * dv), F32),
        scratch_shapes=[pltpu.VMEM((len(order), Tk, width), F32), pltpu.VMEM((len(order), Tk, width), BF16)],
        compiler_params=_params(("parallel", "parallel", "arbitrary")), name="attention",
    )(qt, k, vt)


def _route(logits):
    tt = logits.shape[1]
    le = logits[:N_EXPERTS]
    lg = logits[N_EXPERTS:]
    row8 = lax.broadcasted_iota(jnp.int32, (ROUTER_ROWS - N_EXPERTS, tt), 0)
    lg = jnp.where(row8 < MOE_GROUPS, lg, -jnp.inf)
    gmax = jnp.max(lg, axis=0, keepdims=True)
    gsel = jnp.min(jnp.where(lg == gmax, row8, MOE_GROUPS), axis=0, keepdims=True)
    gprob = 1.0 / jnp.sum(jnp.exp(lg - gmax), axis=0, keepdims=True)
    row = lax.broadcasted_iota(jnp.int32, (N_EXPERTS, tt), 0)
    lm = jnp.where((row // EXPERTS_PER_GROUP) == gsel, le, -jnp.inf)
    m1 = jnp.max(lm, axis=0, keepdims=True)
    i1 = jnp.min(jnp.where(lm == m1, row, N_EXPERTS), axis=0, keepdims=True)
    lm2 = jnp.where(row == i1, -jnp.inf, lm)
    m2 = jnp.max(lm2, axis=0, keepdims=True)
    i2 = jnp.min(jnp.where(lm2 == m2, row, N_EXPERTS), axis=0, keepdims=True)
    r = jnp.exp(m2 - m1)
    w1 = gprob / (1.0 + r)
    w2 = w1 * r
    return i1, i2, w1, w2


def _dispatch_plan(i1, i2, w1, w2, before_ref, lower_ref):
    tt = i1.shape[1]
    row = lax.broadcasted_iota(jnp.int32, (N_EXPERTS, tt), 0)
    hit1 = row == i1
    hit2 = row == i2
    assign = jnp.where(hit1 | hit2, 1.0, 0.0)
    rank = jnp.dot(assign.astype(BF16), before_ref[...], preferred_element_type=F32)
    pos, offs, counts = [], [], []
    for u in range(tt // MOE_SUB):
        lanes = slice(u * MOE_SUB, (u + 1) * MOE_SUB)
        count = jnp.sum(assign[:, lanes], axis=1, keepdims=True)
        segs = jnp.floor((count + (SEG_ALIGN - 1.0)) * (1.0 / SEG_ALIGN))
        segs_b = jnp.broadcast_to(segs, (N_EXPERTS, LANES)).astype(BF16)
        off = SEG_ALIGN * jnp.dot(lower_ref[...], segs_b, preferred_element_type=F32)
        pos.append(off[:, :1] + rank[:, lanes])
        offs.append(off.astype(jnp.int32))
        counts.append(jnp.broadcast_to(count, (N_EXPERTS, LANES)).astype(jnp.int32))
    pos = jnp.concatenate(pos, axis=1)
    pos1 = jnp.sum(jnp.where(hit1, pos, 0.0), axis=0, keepdims=True)
    pos2 = jnp.sum(jnp.where(hit2, pos, 0.0), axis=0, keepdims=True)
    route = jnp.concatenate([pos1, pos2, w1, w2, jnp.zeros((4, tt), F32)], axis=0)
    return route, offs, counts


def _postmix_kernel(h_ref, oa_ref, oc_ref, ub_ref, prev_ref, next_ref, wpool_ref, spool_ref,
                    ga_ref, gb_ref, gc_ref, wout_ref, gffn_ref, wr_ref, before_ref, lower_ref,
                    h1_ref, m_ref, route_ref, offs_ref, counts_ref, ext_ref, *, seq_len):
    i = pl.program_id(1)
    nt = pl.num_programs(1)
    tt = h_ref.shape[1]
    half = B_WIDTH // 2

    u = ub_ref[0]
    ext_ref[POOL_HALO:POOL_HALO + tt, :] = u
    ext_ref[:POOL_HALO, :] = jnp.where(i > 0, prev_ref[0, 0], 0.0)
    ext_ref[POOL_HALO + tt:, :] = jnp.where(i < nt - 1, next_ref[0, 0], 0.0)

    def window(lo, hi, lanes):
        acc = None
        for d in range(lo, hi):
            piece = ext_ref[POOL_HALO + d:POOL_HALO + d + tt, lanes]
            acc = piece if acc is None else acc + piece
        return acc

    lo_lanes = slice(0, half)
    hi_lanes = slice(half, B_WIDTH)
    w2 = window(-1, 1, lo_lanes)
    w4 = w2 + window(-2, -1, lo_lanes) + window(1, 2, lo_lanes)
    w8 = window(-4, 4, hi_lanes)
    w16 = w8 + window(-8, -4, hi_lanes) + window(4, 8, hi_lanes)
    lane = lax.broadcasted_iota(jnp.int32, (tt, half), 1)
    first = lane < B_GROUP_DIM
    sums = jnp.concatenate([jnp.where(first, w2, w4), jnp.where(first, w8, w16)], axis=1)
    pos = i * tt + lax.broadcasted_iota(jnp.int32, (tt, B_WIDTH), 0)
    lane_b = lax.broadcasted_iota(jnp.int32, (tt, B_WIDTH), 1)
    hw = jnp.left_shift(1, lane_b // B_GROUP_DIM)
    cnt = jnp.minimum(pos + hw, seq_len) - jnp.maximum(pos - hw, 0)
    dlt = sums / cnt.astype(F32) - u
    ob = jnp.dot(dlt.astype(BF16), wpool_ref[...], preferred_element_type=F32) * spool_ref[...]

    merged = jnp.concatenate([
        (_rms(oa_ref[0]) * ga_ref[...]).astype(BF16),
        (_rms(ob) * gb_ref[...]).astype(BF16),
        (_rms(oc_ref[0]) * gc_ref[...]).astype(BF16)], axis=1)
    h1 = h_ref[0] + jnp.dot(merged, wout_ref[...], preferred_element_type=F32)
    h1_ref[0] = h1

    m = _rms(h1) * gffn_ref[...]
    m_hi = m.astype(BF16)
    m_ref[0] = m_hi
    m_lo = (m - m_hi.astype(F32)).astype(BF16)
    nt_dims = (((1,), (1,)), ((), ()))
    both = lax.dot_general(wr_ref[...], m_hi, nt_dims, preferred_element_type=F32)
    logits = (both[:ROUTER_ROWS] + both[ROUTER_ROWS:]
              + lax.dot_general(wr_ref[:ROUTER_ROWS, :], m_lo, nt_dims, preferred_element_type=F32))
    route, offs, counts = _dispatch_plan(*_route(logits), before_ref, lower_ref)
    for u in range(len(offs)):
        route_ref[0, 0, u] = route[:, u * MOE_SUB:(u + 1) * MOE_SUB]
        offs_ref[0, 0, u] = offs[u]
        counts_ref[0, 0, u] = counts[u]


def _postmix(h, oa, oc, ub, lw, tabs):
    B, S, D = h.shape
    Tt = min(MOE_TILE, S)
    assert Tt % MOE_SUB == 0
    nt = S // Tt
    rows = Tt // POOL_HALO
    ub_rows = ub.reshape(B, S // POOL_HALO, POOL_HALO, B_WIDTH)
    tile = lambda b, i: (b, i, 0)
    const2 = lambda b, i: (0, 0)
    in_specs = [
        pl.BlockSpec((1, Tt, D), tile),
        pl.BlockSpec((1, Tt, A_WIDTH), tile),
        pl.BlockSpec((1, Tt, C_WIDTH), tile),
        pl.BlockSpec((1, Tt, B_WIDTH), tile),
        pl.BlockSpec((1, 1, POOL_HALO, B_WIDTH), lambda b, i: (b, jnp.maximum(i * rows - 1, 0), 0, 0)),
        pl.BlockSpec((1, 1, POOL_HALO, B_WIDTH),
                     lambda b, i: (b, jnp.minimum((i + 1) * rows, S // POOL_HALO - 1), 0, 0)),
        pl.BlockSpec((B_WIDTH, B_WIDTH), const2),
        pl.BlockSpec((1, B_WIDTH), const2),
        pl.BlockSpec((1, A_WIDTH), const2),
        pl.BlockSpec((1, B_WIDTH), const2),
        pl.BlockSpec((1, C_WIDTH), const2),
        pl.BlockSpec((D, D), const2),
        pl.BlockSpec((1, D), const2),
        pl.BlockSpec((2 * ROUTER_ROWS, D), const2),
        pl.BlockSpec((Tt, Tt), const2),
        pl.BlockSpec((N_EXPERTS, N_EXPERTS), const2),
    ]
    per_tile = lambda b, i: (b, i, 0, 0, 0)
    n_sub = Tt // MOE_SUB
    out_shape = [
        jax.ShapeDtypeStruct((B, S, D), F32),
        jax.ShapeDtypeStruct((B, S, D), BF16),
        jax.ShapeDtypeStruct((B, nt, n_sub, 8, MOE_SUB), F32),
        jax.ShapeDtypeStruct((B, nt, n_sub, N_EXPERTS, LANES), jnp.int32),
        jax.ShapeDtypeStruct((B, nt, n_sub, N_EXPERTS, LANES), jnp.int32),
    ]
    out_specs = [
        pl.BlockSpec((1, Tt, D), tile),
        pl.BlockSpec((1, Tt, D), tile),
        pl.BlockSpec((1, 1, n_sub, 8, MOE_SUB), per_tile),
        pl.BlockSpec((1, 1, n_sub, N_EXPERTS, LANES), per_tile),
        pl.BlockSpec((1, 1, n_sub, N_EXPERTS, LANES), per_tile),
    ]
    return pl.pallas_call(
        functools.partial(_postmix_kernel, seq_len=S), grid=(B, nt),
        in_specs=in_specs, out_specs=out_specs, out_shape=out_shape,
        scratch_shapes=[pltpu.VMEM((Tt + 2 * POOL_HALO, B_WIDTH), F32)],
        compiler_params=_params(("parallel", "parallel")), name="postmix",
    )(h, oa, oc, ub, ub_rows, ub_rows, lw["w_pool"], lw["s_pool"], lw["g_out_a"], lw["g_out_b"],
      lw["g_out_c"], lw["w_out"], lw["g_ffn"], lw["w_router"],
      tabs["before"], tabs["lower"])


def _moe_kernel(plan_ref, m_ref, route_ref, wgu_ref, wd_ref, o_ref, perm_ref, xy_ref, ws_ref, *, n_rows):
    g = pl.program_id(0)
    e = pl.program_id(1)
    n_tiles, _, tm = route_ref.shape
    d = m_ref.shape[1]

    @pl.when(e == 0)
    def _():
        for k in range(n_tiles):
            route = route_ref[k]
            pos1 = route[0:1].astype(jnp.int32)
            pos2 = route[1:2].astype(jnp.int32)
            w1 = route[2:3]
            w2 = route[3:4]
            m = m_ref[k * tm:(k + 1) * tm, :]

            def strip(i, _, k=k, pos1=pos1, pos2=pos2, w1=w1, w2=w2, m=m):
                r0 = pl.multiple_of(i * DISPATCH_STRIP, DISPATCH_STRIP)
                rows = r0 + lax.broadcasted_iota(jnp.int32, (DISPATCH_STRIP, tm), 0)
                hit1 = rows == pos1
                hit2 = rows == pos2
                perm = jnp.where(hit1 | hit2, 1.0, 0.0).astype(BF16)
                perm_ref[k, pl.ds(r0, DISPATCH_STRIP), :] = perm
                ws_ref[k, pl.ds(r0, DISPATCH_STRIP), :] = jnp.sum(
                    jnp.where(hit1, w1, jnp.where(hit2, w2, 0.0)), axis=1, keepdims=True)
                xy_ref[k, pl.ds(r0, DISPATCH_STRIP), :] = jnp.dot(
                    perm, m, preferred_element_type=F32).astype(BF16)
                return 0

            lax.fori_loop(0, n_rows // DISPATCH_STRIP, strip, 0, unroll=True)
            xy_ref[k, n_rows:, :] = jnp.zeros((EXPERT_BLOCK, d), BF16)
            ws_ref[k, n_rows:, :] = jnp.zeros((EXPERT_BLOCK, 1), F32)

    n_exp = wgu_ref.shape[0]
    plans = [[(plan_ref[g * n_tiles + k, e * n_exp + j], plan_ref[g * n_tiles + k, N_EXPERTS + e * n_exp + j])
              for k in range(n_tiles)] for j in range(n_exp)]

    def blocks(bi):
        starts = [[pl.multiple_of(jnp.minimum(off + bi * EXPERT_BLOCK, n_rows), SEG_ALIGN) for off, _ in pj]
                  for pj in plans]
        xs = [[xy_ref[k, pl.ds(r0, EXPERT_BLOCK), :] for k, r0 in enumerate(sj)] for sj in starts]
        ws = [[ws_ref[k, pl.ds(r0, EXPERT_BLOCK), :] for k, r0 in enumerate(sj)] for sj in starts]
        gus = [jnp.dot(jnp.concatenate(xs[j], axis=0), wgu_ref[j], preferred_element_type=F32)
               for j in range(n_exp)]
        hids = [gu[:, :D_EXPERT] * jax.nn.sigmoid(gu[:, :D_EXPERT]) * gu[:, D_EXPERT:]
                * jnp.concatenate(ws[j], axis=0) for j, gu in enumerate(gus)]
        ys = [jnp.dot(hid.astype(BF16), wd_ref[j], preferred_element_type=F32).astype(BF16)
              for j, hid in enumerate(hids)]
        row = lax.broadcasted_iota(jnp.int32, (EXPERT_BLOCK, 1), 0)
        for j in range(n_exp):
            for k, ((_, cnt), r0, x) in enumerate(zip(plans[j], starts[j], xs[j])):
                in_segment = row < cnt - bi * EXPERT_BLOCK
                xy_ref[k, pl.ds(r0, EXPERT_BLOCK), :] = jnp.where(
                    in_segment, ys[j][k * EXPERT_BLOCK:(k + 1) * EXPERT_BLOCK], x)

    blocks(0)
    longest = functools.reduce(jnp.maximum, [cnt for pj in plans for _, cnt in pj])
    n_blocks = lax.shift_right_logical(longest + (EXPERT_BLOCK - 1), EXPERT_BLOCK.bit_length() - 1)

    def more(bi, _):
        blocks(bi)
        return 0

    lax.fori_loop(1, n_blocks, more, 0)

    @pl.when(e == pl.num_programs(1) - 1)
    def _():
        tn_dims = (((0,), (0,)), ((), ()))
        for k in range(n_tiles):
            y = lax.dot_general(perm_ref[k], xy_ref[k, :n_rows, :], tn_dims, preferred_element_type=F32)
            o_ref[k * tm:(k + 1) * tm, :] = y.astype(BF16)


def _moe(m, route, plan, lw):
    T, D = m.shape
    n_all, _, Tm = route.shape
    group = math.gcd(MOE_TILE_GROUP, n_all)
    n_rows = 2 * Tm + N_EXPERTS * SEG_ALIGN
    assert n_rows % DISPATCH_STRIP == 0
    tile = lambda g, e, plan: (g, 0)
    grid_spec = pltpu.PrefetchScalarGridSpec(
        num_scalar_prefetch=1, grid=(n_all // group, N_EXPERTS // EXPERTS_PER_STEP),
        in_specs=[
            pl.BlockSpec((group * Tm, D), tile),
            pl.BlockSpec((group, 8, Tm), lambda g, e, plan: (g, 0, 0)),
            pl.BlockSpec((EXPERTS_PER_STEP, D, 2 * D_EXPERT), lambda g, e, plan: (e, 0, 0)),
            pl.BlockSpec((EXPERTS_PER_STEP, D_EXPERT, D), lambda g, e, plan: (e, 0, 0)),
        ],
        out_specs=pl.BlockSpec((group * Tm, D), tile),
        scratch_shapes=[
            pltpu.VMEM((group, n_rows, Tm), BF16),
            pltpu.VMEM((group, n_rows + EXPERT_BLOCK, D), BF16),
            pltpu.VMEM((group, n_rows + EXPERT_BLOCK, 1), F32),
        ],
    )
    return pl.pallas_call(
        functools.partial(_moe_kernel, n_rows=n_rows), grid_spec=grid_spec,
        out_shape=jax.ShapeDtypeStruct((T, D), BF16),
        compiler_params=_params(("parallel", "arbitrary")), name="moe",
    )(plan, m, route, lw["w_gate_up"], lw["w_down"])


def _ple_kernel(h1_ref, y_ref, p_ref, gple_ref, wgate_ref, wproj_ref, gfinal_ref, o_ref, *, final):
    h = h1_ref[0] + y_ref[0].astype(F32)
    r = (_rms(h) * gple_ref[...]).astype(BF16)
    gate = jax.nn.sigmoid(jnp.dot(r, wgate_ref[...], preferred_element_type=F32))
    emb = jnp.dot(p_ref[0, 0].astype(BF16), wproj_ref[...], preferred_element_type=F32)
    out = h + gate * emb
    if final:
        out = _rms(out) * gfinal_ref[...]
    o_ref[0] = out


def _ple(h1, y, p, layer, lw, g_final, final):
    B, S, D = h1.shape
    Tt = TOKEN_TILE
    tile = lambda b, i: (b, i, 0)
    const2 = lambda b, i: (0, 0)
    return pl.pallas_call(
        functools.partial(_ple_kernel, final=final), grid=(B, S // Tt),
        in_specs=[
            pl.BlockSpec((1, Tt, D), tile),
            pl.BlockSpec((1, Tt, D), tile),
            pl.BlockSpec((1, 1, Tt, PLE_DIM), lambda b, i: (layer, b, i, 0)),
            pl.BlockSpec((1, D), const2),
            pl.BlockSpec((D, D), const2),
            pl.BlockSpec((PLE_DIM, D), const2),
            pl.BlockSpec((1, D), const2),
        ],
        out_specs=pl.BlockSpec((1, Tt, D), tile),
        out_shape=jax.ShapeDtypeStruct((B, S, D), F32),
        compiler_params=_params(("parallel", "parallel")), name="ple",
    )(h1, y, p, lw["g_ple"], lw["w_ple_gate"], lw["w_ple_proj"], g_final)


def _rope_tables(seq_len):
    rows_n = seq_len // GRID_W
    row = jnp.repeat(jnp.arange(rows_n), GRID_W).astype(F32)[:, None]
    col = jnp.tile(jnp.arange(GRID_W), rows_n).astype(F32)[:, None]

    def axial(rot_dim):
        n_freq = rot_dim // 4
        inv = ROPE_THETA ** (-jnp.arange(n_freq, dtype=F32) / n_freq)
        ar, ac = row * inv, col * inv
        cos = jnp.concatenate([jnp.cos(ar), jnp.cos(ar), jnp.cos(ac), jnp.cos(ac)], axis=1)
        sin = jnp.concatenate([-jnp.sin(ar), jnp.sin(ar), -jnp.sin(ac), jnp.sin(ac)], axis=1)
        return cos, sin

    cos_a, sin_a = axial(HEAD_DIM)
    cos_r, sin_r = axial(C_ROPE)
    ones = jnp.ones((seq_len, C_NOPE), F32)
    zeros = jnp.zeros((seq_len, C_NOPE), F32)
    pad = jnp.zeros((seq_len, C_QK_PAD - C_NOPE - C_ROPE), F32)
    return dict(
        cos_a=jnp.concatenate([cos_a, cos_a], axis=1), sin_a=jnp.concatenate([sin_a, sin_a], axis=1),
        cos_c=jnp.concatenate([ones, cos_r, pad], axis=1), sin_c=jnp.concatenate([zeros, sin_r, pad], axis=1),
    )


def _pack_weights(w):
    L = DEPTH
    w_in = w["w_in"]
    def with_swapped(rope):
        parts = rope.reshape(rope.shape[:-1] + (2, 2, C_ROPE // 4))
        return jnp.concatenate([rope, parts[..., ::-1, :].reshape(rope.shape)], axis=-1)

    kr_pad = jnp.pad(with_swapped(w_in[:, :, _KR0:]), ((0, 0), (0, 0), (C_NOPE, 0)))
    q_scale = jnp.full((A_WIDTH,), HEAD_DIM ** -0.5 * LOG2E, F32)
    k_scale = jnp.ones((A_KV_HEADS * HEAD_DIM,), F32)
    g_qk = jnp.concatenate([jnp.tile(w["g_qa"], (1, A_HEADS)), jnp.tile(w["g_ka"], (1, A_KV_HEADS))], axis=1)
    g_qk = g_qk * jnp.concatenate([q_scale, k_scale])[None, :]

    def halves_swapped(cols):
        parts = cols.reshape(cols.shape[:-1] + (_QK_A // (HEAD_DIM // 2), 2, HEAD_DIM // 4))
        return parts[..., ::-1, :].reshape(cols.shape)
    w_q_up = w["w_q_up"].reshape(L, C_Q_RANK, C_HEADS, C_NOPE + C_ROPE)
    w_q_up = jnp.concatenate([w_q_up[..., :C_NOPE], with_swapped(w_q_up[..., C_NOPE:])], axis=-1)
    kv = w["w_kv_up"].reshape(L, C_KV_RANK, C_HEADS, C_NOPE + C_V)
    k_nope = jnp.pad(kv[..., :C_NOPE], ((0, 0), (0, 0), (0, 0), (0, C_QK_PAD - C_NOPE)))
    w_kv_up = jnp.concatenate([k_nope.reshape(L, C_KV_RANK, C_HEADS * C_QK_PAD),
                               kv[..., C_NOPE:].reshape(L, C_KV_RANK, C_WIDTH)], axis=2)
    eye = jnp.eye(B_GROUPS, dtype=F32)
    w_pool = (w["w_pool"][:, :, :, None, :] * eye[None, :, None, :, None]).reshape(L, B_WIDTH, B_WIDTH)
    w_router = jnp.concatenate([w["w_router_expert"], w["w_router_group"]], axis=2)
    w_router = jnp.pad(jnp.swapaxes(w_router, 1, 2), ((0, 0), (0, ROUTER_ROWS - N_EXPERTS - MOE_GROUPS), (0, 0)))
    w_router_hi = w_router.astype(BF16)
    w_router_lo = (w_router - w_router_hi.astype(F32)).astype(BF16)
    row = lambda a: a[:, None, :]
    return dict(
        g_mix=row(w["g_mix"]),
        w_in=jnp.concatenate([w_in[:, :, :_KR0], kr_pad, halves_swapped(w_in[:, :, :_QK_A])], axis=2).astype(BF16),
        g_qk=row(g_qk), g_qk_sw=row(halves_swapped(g_qk)),
        g_cq=row(w["g_cq"]), w_q_up=w_q_up.reshape(L, C_Q_RANK, C_HEADS * C_QK_PAD).astype(BF16),
        g_ckv=row(w["g_ckv"]), w_kv_up=w_kv_up.astype(BF16),
        w_pool=w_pool.astype(BF16), s_pool=row(w["s_pool"]),
        g_out_a=row(w["g_out_a"]), g_out_b=row(w["g_out_b"]), g_out_c=row(w["g_out_c"]),
        w_out=w["w_out"].astype(BF16), g_ffn=row(w["g_ffn"]),
        w_router=jnp.concatenate([w_router_hi, w_router_lo], axis=1),
        w_gate_up=jnp.concatenate([w["w_gate"], w["w_up"]], axis=3).astype(BF16),
        w_down=w["w_down"].astype(BF16),
        g_ple=row(w["g_ple"]), w_ple_gate=w["w_ple_gate"].astype(BF16), w_ple_proj=w["w_ple_proj"].astype(BF16),
    )


def _trunk(x, p, packed, g_final):
    B, S, D = x.shape
    tabs = _rope_tables(S)
    hs = np.kron(np.eye(_QK_A // HEAD_DIM, dtype=np.float32), np.ones((HEAD_DIM, HEAD_DIM), np.float32))
    tabs["hsum"] = jnp.asarray(hs, BF16)
    tm = min(MOE_TILE, S)
    same_sub = np.kron(np.eye(tm // MOE_SUB, dtype=np.float32), np.ones((MOE_SUB, MOE_SUB), np.float32))
    tabs["before"] = jnp.asarray(np.triu(same_sub, 1), BF16)
    tabs["lower"] = jnp.asarray(np.tril(np.ones((N_EXPERTS, N_EXPERTS), np.float32), -1), BF16)
    g_final = g_final[None, :]
    h = x
    for i in range(DEPTH):
        lw = {name: a[i] for name, a in packed.items()}
        qat, ka, vat, ub, qct, kc, vct = _premix(h, lw, tabs)
        oa = _attention(qat, ka, vat, tq=Q_TILE_GQA, group=A_HEADS // A_KV_HEADS, dv=HEAD_DIM)
        oc = _attention(qct, kc, vct, tq=Q_TILE_MLA, group=1, dv=C_V)
        h1, m, route, offs, counts = _postmix(h, oa, oc, ub, lw, tabs)
        plan = jnp.concatenate([offs[..., 0], counts[..., 0]], axis=-1).reshape(-1, 2 * N_EXPERTS)
        y = _moe(m.reshape(B * S, D), route.reshape(-1, 8, MOE_SUB), plan, lw)
        h = _ple(h1, y.reshape(B, S, D), p, i, lw, g_final, final=(i == DEPTH - 1))
    return h


def kernel(x_prompt, x_sample, p_prompt, p_sample, g_mix, w_in, g_qa, g_ka, w_pool, s_pool, g_cq, w_q_up, g_ckv, w_kv_up, g_out_a, g_out_b, g_out_c, w_out, g_ffn, w_router_group, w_router_expert, w_gate, w_up, w_down, g_ple, w_ple_gate, w_ple_proj, g_final):
    weights = dict(g_mix=g_mix, w_in=w_in, g_qa=g_qa, g_ka=g_ka, w_pool=w_pool, s_pool=s_pool, g_cq=g_cq,
                   w_q_up=w_q_up, g_ckv=g_ckv, w_kv_up=w_kv_up, g_out_a=g_out_a, g_out_b=g_out_b,
                   g_out_c=g_out_c, w_out=w_out, g_ffn=g_ffn, w_router_group=w_router_group,
                   w_router_expert=w_router_expert, w_gate=w_gate, w_up=w_up, w_down=w_down, g_ple=g_ple,
                   w_ple_gate=w_ple_gate, w_ple_proj=w_ple_proj)
    packed = _pack_weights(weights)
    return (_trunk(x_prompt, p_prompt, packed, g_final), _trunk(x_sample, p_sample, packed, g_final))
```

```python
import functools
import math

import numpy as np
import jax
import jax.numpy as jnp
from jax import lax
from jax.experimental import pallas as pl
from jax.experimental.pallas import tpu as pltpu

F32 = jnp.float32
BF16 = jnp.bfloat16

D_MODEL = 1024
DEPTH = 4
GRID_W = 64
PLE_DIM = 256
HEAD_DIM = 64
ROPE_THETA = 10000.0
EPS = 1e-6
A_HEADS = 6
A_KV_HEADS = 2
A_WIDTH = A_HEADS * HEAD_DIM
B_GROUPS = 4
B_GROUP_DIM = 64
B_WINDOWS = (2, 4, 8, 16)
B_WIDTH = B_GROUPS * B_GROUP_DIM
C_HEADS = 6
C_NOPE = 64
C_ROPE = 32
C_V = 64
C_Q_RANK = 256
C_KV_RANK = 128
C_WIDTH = C_HEADS * C_V
C_QK_PAD = 128
MOE_GROUPS = 4
EXPERTS_PER_GROUP = 8
N_EXPERTS = MOE_GROUPS * EXPERTS_PER_GROUP
D_EXPERT = 256
ROUTER_ROWS = 48
LOG2E = math.log2(math.e)

LANES = 128
VMEM_BYTES = 64 * 1024 * 1024
VMEM_LIMIT_BYTES = VMEM_BYTES - 4 * 1024 * 1024

TOKEN_TILE = 512
Q_TILE_GQA = 256
Q_TILE_MLA = 512
ATTN_STREAMS = 2
KEY_SUB = 512
LOOKAHEAD = 2
CHUNKS_PER_BODY = 8
MOE_TILE = 1024
MOE_SUB = 256
SEG_ALIGN = 16
EXPERT_BLOCK = 32
MOE_TILE_GROUP = 8
EXPERTS_PER_STEP = 4
DISPATCH_STRIP = 256
V_PAD = 16
POOL_HALO = 8

_QK_A = A_WIDTH + A_KV_HEADS * HEAD_DIM
_V_A0 = _QK_A
_U_B0 = _V_A0 + A_KV_HEADS * HEAD_DIM
_CQ0 = _U_B0 + B_WIDTH
_CKV0 = _CQ0 + C_Q_RANK
_KR0 = _CKV0 + C_KV_RANK
_QK_SW0 = _KR0 + C_QK_PAD
IN_COLS_PACKED = _QK_SW0 + _QK_A


def _params(sem):
    return pltpu.CompilerParams(dimension_semantics=sem, vmem_limit_bytes=VMEM_LIMIT_BYTES)


def _rms(x):
    return x * lax.rsqrt(jnp.mean(x * x, axis=-1, keepdims=True) + EPS)


def _rope_partner(x):
    return pltpu.roll(x, x.shape[-1] - C_ROPE, x.ndim - 1)


def _premix_kernel(h_ref, g_mix_ref, w_in_ref, hsum_ref, g_qk_ref, g_qk_sw_ref, cosa_ref, sina_ref,
                   g_cq_ref, w_qup_ref, g_ckv_ref, w_kvup_ref, cosc_ref, sinc_ref,
                   qat_ref, ka_ref, vat_ref, ub_ref, qct_ref, kc_ref, vct_ref, *, c_scale):
    a = _rms(h_ref[0]) * g_mix_ref[...]
    z = jnp.dot(a.astype(BF16), w_in_ref[...], preferred_element_type=F32)

    qk = z[:, :_QK_A]
    qk_sw = z[:, _QK_SW0:]
    ss = jnp.dot((qk * qk).astype(BF16), hsum_ref[...], preferred_element_type=F32)
    reps = _QK_A // LANES
    cos_a = jnp.concatenate([cosa_ref[...]] * reps, axis=1) * g_qk_ref[...]
    sin_a = jnp.concatenate([sina_ref[...]] * reps, axis=1) * g_qk_sw_ref[...]
    qk = lax.rsqrt(ss * (1.0 / HEAD_DIM) + EPS) * (qk * cos_a + qk_sw * sin_a)
    qat = qk[:, :A_WIDTH].T.astype(BF16)
    tt = qat.shape[1]
    a_group = A_HEADS // A_KV_HEADS
    for hh in range(A_HEADS):
        for qi in range(tt // Q_TILE_GQA):
            col = (qi * a_group + hh % a_group) * Q_TILE_GQA
            qat_ref[0, hh // a_group, :, col:col + Q_TILE_GQA] = qat[
                hh * HEAD_DIM:(hh + 1) * HEAD_DIM, qi * Q_TILE_GQA:(qi + 1) * Q_TILE_GQA]
    ka = qk[:, A_WIDTH:].astype(BF16)
    for hh in range(A_KV_HEADS):
        ka_ref[0, hh] = ka[:, hh * HEAD_DIM:(hh + 1) * HEAD_DIM]
    ones_rows = (lax.broadcasted_iota(jnp.int32, (V_PAD, tt), 0) == 0).astype(BF16)
    vt = z[:, _V_A0:_U_B0].T.astype(BF16)
    for hh in range(A_KV_HEADS):
        vat_ref[0, hh, 0] = jnp.concatenate([vt[hh * HEAD_DIM:(hh + 1) * HEAD_DIM], ones_rows], axis=0)

    ub_ref[0] = z[:, _U_B0:_CQ0]

    cqn = _rms(z[:, _CQ0:_CKV0]) * g_cq_ref[...]
    qc = jnp.dot(cqn.astype(BF16), w_qup_ref[...], preferred_element_type=F32)
    cos_c = cosc_ref[...]
    sin_c = sinc_ref[...]
    cos_q = jnp.concatenate([cos_c * c_scale] * C_HEADS, axis=1)
    sin_q = jnp.concatenate([sin_c * c_scale] * C_HEADS, axis=1)
    qct = (qc * cos_q + _rope_partner(qc) * sin_q).T.astype(BF16)
    ckvn = _rms(z[:, _CKV0:_KR0]) * g_ckv_ref[...]
    kv = jnp.dot(ckvn.astype(BF16), w_kvup_ref[...], preferred_element_type=F32)
    kr = z[:, _KR0:_QK_SW0]
    kr = kr * cos_c + _rope_partner(kr) * sin_c
    for hh in range(C_HEADS):
        qct_ref[0, hh] = qct[hh * C_QK_PAD:(hh + 1) * C_QK_PAD]
        kc_ref[0, hh] = (kv[:, hh * C_QK_PAD:(hh + 1) * C_QK_PAD] + kr).astype(BF16)
    vct = kv[:, C_HEADS * C_QK_PAD:].T.astype(BF16)
    for hh in range(C_HEADS):
        vct_ref[0, hh, 0] = jnp.concatenate([vct[hh * C_V:(hh + 1) * C_V], ones_rows], axis=0)


def _premix(h, lw, tabs):
    B, S, D = h.shape
    Tt = TOKEN_TILE
    nt = S // Tt
    tile = lambda b, i: (b, i, 0)
    const2 = lambda b, i: (0, 0)
    headmajor = lambda b, i: (b, 0, i, 0)
    headmajor_t = lambda b, i: (b, 0, 0, i)
    vt_map = lambda b, i: (b, 0, i, 0, 0)
    tab_map = lambda b, i: (i, 0)
    in_specs = [
        pl.BlockSpec((1, Tt, D), tile),
        pl.BlockSpec((1, D), const2),
        pl.BlockSpec((D, IN_COLS_PACKED), const2),
        pl.BlockSpec((_QK_A, _QK_A), const2),
        pl.BlockSpec((1, _QK_A), const2),
        pl.BlockSpec((1, _QK_A), const2),
        pl.BlockSpec((Tt, LANES), tab_map),
        pl.BlockSpec((Tt, LANES), tab_map),
        pl.BlockSpec((1, C_Q_RANK), const2),
        pl.BlockSpec((C_Q_RANK, C_HEADS * C_QK_PAD), const2),
        pl.BlockSpec((1, C_KV_RANK), const2),
        pl.BlockSpec((C_KV_RANK, C_HEADS * C_QK_PAD + C_WIDTH), const2),
        pl.BlockSpec((Tt, LANES), tab_map),
        pl.BlockSpec((Tt, LANES), tab_map),
    ]
    out_shape = [
        jax.ShapeDtypeStruct((B, A_KV_HEADS, HEAD_DIM, S * (A_HEADS // A_KV_HEADS)), BF16),
        jax.ShapeDtypeStruct((B, A_KV_HEADS, S, HEAD_DIM), BF16),
        jax.ShapeDtypeStruct((B, A_KV_HEADS, nt, HEAD_DIM + V_PAD, Tt), BF16),
        jax.ShapeDtypeStruct((B, S, B_WIDTH), F32),
        jax.ShapeDtypeStruct((B, C_HEADS, C_QK_PAD, S), BF16),
        jax.ShapeDtypeStruct((B, C_HEADS, S, C_QK_PAD), BF16),
        jax.ShapeDtypeStruct((B, C_HEADS, nt, C_V + V_PAD, Tt), BF16),
    ]
    out_specs = [
        pl.BlockSpec((1, A_KV_HEADS, HEAD_DIM, Tt * (A_HEADS // A_KV_HEADS)), headmajor_t),
        pl.BlockSpec((1, A_KV_HEADS, Tt, HEAD_DIM), headmajor),
        pl.BlockSpec((1, A_KV_HEADS, 1, HEAD_DIM + V_PAD, Tt), vt_map),
        pl.BlockSpec((1, Tt, B_WIDTH), tile),
        pl.BlockSpec((1, C_HEADS, C_QK_PAD, Tt), headmajor_t),
        pl.BlockSpec((1, C_HEADS, Tt, C_QK_PAD), headmajor),
        pl.BlockSpec((1, C_HEADS, 1, C_V + V_PAD, Tt), vt_map),
    ]
    kern = functools.partial(_premix_kernel, c_scale=(C_NOPE + C_ROPE) ** -0.5 * LOG2E)
    return pl.pallas_call(
        kern, grid=(B, nt), in_specs=in_specs, out_specs=out_specs, out_shape=out_shape,
        compiler_params=_params(("parallel", "parallel")), name="premix",
    )(h, lw["g_mix"], lw["w_in"], tabs["hsum"], lw["g_qk"], lw["g_qk_sw"], tabs["cos_a"], tabs["sin_a"],
      lw["g_cq"], lw["w_q_up"], lw["g_ckv"], lw["w_kv_up"], tabs["cos_c"], tabs["sin_c"])


def _attn_kernel(qt_ref, k_ref, vt_ref, o_ref, s_buf, p_buf, *, order, n_chunks, chunk, sub, dv, group):
    n_streams, _, width = qt_ref.shape[1:]
    rows = vt_ref.shape[3]
    nsub = chunk // sub
    n_pos = len(order)
    per_body = 1 + max(coff for _, coff in order)
    assert LOOKAHEAD < n_pos and n_chunks % per_body == 0

    def score_piece(pos, j, r):
        si = order[pos][0]
        start = pl.multiple_of(j * chunk + r * sub, sub)
        s = jnp.dot(k_ref[0, si, pl.ds(start, sub), :], qt_ref[0, si], preferred_element_type=F32)
        s_buf[pos, r * sub:(r + 1) * sub, :] = s
        return jnp.max(s, axis=0, keepdims=True)

    def prob_piece(pos, r, m_b):
        s = s_buf[pos, r * sub:(r + 1) * sub, :]
        p_buf[pos, r * sub:(r + 1) * sub, :] = jnp.exp2(s - m_b).astype(BF16)

    def all_scores(pos, j):
        cm = None
        for r in range(nsub):
            c = score_piece(pos, j, r)
            cm = c if cm is None else jnp.maximum(cm, c)
        return cm

    def body(b, carry):
        pending, state = carry
        colmax = dict(enumerate(pending))
        state = list(state)
        for i, (si, coff) in enumerate(order):
            m, acc = state[si]
            m_new = jnp.maximum(m, colmax[i])
            m_b = jnp.broadcast_to(m_new, (sub, width))
            tpos = (i + LOOKAHEAD) % n_pos
            tj = jnp.minimum((b + (i + LOOKAHEAD) // n_pos) * per_body + order[tpos][1], n_chunks - 1)
            cm = None
            for r in range(nsub):
                c = score_piece(tpos, tj, r)
                cm = c if cm is None else jnp.maximum(cm, c)
                prob_piece(i, r, m_b)
            colmax[i + LOOKAHEAD] = cm
            pv = jnp.dot(vt_ref[0, si, b * per_body + coff], p_buf[i], preferred_element_type=F32)
            state[si] = (m_new, jnp.exp2(m - m_new) * acc + pv)
        return tuple(colmax[n_pos + k] for k in range(LOOKAHEAD)), tuple(state)

    pending = tuple(all_scores(k, order[k][1]) for k in range(LOOKAHEAD))
    init = tuple((jnp.full((1, width), -jnp.inf, F32), jnp.zeros((rows, width), F32)) for _ in range(n_streams))
    _, final = lax.fori_loop(0, n_chunks // per_body, body, (pending, init))
    tq = width // group
    outs = []
    for _, acc in final:
        out_t = acc[:dv] * (1.0 / acc[dv:dv + 1])
        outs += [out_t[:, g * tq:(g + 1) * tq] for g in range(group)]
    o_ref[0] = jnp.concatenate(outs, axis=0).T


def _attention(qt, k, vt, *, tq, group, dv):
    B, Hkv, dk, _ = qt.shape
    S = k.shape[2]
    nk, rows, Tk = vt.shape[2:]
    n_streams = ATTN_STREAMS
    width = group * tq
    per_body = math.gcd(CHUNKS_PER_BODY, nk)
    order = tuple((si, coff) for coff in range(per_body) for si in range(n_streams))
    kern = functools.partial(_attn_kernel, order=order, n_chunks=nk, chunk=Tk, sub=KEY_SUB, dv=dv, group=group)
    return pl.pallas_call(
        kern, grid=(B, Hkv // n_streams, S // tq),
        in_specs=[
            pl.BlockSpec((1, n_streams, dk, width), lambda b, g, i: (b, g, 0, i)),
            pl.BlockSpec((1, n_streams, S, dk), lambda b, g, i: (b, g, 0, 0)),
            pl.BlockSpec((1, n_streams, nk, rows, Tk), lambda b, g, i: (b, g, 0, 0, 0)),
        ],
        out_specs=pl.BlockSpec((1, tq, n_streams * group * dv), lambda b, g, i: (b, i, g)),
        out_shape=jax.ShapeDtypeStruct((B, S, Hkv * group * dv), F32),
        scratch_shapes=[pltpu.VMEM((len(order), Tk, width), F32), pltpu.VMEM((len(order), Tk, width), BF16)],
        compiler_params=_params(("parallel", "parallel", "arbitrary")), name="attention",
    )(qt, k, vt)


def _route(logits):
    tt = logits.shape[1]
    le = logits[:N_EXPERTS]
    lg = logits[N_EXPERTS:]
    row8 = lax.broadcasted_iota(jnp.int32, (ROUTER_ROWS - N_EXPERTS, tt), 0)
    lg = jnp.where(row8 < MOE_GROUPS, lg, -jnp.inf)
    gmax = jnp.max(lg, axis=0, keepdims=True)
    gsel = jnp.min(jnp.where(lg == gmax, row8, MOE_GROUPS), axis=0, keepdims=True)
    gprob = 1.0 / jnp.sum(jnp.exp(lg - gmax), axis=0, keepdims=True)
    row = lax.broadcasted_iota(jnp.int32, (N_EXPERTS, tt), 0)
    lm = jnp.where((row // EXPERTS_PER_GROUP) == gsel, le, -jnp.inf)
    m1 = jnp.max(lm, axis=0, keepdims=True)
    i1 = jnp.min(jnp.where(lm == m1, row, N_EXPERTS), axis=0, keepdims=True)
    lm2 = jnp.where(row == i1, -jnp.inf, lm)
    m2 = jnp.max(lm2, axis=0, keepdims=True)
    i2 = jnp.min(jnp.where(lm2 == m2, row, N_EXPERTS), axis=0, keepdims=True)
    r = jnp.exp(m2 - m1)
    w1 = gprob / (1.0 + r)
    w2 = w1 * r
    return i1, i2, w1, w2


def _dispatch_plan(i1, i2, w1, w2, before_ref, lower_ref):
    tt = i1.shape[1]
    row = lax.broadcasted_iota(jnp.int32, (N_EXPERTS, tt), 0)
    hit1 = row == i1
    hit2 = row == i2
    assign = jnp.where(hit1 | hit2, 1.0, 0.0)
    rank = jnp.dot(assign.astype(BF16), before_ref[...], preferred_element_type=F32)
    pos, offs, counts = [], [], []
    for u in range(tt // MOE_SUB):
        lanes = slice(u * MOE_SUB, (u + 1) * MOE_SUB)
        count = jnp.sum(assign[:, lanes], axis=1, keepdims=True)
        segs = jnp.floor((count + (SEG_ALIGN - 1.0)) * (1.0 / SEG_ALIGN))
        segs_b = jnp.broadcast_to(segs, (N_EXPERTS, LANES)).astype(BF16)
        off = SEG_ALIGN * jnp.dot(lower_ref[...], segs_b, preferred_element_type=F32)
        pos.append(off[:, :1] + rank[:, lanes])
        offs.append(off.astype(jnp.int32))
        counts.append(jnp.broadcast_to(count, (N_EXPERTS, LANES)).astype(jnp.int32))
    pos = jnp.concatenate(pos, axis=1)
    pos1 = jnp.sum(jnp.where(hit1, pos, 0.0), axis=0, keepdims=True)
    pos2 = jnp.sum(jnp.where(hit2, pos, 0.0), axis=0, keepdims=True)
    route = jnp.concatenate([pos1, pos2, w1, w2, jnp.zeros((4, tt), F32)], axis=0)
    return route, offs, counts


def _postmix_kernel(h_ref, oa_ref, oc_ref, ub_ref, prev_ref, next_ref, wpool_ref, spool_ref,
                    ga_ref, gb_ref, gc_ref, wout_ref, gffn_ref, wr_ref, before_ref, lower_ref,
                    h1_ref, m_ref, route_ref, offs_ref, counts_ref, ext_ref, *, seq_len):
    i = pl.program_id(1)
    nt = pl.num_programs(1)
    tt = h_ref.shape[1]
    half = B_WIDTH // 2

    u = ub_ref[0]
    ext_ref[POOL_HALO:POOL_HALO + tt, :] = u
    ext_ref[:POOL_HALO, :] = jnp.where(i > 0, prev_ref[0, 0], 0.0)
    ext_ref[POOL_HALO + tt:, :] = jnp.where(i < nt - 1, next_ref[0, 0], 0.0)

    def window(lo, hi, lanes):
        acc = None
        for d in range(lo, hi):
            piece = ext_ref[POOL_HALO + d:POOL_HALO + d + tt, lanes]
            acc = piece if acc is None else acc + piece
        return acc

    lo_lanes = slice(0, half)
    hi_lanes = slice(half, B_WIDTH)
    w2 = window(-1, 1, lo_lanes)
    w4 = w2 + window(-2, -1, lo_lanes) + window(1, 2, lo_lanes)
    w8 = window(-4, 4, hi_lanes)
    w16 = w8 + window(-8, -4, hi_lanes) + window(4, 8, hi_lanes)
    lane = lax.broadcasted_iota(jnp.int32, (tt, half), 1)
    first = lane < B_GROUP_DIM
    sums = jnp.concatenate([jnp.where(first, w2, w4), jnp.where(first, w8, w16)], axis=1)
    pos = i * tt + lax.broadcasted_iota(jnp.int32, (tt, B_WIDTH), 0)
    lane_b = lax.broadcasted_iota(jnp.int32, (tt, B_WIDTH), 1)
    hw = jnp.left_shift(1, lane_b // B_GROUP_DIM)
    cnt = jnp.minimum(pos + hw, seq_len) - jnp.maximum(pos - hw, 0)
    dlt = sums / cnt.astype(F32) - u
    ob = jnp.dot(dlt.astype(BF16), wpool_ref[...], preferred_element_type=F32) * spool_ref[...]

    merged = jnp.concatenate([
        (_rms(oa_ref[0]) * ga_ref[...]).astype(BF16),
        (_rms(ob) * gb_ref[...]).astype(BF16),
        (_rms(oc_ref[0]) * gc_ref[...]).astype(BF16)], axis=1)
    h1 = h_ref[0] + jnp.dot(merged, wout_ref[...], preferred_element_type=F32)
    h1_ref[0] = h1

    m = _rms(h1) * gffn_ref[...]
    m_hi = m.astype(BF16)
    m_ref[0] = m_hi
    m_lo = (m - m_hi.astype(F32)).astype(BF16)
    nt_dims = (((1,), (1,)), ((), ()))
    both = lax.dot_general(wr_ref[...], m_hi, nt_dims, preferred_element_type=F32)
    logits = (both[:ROUTER_ROWS] + both[ROUTER_ROWS:]
              + lax.dot_general(wr_ref[:ROUTER_ROWS, :], m_lo, nt_dims, preferred_element_type=F32))
    route, offs, counts = _dispatch_plan(*_route(logits), before_ref, lower_ref)
    for u in range(len(offs)):
        route_ref[0, 0, u] = route[:, u * MOE_SUB:(u + 1) * MOE_SUB]
        offs_ref[0, 0, u] = offs[u]
        counts_ref[0, 0, u] = counts[u]


def _postmix(h, oa, oc, ub, lw, tabs):
    B, S, D = h.shape
    Tt = min(MOE_TILE, S)
    assert Tt % MOE_SUB == 0
    nt = S // Tt
    rows = Tt // POOL_HALO
    ub_rows = ub.reshape(B, S // POOL_HALO, POOL_HALO, B_WIDTH)
    tile = lambda b, i: (b, i, 0)
    const2 = lambda b, i: (0, 0)
    in_specs = [
        pl.BlockSpec((1, Tt, D), tile),
        pl.BlockSpec((1, Tt, A_WIDTH), tile),
        pl.BlockSpec((1, Tt, C_WIDTH), tile),
        pl.BlockSpec((1, Tt, B_WIDTH), tile),
        pl.BlockSpec((1, 1, POOL_HALO, B_WIDTH), lambda b, i: (b, jnp.maximum(i * rows - 1, 0), 0, 0)),
        pl.BlockSpec((1, 1, POOL_HALO, B_WIDTH),
                     lambda b, i: (b, jnp.minimum((i + 1) * rows, S // POOL_HALO - 1), 0, 0)),
        pl.BlockSpec((B_WIDTH, B_WIDTH), const2),
        pl.BlockSpec((1, B_WIDTH), const2),
        pl.BlockSpec((1, A_WIDTH), const2),
        pl.BlockSpec((1, B_WIDTH), const2),
        pl.BlockSpec((1, C_WIDTH), const2),
        pl.BlockSpec((D, D), const2),
        pl.BlockSpec((1, D), const2),
        pl.BlockSpec((2 * ROUTER_ROWS, D), const2),
        pl.BlockSpec((Tt, Tt), const2),
        pl.BlockSpec((N_EXPERTS, N_EXPERTS), const2),
    ]
    per_tile = lambda b, i: (b, i, 0, 0, 0)
    n_sub = Tt // MOE_SUB
    out_shape = [
        jax.ShapeDtypeStruct((B, S, D), F32),
        jax.ShapeDtypeStruct((B, S, D), BF16),
        jax.ShapeDtypeStruct((B, nt, n_sub, 8, MOE_SUB), F32),
        jax.ShapeDtypeStruct((B, nt, n_sub, N_EXPERTS, LANES), jnp.int32),
        jax.ShapeDtypeStruct((B, nt, n_sub, N_EXPERTS, LANES), jnp.int32),
    ]
    out_specs = [
        pl.BlockSpec((1, Tt, D), tile),
        pl.BlockSpec((1, Tt, D), tile),
        pl.BlockSpec((1, 1, n_sub, 8, MOE_SUB), per_tile),
        pl.BlockSpec((1, 1, n_sub, N_EXPERTS, LANES), per_tile),
        pl.BlockSpec((1, 1, n_sub, N_EXPERTS, LANES), per_tile),
    ]
    return pl.pallas_call(
        functools.partial(_postmix_kernel, seq_len=S), grid=(B, nt),
        in_specs=in_specs, out_specs=out_specs, out_shape=out_shape,
        scratch_shapes=[pltpu.VMEM((Tt + 2 * POOL_HALO, B_WIDTH), F32)],
        compiler_params=_params(("parallel", "parallel")), name="postmix",
    )(h, oa, oc, ub, ub_rows, ub_rows, lw["w_pool"], lw["s_pool"], lw["g_out_a"], lw["g_out_b"],
      lw["g_out_c"], lw["w_out"], lw["g_ffn"], lw["w_router"],
      tabs["before"], tabs["lower"])


def _moe_kernel(plan_ref, m_ref, route_ref, wgu_ref, wd_ref, o_ref, perm_ref, xy_ref, ws_ref, *, n_rows):
    g = pl.program_id(0)
    e = pl.program_id(1)
    n_tiles, _, tm = route_ref.shape
    d = m_ref.shape[1]

    @pl.when(e == 0)
    def _():
        for k in range(n_tiles):
            route = route_ref[k]
            pos1 = route[0:1].astype(jnp.int32)
            pos2 = route[1:2].astype(jnp.int32)
            w1 = route[2:3]
            w2 = route[3:4]
            m = m_ref[k * tm:(k + 1) * tm, :]

            def strip(i, _, k=k, pos1=pos1, pos2=pos2, w1=w1, w2=w2, m=m):
                r0 = pl.multiple_of(i * DISPATCH_STRIP, DISPATCH_STRIP)
                rows = r0 + lax.broadcasted_iota(jnp.int32, (DISPATCH_STRIP, tm), 0)
                hit1 = rows == pos1
                hit2 = rows == pos2
                perm = jnp.where(hit1 | hit2, 1.0, 0.0).astype(BF16)
                perm_ref[k, pl.ds(r0, DISPATCH_STRIP), :] = perm
                ws_ref[k, pl.ds(r0, DISPATCH_STRIP), :] = jnp.sum(
                    jnp.where(hit1, w1, jnp.where(hit2, w2, 0.0)), axis=1, keepdims=True)
                xy_ref[k, pl.ds(r0, DISPATCH_STRIP), :] = jnp.dot(
                    perm, m, preferred_element_type=F32).astype(BF16)
                return 0

            lax.fori_loop(0, n_rows // DISPATCH_STRIP, strip, 0, unroll=True)
            xy_ref[k, n_rows:, :] = jnp.zeros((EXPERT_BLOCK, d), BF16)
            ws_ref[k, n_rows:, :] = jnp.zeros((EXPERT_BLOCK, 1), F32)

    n_exp = wgu_ref.shape[0]
    plans = [[(plan_ref[g * n_tiles + k, e * n_exp + j], plan_ref[g * n_tiles + k, N_EXPERTS + e * n_exp + j])
              for k in range(n_tiles)] for j in range(n_exp)]

    def blocks(bi):
        starts = [[pl.multiple_of(jnp.minimum(off + bi * EXPERT_BLOCK, n_rows), SEG_ALIGN) for off, _ in pj]
                  for pj in plans]
        xs = [[xy_ref[k, pl.ds(r0, EXPERT_BLOCK), :] for k, r0 in enumerate(sj)] for sj in starts]
        ws = [[ws_ref[k, pl.ds(r0, EXPERT_BLOCK), :] for k, r0 in enumerate(sj)] for sj in starts]
        gus = [jnp.dot(jnp.concatenate(xs[j], axis=0), wgu_ref[j], preferred_element_type=F32)
               for j in range(n_exp)]
        hids = [gu[:, :D_EXPERT] * jax.nn.sigmoid(gu[:, :D_EXPERT]) * gu[:, D_EXPERT:]
                * jnp.concatenate(ws[j], axis=0) for j, gu in enumerate(gus)]
        ys = [jnp.dot(hid.astype(BF16), wd_ref[j], preferred_element_type=F32).astype(BF16)
              for j, hid in enumerate(hids)]
        row = lax.broadcasted_iota(jnp.int32, (EXPERT_BLOCK, 1), 0)
        for j in range(n_exp):
            for k, ((_, cnt), r0, x) in enumerate(zip(plans[j], starts[j], xs[j])):
                in_segment = row < cnt - bi * EXPERT_BLOCK
                xy_ref[k, pl.ds(r0, EXPERT_BLOCK), :] = jnp.where(
                    in_segment, ys[j][k * EXPERT_BLOCK:(k + 1) * EXPERT_BLOCK], x)

    blocks(0)
    longest = functools.reduce(jnp.maximum, [cnt for pj in plans for _, cnt in pj])
    n_blocks = lax.shift_right_logical(longest + (EXPERT_BLOCK - 1), EXPERT_BLOCK.bit_length() - 1)

    def more(bi, _):
        blocks(bi)
        return 0

    lax.fori_loop(1, n_blocks, more, 0)

    @pl.when(e == pl.num_programs(1) - 1)
    def _():
        tn_dims = (((0,), (0,)), ((), ()))
        for k in range(n_tiles):
            y = lax.dot_general(perm_ref[k], xy_ref[k, :n_rows, :], tn_dims, preferred_element_type=F32)
            o_ref[k * tm:(k + 1) * tm, :] = y.astype(BF16)


def _moe(m, route, plan, lw):
    T, D = m.shape
    n_all, _, Tm = route.shape
    group = math.gcd(MOE_TILE_GROUP, n_all)
    n_rows = 2 * Tm + N_EXPERTS * SEG_ALIGN
    assert n_rows % DISPATCH_STRIP == 0
    tile = lambda g, e, plan: (g, 0)
    grid_spec = pltpu.PrefetchScalarGridSpec(
        num_scalar_prefetch=1, grid=(n_all // group, N_EXPERTS // EXPERTS_PER_STEP),
        in_specs=[
            pl.BlockSpec((group * Tm, D), tile),
            pl.BlockSpec((group, 8, Tm), lambda g, e, plan: (g, 0, 0)),
            pl.BlockSpec((EXPERTS_PER_STEP, D, 2 * D_EXPERT), lambda g, e, plan: (e, 0, 0)),
            pl.BlockSpec((EXPERTS_PER_STEP, D_EXPERT, D), lambda g, e, plan: (e, 0, 0)),
        ],
        out_specs=pl.BlockSpec((group * Tm, D), tile),
        scratch_shapes=[
            pltpu.VMEM((group, n_rows, Tm), BF16),
            pltpu.VMEM((group, n_rows + EXPERT_BLOCK, D), BF16),
            pltpu.VMEM((group, n_rows + EXPERT_BLOCK, 1), F32),
        ],
    )
    return pl.pallas_call(
        functools.partial(_moe_kernel, n_rows=n_rows), grid_spec=grid_spec,
        out_shape=jax.ShapeDtypeStruct((T, D), BF16),
        compiler_params=_params(("parallel", "arbitrary")), name="moe",
    )(plan, m, route, lw["w_gate_up"], lw["w_down"])


def _ple_kernel(h1_ref, y_ref, p_ref, gple_ref, wgate_ref, wproj_ref, gfinal_ref, o_ref, *, final):
    h = h1_ref[0] + y_ref[0].astype(F32)
    r = (_rms(h) * gple_ref[...]).astype(BF16)
    gate = jax.nn.sigmoid(jnp.dot(r, wgate_ref[...], preferred_element_type=F32))
    emb = jnp.dot(p_ref[0, 0].astype(BF16), wproj_ref[...], preferred_element_type=F32)
    out = h + gate * emb
    if final:
        out = _rms(out) * gfinal_ref[...]
    o_ref[0] = out


def _ple(h1, y, p, layer, lw, g_final, final):
    B, S, D = h1.shape
    Tt = TOKEN_TILE
    tile = lambda b, i: (b, i, 0)
    const2 = lambda b, i: (0, 0)
    return pl.pallas_call(
        functools.partial(_ple_kernel, final=final), grid=(B, S // Tt),
        in_specs=[
            pl.BlockSpec((1, Tt, D), tile),
            pl.BlockSpec((1, Tt, D), tile),
            pl.BlockSpec((1, 1, Tt, PLE_DIM), lambda b, i: (layer, b, i, 0)),
            pl.BlockSpec((1, D), const2),
            pl.BlockSpec((D, D), const2),
            pl.BlockSpec((PLE_DIM, D), const2),
            pl.BlockSpec((1, D), const2),
        ],
        out_specs=pl.BlockSpec((1, Tt, D), tile),
        out_shape=jax.ShapeDtypeStruct((B, S, D), F32),
        compiler_params=_params(("parallel", "parallel")), name="ple",
    )(h1, y, p, lw["g_ple"], lw["w_ple_gate"], lw["w_ple_proj"], g_final)


def _rope_tables(seq_len):
    rows_n = seq_len // GRID_W
    row = jnp.repeat(jnp.arange(rows_n), GRID_W).astype(F32)[:, None]
    col = jnp.tile(jnp.arange(GRID_W), rows_n).astype(F32)[:, None]

    def axial(rot_dim):
        n_freq = rot_dim // 4
        inv = ROPE_THETA ** (-jnp.arange(n_freq, dtype=F32) / n_freq)
        ar, ac = row * inv, col * inv
        cos = jnp.concatenate([jnp.cos(ar), jnp.cos(ar), jnp.cos(ac), jnp.cos(ac)], axis=1)
        sin = jnp.concatenate([-jnp.sin(ar), jnp.sin(ar), -jnp.sin(ac), jnp.sin(ac)], axis=1)
        return cos, sin

    cos_a, sin_a = axial(HEAD_DIM)
    cos_r, sin_r = axial(C_ROPE)
    ones = jnp.ones((seq_len, C_NOPE), F32)
    zeros = jnp.zeros((seq_len, C_NOPE), F32)
    pad = jnp.zeros((seq_len, C_QK_PAD - C_NOPE - C_ROPE), F32)
    return dict(
        cos_a=jnp.concatenate([cos_a, cos_a], axis=1), sin_a=jnp.concatenate([sin_a, sin_a], axis=1),
        cos_c=jnp.concatenate([ones, cos_r, pad], axis=1), sin_c=jnp.concatenate([zeros, sin_r, pad], axis=1),
    )


def _pack_weights(w):
    L = DEPTH
    w_in = w["w_in"]
    def with_swapped(rope):
        parts = rope.reshape(rope.shape[:-1] + (2, 2, C_ROPE // 4))
        return jnp.concatenate([rope, parts[..., ::-1, :].reshape(rope.shape)], axis=-1)

    kr_pad = jnp.pad(with_swapped(w_in[:, :, _KR0:]), ((0, 0), (0, 0), (C_NOPE, 0)))
    q_scale = jnp.full((A_WIDTH,), HEAD_DIM ** -0.5 * LOG2E, F32)
    k_scale = jnp.ones((A_KV_HEADS * HEAD_DIM,), F32)
    g_qk = jnp.concatenate([jnp.tile(w["g_qa"], (1, A_HEADS)), jnp.tile(w["g_ka"], (1, A_KV_HEADS))], axis=1)
    g_qk = g_qk * jnp.concatenate([q_scale, k_scale])[None, :]

    def halves_swapped(cols):
        parts = cols.reshape(cols.shape[:-1] + (_QK_A // (HEAD_DIM // 2), 2, HEAD_DIM // 4))
        return parts[..., ::-1, :].reshape(cols.shape)
    w_q_up = w["w_q_up"].reshape(L, C_Q_RANK, C_HEADS, C_NOPE + C_ROPE)
    w_q_up = jnp.concatenate([w_q_up[..., :C_NOPE], with_swapped(w_q_up[..., C_NOPE:])], axis=-1)
    kv = w["w_kv_up"].reshape(L, C_KV_RANK, C_HEADS, C_NOPE + C_V)
    k_nope = jnp.pad(kv[..., :C_NOPE], ((0, 0), (0, 0), (0, 0), (0, C_QK_PAD - C_NOPE)))
    w_kv_up = jnp.concatenate([k_nope.reshape(L, C_KV_RANK, C_HEADS * C_QK_PAD),
                               kv[..., C_NOPE:].reshape(L, C_KV_RANK, C_WIDTH)], axis=2)
    eye = jnp.eye(B_GROUPS, dtype=F32)
    w_pool = (w["w_pool"][:, :, :, None, :] * eye[None, :, None, :, None]).reshape(L, B_WIDTH, B_WIDTH)
    w_router = jnp.concatenate([w["w_router_expert"], w["w_router_group"]], axis=2)
    w_router = jnp.pad(jnp.swapaxes(w_router, 1, 2), ((0, 0), (0, ROUTER_ROWS - N_EXPERTS - MOE_GROUPS), (0, 0)))
    w_router_hi = w_router.astype(BF16)
    w_router_lo = (w_router - w_router_hi.astype(F32)).astype(BF16)
    row = lambda a: a[:, None, :]
    return dict(
        g_mix=row(w["g_mix"]),
        w_in=jnp.concatenate([w_in[:, :, :_KR0], kr_pad, halves_swapped(w_in[:, :, :_QK_A])], axis=2).astype(BF16),
        g_qk=row(g_qk), g_qk_sw=row(halves_swapped(g_qk)),
        g_cq=row(w["g_cq"]), w_q_up=w_q_up.reshape(L, C_Q_RANK, C_HEADS * C_QK_PAD).astype(BF16),
        g_ckv=row(w["g_ckv"]), w_kv_up=w_kv_up.astype(BF16),
        w_pool=w_pool.astype(BF16), s_pool=row(w["s_pool"]),
        g_out_a=row(w["g_out_a"]), g_out_b=row(w["g_out_b"]), g_out_c=row(w["g_out_c"]),
        w_out=w["w_out"].astype(BF16), g_ffn=row(w["g_ffn"]),
        w_router=jnp.concatenate([w_router_hi, w_router_lo], axis=1),
        w_gate_up=jnp.concatenate([w["w_gate"], w["w_up"]], axis=3).astype(BF16),
        w_down=w["w_down"].astype(BF16),
        g_ple=row(w["g_ple"]), w_ple_gate=w["w_ple_gate"].astype(BF16), w_ple_proj=w["w_ple_proj"].astype(BF16),
    )


def _trunk(x, p, packed, g_final):
    B, S, D = x.shape
    tabs = _rope_tables(S)
    hs = np.kron(np.eye(_QK_A // HEAD_DIM, dtype=np.float32), np.ones((HEAD_DIM, HEAD_DIM), np.float32))
    tabs["hsum"] = jnp.asarray(hs, BF16)
    tm = min(MOE_TILE, S)
    same_sub = np.kron(np.eye(tm // MOE_SUB, dtype=np.float32), np.ones((MOE_SUB, MOE_SUB), np.float32))
    tabs["before"] = jnp.asarray(np.triu(same_sub, 1), BF16)
    tabs["lower"] = jnp.asarray(np.tril(np.ones((N_EXPERTS, N_EXPERTS), np.float32), -1), BF16)
    g_final = g_final[None, :]
    h = x
    for i in range(DEPTH):
        lw = {name: a[i] for name, a in packed.items()}
        qat, ka, vat, ub, qct, kc, vct = _premix(h, lw, tabs)
        oa = _attention(qat, ka, vat, tq=Q_TILE_GQA, group=A_HEADS // A_KV_HEADS, dv=HEAD_DIM)
        oc = _attention(qct, kc, vct, tq=Q_TILE_MLA, group=1, dv=C_V)
        h1, m, route, offs, counts = _postmix(h, oa, oc, ub, lw, tabs)
        plan = jnp.concatenate([offs[..., 0], counts[..., 0]], axis=-1).reshape(-1, 2 * N_EXPERTS)
        y = _moe(m.reshape(B * S, D), route.reshape(-1, 8, MOE_SUB), plan, lw)
        h = _ple(h1, y.reshape(B, S, D), p, i, lw, g_final, final=(i == DEPTH - 1))
    return h


def kernel(x_prompt, x_sample, p_prompt, p_sample, g_mix, w_in, g_qa, g_ka, w_pool, s_pool, g_cq, w_q_up, g_ckv, w_kv_up, g_out_a, g_out_b, g_out_c, w_out, g_ffn, w_router_group, w_router_expert, w_gate, w_up, w_down, g_ple, w_ple_gate, w_ple_proj, g_final):
    weights = dict(g_mix=g_mix, w_in=w_in, g_qa=g_qa, g_ka=g_ka, w_pool=w_pool, s_pool=s_pool, g_cq=g_cq,
                   w_q_up=w_q_up, g_ckv=g_ckv, w_kv_up=w_kv_up, g_out_a=g_out_a, g_out_b=g_out_b,
                   g_out_c=g_out_c, w_out=w_out, g_ffn=g_ffn, w_router_group=w_router_group,
                   w_router_expert=w_router_expert, w_gate=w_gate, w_up=w_up, w_down=w_down, g_ple=g_ple,
                   w_ple_gate=w_ple_gate, w_ple_proj=w_ple_proj)
    packed = _pack_weights(weights)
    return (_trunk(x_prompt, p_prompt, packed, g_final), _trunk(x_sample, p_sample, packed, g_final))
```

```python
import functools
import math

import numpy as np
import jax
import jax.numpy as jnp
from jax import lax
from jax.experimental import pallas as pl
from jax.experimental.pallas import tpu as pltpu

F32 = jnp.float32
BF16 = jnp.bfloat16

D_MODEL = 1024
DEPTH = 4
GRID_W = 64
PLE_DIM = 256
HEAD_DIM = 64
ROPE_THETA = 10000.0
EPS = 1e-6
A_HEADS = 6
A_KV_HEADS = 2
A_WIDTH = A_HEADS * HEAD_DIM
B_GROUPS = 4
B_GROUP_DIM = 64
B_WINDOWS = (2, 4, 8, 16)
B_WIDTH = B_GROUPS * B_GROUP_DIM
C_HEADS = 6
C_NOPE = 64
C_ROPE = 32
C_V = 64
C_Q_RANK = 256
C_KV_RANK = 128
C_WIDTH = C_HEADS * C_V
C_QK_PAD = 128
MOE_GROUPS = 4
EXPERTS_PER_GROUP = 8
N_EXPERTS = MOE_GROUPS * EXPERTS_PER_GROUP
D_EXPERT = 256
ROUTER_ROWS = 48
LOG2E = math.log2(math.e)

LANES = 128
VMEM_BYTES = 64 * 1024 * 1024
VMEM_LIMIT_BYTES = VMEM_BYTES - 4 * 1024 * 1024

TOKEN_TILE = 512
Q_TILE_GQA = 256
Q_TILE_MLA = 512
ATTN_STREAMS = 2
KEY_SUB = 512
LOOKAHEAD = 2
CHUNKS_PER_BODY = 8
MOE_TILE = 1024
MOE_SUB = 256
SEG_ALIGN = 16
EXPERT_BLOCK = 32
MOE_TILE_GROUP = 8
EXPERTS_PER_STEP = 4
DISPATCH_STRIP = 256
V_PAD = 16
POOL_HALO = 8

_QK_A = A_WIDTH + A_KV_HEADS * HEAD_DIM
_V_A0 = _QK_A
_U_B0 = _V_A0 + A_KV_HEADS * HEAD_DIM
_CQ0 = _U_B0 + B_WIDTH
_CKV0 = _CQ0 + C_Q_RANK
_KR0 = _CKV0 + C_KV_RANK
_QK_SW0 = _KR0 + C_QK_PAD
IN_COLS_PACKED = _QK_SW0 + _QK_A


def _params(sem):
    return pltpu.CompilerParams(dimension_semantics=sem, vmem_limit_bytes=VMEM_LIMIT_BYTES)


def _rms(x):
    return x * lax.rsqrt(jnp.mean(x * x, axis=-1, keepdims=True) + EPS)


def _rope_partner(x):
    return pltpu.roll(x, x.shape[-1] - C_ROPE, x.ndim - 1)


def _premix_kernel(h_ref, *refs, c_scale):
    _premix_body(h_ref[0], *refs, c_scale=c_scale)


def _ple_premix_kernel(h1_ref, y_ref, p_ref, gple_ref, wgate_ref, wproj_ref, *refs, c_scale):
    *premix_refs, h_ref = refs
    h = h1_ref[0] + y_ref[0].astype(F32)
    r = (_rms(h) * gple_ref[...]).astype(BF16)
    gate = jax.nn.sigmoid(jnp.dot(r, wgate_ref[...], preferred_element_type=F32))
    emb = jnp.dot(p_ref[0, 0].astype(BF16), wproj_ref[...], preferred_element_type=F32)
    h = h + gate * emb
    h_ref[0] = h
    _premix_body(h, *premix_refs, c_scale=c_scale)


def _premix_body(h, g_mix_ref, w_in_ref, hsum_ref, g_qk_ref, g_qk_sw_ref, cosa_ref, sina_ref,
                 g_cq_ref, w_qup_ref, g_ckv_ref, w_kvup_ref, cosc_ref, sinc_ref,
                 qat_ref, ka_ref, vat_ref, ub_ref, qct_ref, kc_ref, vct_ref, *, c_scale):
    a = _rms(h) * g_mix_ref[...]
    z = jnp.dot(a.astype(BF16), w_in_ref[...], preferred_element_type=F32)

    qk = z[:, :_QK_A]
    qk_sw = z[:, _QK_SW0:]
    ss = jnp.dot((qk * qk).astype(BF16), hsum_ref[...], preferred_element_type=F32)
    reps = _QK_A // LANES
    cos_a = jnp.concatenate([cosa_ref[...]] * reps, axis=1) * g_qk_ref[...]
    sin_a = jnp.concatenate([sina_ref[...]] * reps, axis=1) * g_qk_sw_ref[...]
    qk = lax.rsqrt(ss * (1.0 / HEAD_DIM) + EPS) * (qk * cos_a + qk_sw * sin_a)
    qat = qk[:, :A_WIDTH].T.astype(BF16)
    tt = qat.shape[1]
    a_group = A_HEADS // A_KV_HEADS
    for hh in range(A_HEADS):
        for qi in range(tt // Q_TILE_GQA):
            col = (qi * a_group + hh % a_group) * Q_TILE_GQA
            qat_ref[0, hh // a_group, :, col:col + Q_TILE_GQA] = qat[
                hh * HEAD_DIM:(hh + 1) * HEAD_DIM, qi * Q_TILE_GQA:(qi + 1) * Q_TILE_GQA]
    ka = qk[:, A_WIDTH:].astype(BF16)
    for hh in range(A_KV_HEADS):
        ka_ref[0, hh] = ka[:, hh * HEAD_DIM:(hh + 1) * HEAD_DIM]
    ones_rows = (lax.broadcasted_iota(jnp.int32, (V_PAD, tt), 0) == 0).astype(BF16)
    vt = z[:, _V_A0:_U_B0].T.astype(BF16)
    for hh in range(A_KV_HEADS):
        vat_ref[0, hh, 0] = jnp.concatenate([vt[hh * HEAD_DIM:(hh + 1) * HEAD_DIM], ones_rows], axis=0)

    ub_ref[0] = z[:, _U_B0:_CQ0]

    cqn = _rms(z[:, _CQ0:_CKV0]) * g_cq_ref[...]
    qc = jnp.dot(cqn.astype(BF16), w_qup_ref[...], preferred_element_type=F32)
    cos_c = cosc_ref[...]
    sin_c = sinc_ref[...]
    cos_q = jnp.concatenate([cos_c * c_scale] * C_HEADS, axis=1)
    sin_q = jnp.concatenate([sin_c * c_scale] * C_HEADS, axis=1)
    qct = (qc * cos_q + _rope_partner(qc) * sin_q).T.astype(BF16)
    ckvn = _rms(z[:, _CKV0:_KR0]) * g_ckv_ref[...]
    kv = jnp.dot(ckvn.astype(BF16), w_kvup_ref[...], preferred_element_type=F32)
    kr = z[:, _KR0:_QK_SW0]
    kr = kr * cos_c + _rope_partner(kr) * sin_c
    for hh in range(C_HEADS):
        qct_ref[0, hh] = qct[hh * C_QK_PAD:(hh + 1) * C_QK_PAD]
        kc_ref[0, hh] = (kv[:, hh * C_QK_PAD:(hh + 1) * C_QK_PAD] + kr).astype(BF16)
    vct = kv[:, C_HEADS * C_QK_PAD:].T.astype(BF16)
    for hh in range(C_HEADS):
        vct_ref[0, hh, 0] = jnp.concatenate([vct[hh * C_V:(hh + 1) * C_V], ones_rows], axis=0)


def _premix(h, lw, tabs, ple=None):
    B, S, D = (h if ple is None else ple[0]).shape
    Tt = TOKEN_TILE
    nt = S // Tt
    tile = lambda b, i: (b, i, 0)
    const2 = lambda b, i: (0, 0)
    headmajor = lambda b, i: (b, 0, i, 0)
    headmajor_t = lambda b, i: (b, 0, 0, i)
    vt_map = lambda b, i: (b, 0, i, 0, 0)
    tab_map = lambda b, i: (i, 0)
    in_specs = [
        pl.BlockSpec((1, Tt, D), tile),
        pl.BlockSpec((1, D), const2),
        pl.BlockSpec((D, IN_COLS_PACKED), const2),
        pl.BlockSpec((_QK_A, _QK_A), const2),
        pl.BlockSpec((1, _QK_A), const2),
        pl.BlockSpec((1, _QK_A), const2),
        pl.BlockSpec((Tt, LANES), tab_map),
        pl.BlockSpec((Tt, LANES), tab_map),
        pl.BlockSpec((1, C_Q_RANK), const2),
        pl.BlockSpec((C_Q_RANK, C_HEADS * C_QK_PAD), const2),
        pl.BlockSpec((1, C_KV_RANK), const2),
        pl.BlockSpec((C_KV_RANK, C_HEADS * C_QK_PAD + C_WIDTH), const2),
        pl.BlockSpec((Tt, LANES), tab_map),
        pl.BlockSpec((Tt, LANES), tab_map),
    ]
    out_shape = [
        jax.ShapeDtypeStruct((B, A_KV_HEADS, HEAD_DIM, S * (A_HEADS // A_KV_HEADS)), BF16),
        jax.ShapeDtypeStruct((B, A_KV_HEADS, S, HEAD_DIM), BF16),
        jax.ShapeDtypeStruct((B, A_KV_HEADS, nt, HEAD_DIM + V_PAD, Tt), BF16),
        jax.ShapeDtypeStruct((B, S, B_WIDTH), F32),
        jax.ShapeDtypeStruct((B, C_HEADS, C_QK_PAD, S), BF16),
        jax.ShapeDtypeStruct((B, C_HEADS, S, C_QK_PAD), BF16),
        jax.ShapeDtypeStruct((B, C_HEADS, nt, C_V + V_PAD, Tt), BF16),
    ]
    out_specs = [
        pl.BlockSpec((1, A_KV_HEADS, HEAD_DIM, Tt * (A_HEADS // A_KV_HEADS)), headmajor_t),
        pl.BlockSpec((1, A_KV_HEADS, Tt, HEAD_DIM), headmajor),
        pl.BlockSpec((1, A_KV_HEADS, 1, HEAD_DIM + V_PAD, Tt), vt_map),
        pl.BlockSpec((1, Tt, B_WIDTH), tile),
        pl.BlockSpec((1, C_HEADS, C_QK_PAD, Tt), headmajor_t),
        pl.BlockSpec((1, C_HEADS, Tt, C_QK_PAD), headmajor),
        pl.BlockSpec((1, C_HEADS, 1, C_V + V_PAD, Tt), vt_map),
    ]
    c_scale = (C_NOPE + C_ROPE) ** -0.5 * LOG2E
    if ple is None:
        kern = functools.partial(_premix_kernel, c_scale=c_scale)
        first = (h,)
    else:
        h1, y, p, layer, plw = ple
        kern = functools.partial(_ple_premix_kernel, c_scale=c_scale)
        first = (h1, y, p, plw["g_ple"], plw["w_ple_gate"], plw["w_ple_proj"])
        in_specs = [
            pl.BlockSpec((1, Tt, D), tile),
            pl.BlockSpec((1, Tt, D), tile),
            pl.BlockSpec((1, 1, Tt, PLE_DIM), lambda b, i: (layer, b, i, 0)),
            pl.BlockSpec((1, D), const2),
            pl.BlockSpec((D, D), const2),
            pl.BlockSpec((PLE_DIM, D), const2),
        ] + in_specs[1:]
        out_shape = out_shape + [jax.ShapeDtypeStruct((B, S, D), F32)]
        out_specs = out_specs + [pl.BlockSpec((1, Tt, D), tile)]
    return pl.pallas_call(
        kern, grid=(B, nt), in_specs=in_specs, out_specs=out_specs, out_shape=out_shape,
        compiler_params=_params(("parallel", "parallel")), name="premix",
    )(*first, lw["g_mix"], lw["w_in"], tabs["hsum"], lw["g_qk"], lw["g_qk_sw"], tabs["cos_a"], tabs["sin_a"],
      lw["g_cq"], lw["w_q_up"], lw["g_ckv"], lw["w_kv_up"], tabs["cos_c"], tabs["sin_c"])


def _attn_kernel(qt_ref, k_ref, vt_ref, o_ref, s_buf, p_buf, *, order, n_chunks, chunk, sub, dv, group):
    n_streams, _, width = qt_ref.shape[1:]
    rows = vt_ref.shape[3]
    nsub = chunk // sub
    n_pos = len(order)
    per_body = 1 + max(coff for _, coff in order)
    assert LOOKAHEAD < n_pos and n_chunks % per_body == 0

    def score_piece(pos, j, r):
        si = order[pos][0]
        start = pl.multiple_of(j * chunk + r * sub, sub)
        s = jnp.dot(k_ref[0, si, pl.ds(start, sub), :], qt_ref[0, si], preferred_element_type=F32)
        s_buf[pos, r * sub:(r + 1) * sub, :] = s
        return jnp.max(s, axis=0, keepdims=True)

    def prob_piece(pos, r, m_b):
        s = s_buf[pos, r * sub:(r + 1) * sub, :]
        p_buf[pos, r * sub:(r + 1) * sub, :] = jnp.exp2(s - m_b).astype(BF16)

    def all_scores(pos, j):
        cm = None
        for r in range(nsub):
            c = score_piece(pos, j, r)
            cm = c if cm is None else jnp.maximum(cm, c)
        return cm

    def body(b, carry):
        pending, state = carry
        colmax = dict(enumerate(pending))
        state = list(state)
        for i, (si, coff) in enumerate(order):
            m, acc = state[si]
            m_new = jnp.maximum(m, colmax[i])
            m_b = jnp.broadcast_to(m_new, (sub, width))
            tpos = (i + LOOKAHEAD) % n_pos
            tj = jnp.minimum((b + (i + LOOKAHEAD) // n_pos) * per_body + order[tpos][1], n_chunks - 1)
            cm = None
            for r in range(nsub):
                c = score_piece(tpos, tj, r)
                cm = c if cm is None else jnp.maximum(cm, c)
                prob_piece(i, r, m_b)
            colmax[i + LOOKAHEAD] = cm
            pv = jnp.dot(vt_ref[0, si, b * per_body + coff], p_buf[i], preferred_element_type=F32)
            state[si] = (m_new, jnp.exp2(m - m_new) * acc + pv)
        return tuple(colmax[n_pos + k] for k in range(LOOKAHEAD)), tuple(state)

    pending = tuple(all_scores(k, order[k][1]) for k in range(LOOKAHEAD))
    init = tuple((jnp.full((1, width), -jnp.inf, F32), jnp.zeros((rows, width), F32)) for _ in range(n_streams))
    _, final = lax.fori_loop(0, n_chunks // per_body, body, (pending, init))
    tq = width // group
    outs = []
    for _, acc in final:
        out_t = acc[:dv] * (1.0 / acc[dv:dv + 1])
        outs += [out_t[:, g * tq:(g + 1) * tq] for g in range(group)]
    o_ref[0] = jnp.concatenate(outs, axis=0).T


def _attention(qt, k, vt, *, tq, group, dv):
    B, Hkv, dk, _ = qt.shape
    S = k.shape[2]
    nk, rows, Tk = vt.shape[2:]
    n_streams = ATTN_STREAMS
    width = group * tq
    per_body = math.gcd(CHUNKS_PER_BODY, nk)
    order = tuple((si, coff) for coff in range(per_body) for si in range(n_streams))
    kern = functools.partial(_attn_kernel, order=order, n_chunks=nk, chunk=Tk, sub=KEY_SUB, dv=dv, group=group)
    return pl.pallas_call(
        kern, grid=(B, Hkv // n_streams, S // tq),
        in_specs=[
            pl.BlockSpec((1, n_streams, dk, width), lambda b, g, i: (b, g, 0, i)),
            pl.BlockSpec((1, n_streams, S, dk), lambda b, g, i: (b, g, 0, 0)),
            pl.BlockSpec((1, n_streams, nk, rows, Tk), lambda b, g, i: (b, g, 0, 0, 0)),
        ],
        out_specs=pl.BlockSpec((1, tq, n_streams * group * dv), lambda b, g, i: (b, i, g)),
        out_shape=jax.ShapeDtypeStruct((B, S, Hkv * group * dv), F32),
        scratch_shapes=[pltpu.VMEM((len(order), Tk, width), F32), pltpu.VMEM((len(order), Tk, width), BF16)],
        compiler_params=_params(("parallel", "parallel", "arbitrary")), name="attention",
    )(qt, k, vt)


def _route(logits):
    tt = logits.shape[1]
    le = logits[:N_EXPERTS]
    lg = logits[N_EXPERTS:]
    row8 = lax.broadcasted_iota(jnp.int32, (ROUTER_ROWS - N_EXPERTS, tt), 0)
    lg = jnp.where(row8 < MOE_GROUPS, lg, -jnp.inf)
    gmax = jnp.max(lg, axis=0, keepdims=True)
    gsel = jnp.min(jnp.where(lg == gmax, row8, MOE_GROUPS), axis=0, keepdims=True)
    gprob = 1.0 / jnp.sum(jnp.exp(lg - gmax), axis=0, keepdims=True)
    row = lax.broadcasted_iota(jnp.int32, (N_EXPERTS, tt), 0)
    lm = jnp.where((row // EXPERTS_PER_GROUP) == gsel, le, -jnp.inf)
    m1 = jnp.max(lm, axis=0, keepdims=True)
    i1 = jnp.min(jnp.where(lm == m1, row, N_EXPERTS), axis=0, keepdims=True)
    lm2 = jnp.where(row == i1, -jnp.inf, lm)
    m2 = jnp.max(lm2, axis=0, keepdims=True)
    i2 = jnp.min(jnp.where(lm2 == m2, row, N_EXPERTS), axis=0, keepdims=True)
    r = jnp.exp(m2 - m1)
    w1 = gprob / (1.0 + r)
    w2 = w1 * r
    return i1, i2, w1, w2


def _dispatch_plan(i1, i2, w1, w2, before_ref, lower_ref):
    tt = i1.shape[1]
    row = lax.broadcasted_iota(jnp.int32, (N_EXPERTS, tt), 0)
    hit1 = row == i1
    hit2 = row == i2
    assign = jnp.where(hit1 | hit2, 1.0, 0.0)
    rank = jnp.dot(assign.astype(BF16), before_ref[...], preferred_element_type=F32)
    pos, offs, counts = [], [], []
    for u in range(tt // MOE_SUB):
        lanes = slice(u * MOE_SUB, (u + 1) * MOE_SUB)
        count = jnp.sum(assign[:, lanes], axis=1, keepdims=True)
        segs = jnp.floor((count + (SEG_ALIGN - 1.0)) * (1.0 / SEG_ALIGN))
        segs_b = jnp.broadcast_to(segs, (N_EXPERTS, LANES)).astype(BF16)
        off = SEG_ALIGN * jnp.dot(lower_ref[...], segs_b, preferred_element_type=F32)
        pos.append(off[:, :1] + rank[:, lanes])
        offs.append(off.astype(jnp.int32))
        counts.append(jnp.broadcast_to(count, (N_EXPERTS, LANES)).astype(jnp.int32))
    pos = jnp.concatenate(pos, axis=1)
    pos1 = jnp.sum(jnp.where(hit1, pos, 0.0), axis=0, keepdims=True)
    pos2 = jnp.sum(jnp.where(hit2, pos, 0.0), axis=0, keepdims=True)
    route = jnp.concatenate([pos1, pos2, w1, w2, jnp.zeros((4, tt), F32)], axis=0)
    return route, offs, counts


def _postmix_kernel(h_ref, oa_ref, oc_ref, ub_ref, prev_ref, next_ref, wpool_ref, spool_ref,
                    ga_ref, gb_ref, gc_ref, wout_ref, gffn_ref, wr_ref, before_ref, lower_ref,
                    h1_ref, m_ref, route_ref, offs_ref, counts_ref, ext_ref, *, seq_len):
    i = pl.program_id(1)
    nt = pl.num_programs(1)
    tt = h_ref.shape[1]
    half = B_WIDTH // 2

    u = ub_ref[0]
    ext_ref[POOL_HALO:POOL_HALO + tt, :] = u
    ext_ref[:POOL_HALO, :] = jnp.where(i > 0, prev_ref[0, 0], 0.0)
    ext_ref[POOL_HALO + tt:, :] = jnp.where(i < nt - 1, next_ref[0, 0], 0.0)

    def window(lo, hi, lanes):
        acc = None
        for d in range(lo, hi):
            piece = ext_ref[POOL_HALO + d:POOL_HALO + d + tt, lanes]
            acc = piece if acc is None else acc + piece
        return acc

    lo_lanes = slice(0, half)
    hi_lanes = slice(half, B_WIDTH)
    w2 = window(-1, 1, lo_lanes)
    w4 = w2 + window(-2, -1, lo_lanes) + window(1, 2, lo_lanes)
    w8 = window(-4, 4, hi_lanes)
    w16 = w8 + window(-8, -4, hi_lanes) + window(4, 8, hi_lanes)
    lane = lax.broadcasted_iota(jnp.int32, (tt, half), 1)
    first = lane < B_GROUP_DIM
    sums = jnp.concatenate([jnp.where(first, w2, w4), jnp.where(first, w8, w16)], axis=1)
    pos = i * tt + lax.broadcasted_iota(jnp.int32, (tt, B_WIDTH), 0)
    lane_b = lax.broadcasted_iota(jnp.int32, (tt, B_WIDTH), 1)
    hw = jnp.left_shift(1, lane_b // B_GROUP_DIM)
    cnt = jnp.minimum(pos + hw, seq_len) - jnp.maximum(pos - hw, 0)
    dlt = sums / cnt.astype(F32) - u
    ob = jnp.dot(dlt.astype(BF16), wpool_ref[...], preferred_element_type=F32) * spool_ref[...]

    merged = jnp.concatenate([
        (_rms(oa_ref[0]) * ga_ref[...]).astype(BF16),
        (_rms(ob) * gb_ref[...]).astype(BF16),
        (_rms(oc_ref[0]) * gc_ref[...]).astype(BF16)], axis=1)
    h1 = h_ref[0] + jnp.dot(merged, wout_ref[...], preferred_element_type=F32)
    h1_ref[0] = h1

    m = _rms(h1) * gffn_ref[...]
    m_hi = m.astype(BF16)
    m_ref[0] = m_hi
    m_lo = (m - m_hi.astype(F32)).astype(BF16)
    nt_dims = (((1,), (1,)), ((), ()))
    both = lax.dot_general(wr_ref[...], m_hi, nt_dims, preferred_element_type=F32)
    logits = (both[:ROUTER_ROWS] + both[ROUTER_ROWS:]
              + lax.dot_general(wr_ref[:ROUTER_ROWS, :], m_lo, nt_dims, preferred_element_type=F32))
    route, offs, counts = _dispatch_plan(*_route(logits), before_ref, lower_ref)
    for u in range(len(offs)):
        route_ref[0, 0, u] = route[:, u * MOE_SUB:(u + 1) * MOE_SUB]
        offs_ref[0, 0, u] = offs[u]
        counts_ref[0, 0, u] = counts[u]


def _postmix(h, oa, oc, ub, lw, tabs):
    B, S, D = h.shape
    Tt = min(MOE_TILE, S)
    assert Tt % MOE_SUB == 0
    nt = S // Tt
    rows = Tt // POOL_HALO
    ub_rows = ub.reshape(B, S // POOL_HALO, POOL_HALO, B_WIDTH)
    tile = lambda b, i: (b, i, 0)
    const2 = lambda b, i: (0, 0)
    in_specs = [
        pl.BlockSpec((1, Tt, D), tile),
        pl.BlockSpec((1, Tt, A_WIDTH), tile),
        pl.BlockSpec((1, Tt, C_WIDTH), tile),
        pl.BlockSpec((1, Tt, B_WIDTH), tile),
        pl.BlockSpec((1, 1, POOL_HALO, B_WIDTH), lambda b, i: (b, jnp.maximum(i * rows - 1, 0), 0, 0)),
        pl.BlockSpec((1, 1, POOL_HALO, B_WIDTH),
                     lambda b, i: (b, jnp.minimum((i + 1) * rows, S // POOL_HALO - 1), 0, 0)),
        pl.BlockSpec((B_WIDTH, B_WIDTH), const2),
        pl.BlockSpec((1, B_WIDTH), const2),
        pl.BlockSpec((1, A_WIDTH), const2),
        pl.BlockSpec((1, B_WIDTH), const2),
        pl.BlockSpec((1, C_WIDTH), const2),
        pl.BlockSpec((D, D), const2),
        pl.BlockSpec((1, D), const2),
        pl.BlockSpec((2 * ROUTER_ROWS, D), const2),
        pl.BlockSpec((Tt, Tt), const2),
        pl.BlockSpec((N_EXPERTS, N_EXPERTS), const2),
    ]
    per_tile = lambda b, i: (b, i, 0, 0, 0)
    n_sub = Tt // MOE_SUB
    out_shape = [
        jax.ShapeDtypeStruct((B, S, D), F32),
        jax.ShapeDtypeStruct((B, S, D), BF16),
        jax.ShapeDtypeStruct((B, nt, n_sub, 8, MOE_SUB), F32),
        jax.ShapeDtypeStruct((B, nt, n_sub, N_EXPERTS, LANES), jnp.int32),
        jax.ShapeDtypeStruct((B, nt, n_sub, N_EXPERTS, LANES), jnp.int32),
    ]
    out_specs = [
        pl.BlockSpec((1, Tt, D), tile),
        pl.BlockSpec((1, Tt, D), tile),
        pl.BlockSpec((1, 1, n_sub, 8, MOE_SUB), per_tile),
        pl.BlockSpec((1, 1, n_sub, N_EXPERTS, LANES), per_tile),
        pl.BlockSpec((1, 1, n_sub, N_EXPERTS, LANES), per_tile),
    ]
    return pl.pallas_call(
        functools.partial(_postmix_kernel, seq_len=S), grid=(B, nt),
        in_specs=in_specs, out_specs=out_specs, out_shape=out_shape,
        scratch_shapes=[pltpu.VMEM((Tt + 2 * POOL_HALO, B_WIDTH), F32)],
        compiler_params=_params(("parallel", "parallel")), name="postmix",
    )(h, oa, oc, ub, ub_rows, ub_rows, lw["w_pool"], lw["s_pool"], lw["g_out_a"], lw["g_out_b"],
      lw["g_out_c"], lw["w_out"], lw["g_ffn"], lw["w_router"],
      tabs["before"], tabs["lower"])


def _moe_kernel(plan_ref, m_ref, route_ref, wgu_ref, wd_ref, o_ref, perm_ref, xy_ref, ws_ref, *, n_rows):
    g = pl.program_id(0)
    e = pl.program_id(1)
    n_tiles, _, tm = route_ref.shape
    d = m_ref.shape[1]

    @pl.when(e == 0)
    def _():
        for k in range(n_tiles):
            route = route_ref[k]
            pos1 = route[0:1].astype(jnp.int32)
            pos2 = route[1:2].astype(jnp.int32)
            w1 = route[2:3]
            w2 = route[3:4]
            m = m_ref[k * tm:(k + 1) * tm, :]

            def strip(i, _, k=k, pos1=pos1, pos2=pos2, w1=w1, w2=w2, m=m):
                r0 = pl.multiple_of(i * DISPATCH_STRIP, DISPATCH_STRIP)
                rows = r0 + lax.broadcasted_iota(jnp.int32, (DISPATCH_STRIP, tm), 0)
                hit1 = rows == pos1
                hit2 = rows == pos2
                perm = jnp.where(hit1 | hit2, 1.0, 0.0).astype(BF16)
                perm_ref[k, pl.ds(r0, DISPATCH_STRIP), :] = perm
                ws_ref[k, pl.ds(r0, DISPATCH_STRIP), :] = jnp.sum(
                    jnp.where(hit1, w1, jnp.where(hit2, w2, 0.0)), axis=1, keepdims=True)
                xy_ref[k, pl.ds(r0, DISPATCH_STRIP), :] = jnp.dot(
                    perm, m, preferred_element_type=F32).astype(BF16)
                return 0

            lax.fori_loop(0, n_rows // DISPATCH_STRIP, strip, 0, unroll=True)
            xy_ref[k, n_rows:, :] = jnp.zeros((EXPERT_BLOCK, d), BF16)
            ws_ref[k, n_rows:, :] = jnp.zeros((EXPERT_BLOCK, 1), F32)

    n_exp = wgu_ref.shape[0]
    plans = [[(plan_ref[g * n_tiles + k, e * n_exp + j], plan_ref[g * n_tiles + k, N_EXPERTS + e * n_exp + j])
              for k in range(n_tiles)] for j in range(n_exp)]

    def blocks(bi):
        starts = [[pl.multiple_of(jnp.minimum(off + bi * EXPERT_BLOCK, n_rows), SEG_ALIGN) for off, _ in pj]
                  for pj in plans]
        xs = [[xy_ref[k, pl.ds(r0, EXPERT_BLOCK), :] for k, r0 in enumerate(sj)] for sj in starts]
        ws = [[ws_ref[k, pl.ds(r0, EXPERT_BLOCK), :] for k, r0 in enumerate(sj)] for sj in starts]
        gus = [jnp.dot(jnp.concatenate(xs[j], axis=0), wgu_ref[j], preferred_element_type=F32)
               for j in range(n_exp)]
        hids = [gu[:, :D_EXPERT] * jax.nn.sigmoid(gu[:, :D_EXPERT]) * gu[:, D_EXPERT:]
                * jnp.concatenate(ws[j], axis=0) for j, gu in enumerate(gus)]
        ys = [jnp.dot(hid.astype(BF16), wd_ref[j], preferred_element_type=F32).astype(BF16)
              for j, hid in enumerate(hids)]
        row = lax.broadcasted_iota(jnp.int32, (EXPERT_BLOCK, 1), 0)
        for j in range(n_exp):
            for k, ((_, cnt), r0, x) in enumerate(zip(plans[j], starts[j], xs[j])):
                in_segment = row < cnt - bi * EXPERT_BLOCK
                xy_ref[k, pl.ds(r0, EXPERT_BLOCK), :] = jnp.where(
                    in_segment, ys[j][k * EXPERT_BLOCK:(k + 1) * EXPERT_BLOCK], x)

    blocks(0)
    longest = functools.reduce(jnp.maximum, [cnt for pj in plans for _, cnt in pj])
    n_blocks = lax.shift_right_logical(longest + (EXPERT_BLOCK - 1), EXPERT_BLOCK.bit_length() - 1)

    def more(bi, _):
        blocks(bi)
        return 0

    lax.fori_loop(1, n_blocks, more, 0)

    @pl.when(e == pl.num_programs(1) - 1)
    def _():
        tn_dims = (((0,), (0,)), ((), ()))
        for k in range(n_tiles):
            y = lax.dot_general(perm_ref[k], xy_ref[k, :n_rows, :], tn_dims, preferred_element_type=F32)
            o_ref[k * tm:(k + 1) * tm, :] = y.astype(BF16)


def _moe(m, route, plan, lw):
    T, D = m.shape
    n_all, _, Tm = route.shape
    group = math.gcd(MOE_TILE_GROUP, n_all)
    n_rows = 2 * Tm + N_EXPERTS * SEG_ALIGN
    assert n_rows % DISPATCH_STRIP == 0
    tile = lambda g, e, plan: (g, 0)
    grid_spec = pltpu.PrefetchScalarGridSpec(
        num_scalar_prefetch=1, grid=(n_all // group, N_EXPERTS // EXPERTS_PER_STEP),
        in_specs=[
            pl.BlockSpec((group * Tm, D), tile),
            pl.BlockSpec((group, 8, Tm), lambda g, e, plan: (g, 0, 0)),
            pl.BlockSpec((EXPERTS_PER_STEP, D, 2 * D_EXPERT), lambda g, e, plan: (e, 0, 0)),
            pl.BlockSpec((EXPERTS_PER_STEP, D_EXPERT, D), lambda g, e, plan: (e, 0, 0)),
        ],
        out_specs=pl.BlockSpec((group * Tm, D), tile),
        scratch_shapes=[
            pltpu.VMEM((group, n_rows, Tm), BF16),
            pltpu.VMEM((group, n_rows + EXPERT_BLOCK, D), BF16),
            pltpu.VMEM((group, n_rows + EXPERT_BLOCK, 1), F32),
        ],
    )
    return pl.pallas_call(
        functools.partial(_moe_kernel, n_rows=n_rows), grid_spec=grid_spec,
        out_shape=jax.ShapeDtypeStruct((T, D), BF16),
        compiler_params=_params(("parallel", "arbitrary")), name="moe",
    )(plan, m, route, lw["w_gate_up"], lw["w_down"])


def _ple_kernel(h1_ref, y_ref, p_ref, gple_ref, wgate_ref, wproj_ref, gfinal_ref, o_ref, *, final):
    h = h1_ref[0] + y_ref[0].astype(F32)
    r = (_rms(h) * gple_ref[...]).astype(BF16)
    gate = jax.nn.sigmoid(jnp.dot(r, wgate_ref[...], preferred_element_type=F32))
    emb = jnp.dot(p_ref[0, 0].astype(BF16), wproj_ref[...], preferred_element_type=F32)
    out = h + gate * emb
    if final:
        out = _rms(out) * gfinal_ref[...]
    o_ref[0] = out


def _ple(h1, y, p, layer, lw, g_final, final):
    B, S, D = h1.shape
    Tt = TOKEN_TILE
    tile = lambda b, i: (b, i, 0)
    const2 = lambda b, i: (0, 0)
    return pl.pallas_call(
        functools.partial(_ple_kernel, final=final), grid=(B, S // Tt),
        in_specs=[
            pl.BlockSpec((1, Tt, D), tile),
            pl.BlockSpec((1, Tt, D), tile),
            pl.BlockSpec((1, 1, Tt, PLE_DIM), lambda b, i: (layer, b, i, 0)),
            pl.BlockSpec((1, D), const2),
            pl.BlockSpec((D, D), const2),
            pl.BlockSpec((PLE_DIM, D), const2),
            pl.BlockSpec((1, D), const2),
        ],
        out_specs=pl.BlockSpec((1, Tt, D), tile),
        out_shape=jax.ShapeDtypeStruct((B, S, D), F32),
        compiler_params=_params(("parallel", "parallel")), name="ple",
    )(h1, y, p, lw["g_ple"], lw["w_ple_gate"], lw["w_ple_proj"], g_final)


def _rope_tables(seq_len):
    rows_n = seq_len // GRID_W
    row = jnp.repeat(jnp.arange(rows_n), GRID_W).astype(F32)[:, None]
    col = jnp.tile(jnp.arange(GRID_W), rows_n).astype(F32)[:, None]

    def axial(rot_dim):
        n_freq = rot_dim // 4
        inv = ROPE_THETA ** (-jnp.arange(n_freq, dtype=F32) / n_freq)
        ar, ac = row * inv, col * inv
        cos = jnp.concatenate([jnp.cos(ar), jnp.cos(ar), jnp.cos(ac), jnp.cos(ac)], axis=1)
        sin = jnp.concatenate([-jnp.sin(ar), jnp.sin(ar), -jnp.sin(ac), jnp.sin(ac)], axis=1)
        return cos, sin

    cos_a, sin_a = axial(HEAD_DIM)
    cos_r, sin_r = axial(C_ROPE)
    ones = jnp.ones((seq_len, C_NOPE), F32)
    zeros = jnp.zeros((seq_len, C_NOPE), F32)
    pad = jnp.zeros((seq_len, C_QK_PAD - C_NOPE - C_ROPE), F32)
    return dict(
        cos_a=jnp.concatenate([cos_a, cos_a], axis=1), sin_a=jnp.concatenate([sin_a, sin_a], axis=1),
        cos_c=jnp.concatenate([ones, cos_r, pad], axis=1), sin_c=jnp.concatenate([zeros, sin_r, pad], axis=1),
    )


def _pack_weights(w):
    L = DEPTH
    w_in = w["w_in"]
    def with_swapped(rope):
        parts = rope.reshape(rope.shape[:-1] + (2, 2, C_ROPE // 4))
        return jnp.concatenate([rope, parts[..., ::-1, :].reshape(rope.shape)], axis=-1)

    kr_pad = jnp.pad(with_swapped(w_in[:, :, _KR0:]), ((0, 0), (0, 0), (C_NOPE, 0)))
    q_scale = jnp.full((A_WIDTH,), HEAD_DIM ** -0.5 * LOG2E, F32)
    k_scale = jnp.ones((A_KV_HEADS * HEAD_DIM,), F32)
    g_qk = jnp.concatenate([jnp.tile(w["g_qa"], (1, A_HEADS)), jnp.tile(w["g_ka"], (1, A_KV_HEADS))], axis=1)
    g_qk = g_qk * jnp.concatenate([q_scale, k_scale])[None, :]

    def halves_swapped(cols):
        parts = cols.reshape(cols.shape[:-1] + (_QK_A // (HEAD_DIM // 2), 2, HEAD_DIM // 4))
        return parts[..., ::-1, :].reshape(cols.shape)
    w_q_up = w["w_q_up"].reshape(L, C_Q_RANK, C_HEADS, C_NOPE + C_ROPE)
    w_q_up = jnp.concatenate([w_q_up[..., :C_NOPE], with_swapped(w_q_up[..., C_NOPE:])], axis=-1)
    kv = w["w_kv_up"].reshape(L, C_KV_RANK, C_HEADS, C_NOPE + C_V)
    k_nope = jnp.pad(kv[..., :C_NOPE], ((0, 0), (0, 0), (0, 0), (0, C_QK_PAD - C_NOPE)))
    w_kv_up = jnp.concatenate([k_nope.reshape(L, C_KV_RANK, C_HEADS * C_QK_PAD),
                               kv[..., C_NOPE:].reshape(L, C_KV_RANK, C_WIDTH)], axis=2)
    eye = jnp.eye(B_GROUPS, dtype=F32)
    w_pool = (w["w_pool"][:, :, :, None, :] * eye[None, :, None, :, None]).reshape(L, B_WIDTH, B_WIDTH)
    w_router = jnp.concatenate([w["w_router_expert"], w["w_router_group"]], axis=2)
    w_router = jnp.pad(jnp.swapaxes(w_router, 1, 2), ((0, 0), (0, ROUTER_ROWS - N_EXPERTS - MOE_GROUPS), (0, 0)))
    w_router_hi = w_router.astype(BF16)
    w_router_lo = (w_router - w_router_hi.astype(F32)).astype(BF16)
    row = lambda a: a[:, None, :]
    return dict(
        g_mix=row(w["g_mix"]),
        w_in=jnp.concatenate([w_in[:, :, :_KR0], kr_pad, halves_swapped(w_in[:, :, :_QK_A])], axis=2).astype(BF16),
        g_qk=row(g_qk), g_qk_sw=row(halves_swapped(g_qk)),
        g_cq=row(w["g_cq"]), w_q_up=w_q_up.reshape(L, C_Q_RANK, C_HEADS * C_QK_PAD).astype(BF16),
        g_ckv=row(w["g_ckv"]), w_kv_up=w_kv_up.astype(BF16),
        w_pool=w_pool.astype(BF16), s_pool=row(w["s_pool"]),
        g_out_a=row(w["g_out_a"]), g_out_b=row(w["g_out_b"]), g_out_c=row(w["g_out_c"]),
        w_out=w["w_out"].astype(BF16), g_ffn=row(w["g_ffn"]),
        w_router=jnp.concatenate([w_router_hi, w_router_lo], axis=1),
        w_gate_up=jnp.concatenate([w["w_gate"], w["w_up"]], axis=3).astype(BF16),
        w_down=w["w_down"].astype(BF16),
        g_ple=row(w["g_ple"]), w_ple_gate=w["w_ple_gate"].astype(BF16), w_ple_proj=w["w_ple_proj"].astype(BF16),
    )


def _trunk(x, p, packed, g_final):
    B, S, D = x.shape
    tabs = _rope_tables(S)
    hs = np.kron(np.eye(_QK_A // HEAD_DIM, dtype=np.float32), np.ones((HEAD_DIM, HEAD_DIM), np.float32))
    tabs["hsum"] = jnp.asarray(hs, BF16)
    tm = min(MOE_TILE, S)
    same_sub = np.kron(np.eye(tm // MOE_SUB, dtype=np.float32), np.ones((MOE_SUB, MOE_SUB), np.float32))
    tabs["before"] = jnp.asarray(np.triu(same_sub, 1), BF16)
    tabs["lower"] = jnp.asarray(np.tril(np.ones((N_EXPERTS, N_EXPERTS), np.float32), -1), BF16)
    g_final = g_final[None, :]
    h = x
    pending = None
    for i in range(DEPTH):
        lw = {name: a[i] for name, a in packed.items()}
        if pending is None:
            qat, ka, vat, ub, qct, kc, vct = _premix(h, lw, tabs)
        else:
            qat, ka, vat, ub, qct, kc, vct, h = _premix(None, lw, tabs, ple=pending)
        oa = _attention(qat, ka, vat, tq=Q_TILE_GQA, group=A_HEADS // A_KV_HEADS, dv=HEAD_DIM)
        oc = _attention(qct, kc, vct, tq=Q_TILE_MLA, group=1, dv=C_V)
        h1, m, route, offs, counts = _postmix(h, oa, oc, ub, lw, tabs)
        plan = jnp.concatenate([offs[..., 0], counts[..., 0]], axis=-1).reshape(-1, 2 * N_EXPERTS)
        y = _moe(m.reshape(B * S, D), route.reshape(-1, 8, MOE_SUB), plan, lw)
        pending = (h1, y.reshape(B, S, D), p, i, lw)
    return _ple(*pending, g_final, final=True)


def kernel(x_prompt, x_sample, p_prompt, p_sample, g_mix, w_in, g_qa, g_ka, w_pool, s_pool, g_cq, w_q_up, g_ckv, w_kv_up, g_out_a, g_out_b, g_out_c, w_out, g_ffn, w_router_group, w_router_expert, w_gate, w_up, w_down, g_ple, w_ple_gate, w_ple_proj, g_final):
    weights = dict(g_mix=g_mix, w_in=w_in, g_qa=g_qa, g_ka=g_ka, w_pool=w_pool, s_pool=s_pool, g_cq=g_cq,
                   w_q_up=w_q_up, g_ckv=g_ckv, w_kv_up=w_kv_up, g_out_a=g_out_a, g_out_b=g_out_b,
                   g_out_c=g_out_c, w_out=w_out, g_ffn=g_ffn, w_router_group=w_router_group,
                   w_router_expert=w_router_expert, w_gate=w_gate, w_up=w_up, w_down=w_down, g_ple=g_ple,
                   w_ple_gate=w_ple_gate, w_ple_proj=w_ple_proj)
    packed = _pack_weights(weights)
    return (_trunk(x_prompt, p_prompt, packed, g_final), _trunk(x_sample, p_sample, packed, g_final))
```
